```python
import jax, jax.numpy as jnp
from jax import lax
import numpy as np

D_MODEL = 2048
BATCH = 8
SEQ = 4096
DEPTH = 2
DEC_BATCH = 8
DEC_SEQ = 16
PAST_LEN = 4096

CHUNK = 64
Q_BLOCK = 128
N_MIXERS = 2
N_SB_LAYERS = (DEPTH + 1) // 2
N_GLA_LAYERS = DEPTH // 2
SB_HEADS = 16
SB_HEAD_DIM = D_MODEL // SB_HEADS
SB_WIDTH = SB_HEADS * SB_HEAD_DIM
GLA_HEADS = 4
GLA_DK = (D_MODEL // 2) // GLA_HEADS
GLA_DV = D_MODEL // GLA_HEADS
GLA_GATE_RANK = 16
GLA_TAU = 16.0
GLA_IN = 2 * GLA_HEADS * GLA_DK + 2 * GLA_HEADS * GLA_DV + GLA_GATE_RANK
EPS = 1e-6

kernel_name = "sb_gla_hybrid_stream_step"


def rmsnorm(x, g):
    xf = x.astype(jnp.float32)
    y = xf * lax.rsqrt(jnp.mean(xf * xf, axis=-1, keepdims=True) + EPS)
    return (y * g.astype(jnp.float32)).astype(x.dtype)


def sb_block(q, qpos, k, v, kpos):
    z = jnp.einsum('bqhd,bkhd->bhqk', q, k).astype(jnp.float32) * (SB_HEAD_DIM ** -0.5)
    mask = kpos[None, :] < qpos[:, None]
    neg_log_keep = jnp.where(mask, jax.nn.softplus(z), 0.0)
    between = lax.cumsum(neg_log_keep, axis=3, reverse=True) - neg_log_keep
    a = jnp.where(mask, jnp.exp(jax.nn.log_sigmoid(z) - between), 0.0)
    return jnp.einsum('bhqk,bkhd->bqhd', a, v.astype(jnp.float32))


def sb_mixer(h, w_in, w_out, past_k=None, past_v=None):
    B, T, _ = h.shape
    q, k, v, zg = jnp.split(h @ w_in, 4, axis=-1)
    shp = (B, T, SB_HEADS, SB_HEAD_DIM)
    q, k, v = q.reshape(shp), k.reshape(shp), v.reshape(shp)
    if past_k is None:
        nb = T // Q_BLOCK
        pos = jnp.arange(T, dtype=jnp.int32)
        qb = q.reshape(B, nb, Q_BLOCK, SB_HEADS, SB_HEAD_DIM).transpose(1, 0, 2, 3, 4)
        o = lax.map(lambda a: sb_block(a[0], a[1], k, v, pos), (qb, pos.reshape(nb, Q_BLOCK)))
        o = o.transpose(1, 0, 2, 3, 4).reshape(B, T, SB_WIDTH)
    else:
        P = past_k.shape[1]
        keys = jnp.concatenate([past_k, k], axis=1)
        vals = jnp.concatenate([past_v, v], axis=1)
        kpos = jnp.arange(P + T, dtype=jnp.int32)
        qpos = P + jnp.arange(T, dtype=jnp.int32)
        o = sb_block(q, qpos, keys, vals, kpos).reshape(B, T, SB_WIDTH)
    y = (o * jax.nn.silu(zg.astype(jnp.float32))).astype(h.dtype) @ w_out
    return y, k, v


def gla_chunk(S, inp):
    q, k, v, g = inp
    C = q.shape[1]
    b = jnp.cumsum(g, axis=1)
    b_last = b[:, -1]
    q_in = q * jnp.exp(b)
    a = jnp.einsum('bchd,bshd->bhcs', q_in, k * jnp.exp(-b))
    a = jnp.where(jnp.tril(jnp.ones((C, C), dtype=bool)), a, 0.0)
    o = jnp.einsum('bhcs,bshe->bche', a, v) + jnp.einsum('bchd,bhde->bche', q_in, S)
    S = jnp.exp(b_last)[..., None] * S + jnp.einsum('bshd,bshe->bhde', k * jnp.exp(b_last[:, None] - b), v)
    return S, o


def gla_mixer(h, w_in, w_a2, b_a, g_out, w_out, S0=None):
    B, T, _ = h.shape
    f32 = jnp.float32
    nk = GLA_HEADS * GLA_DK
    nv = GLA_HEADS * GLA_DV
    q, k, v, r, a_lr = jnp.split(h @ w_in, [nk, 2 * nk, 2 * nk + nv, 2 * nk + 2 * nv], axis=-1)
    q = q.astype(f32).reshape(B, T, GLA_HEADS, GLA_DK) * (GLA_DK ** -0.5)
    k = k.astype(f32).reshape(B, T, GLA_HEADS, GLA_DK)
    v = v.astype(f32).reshape(B, T, GLA_HEADS, GLA_DV)
    g = (jax.nn.log_sigmoid((a_lr @ w_a2 + b_a).astype(f32)) / GLA_TAU).reshape(B, T, GLA_HEADS, GLA_DK)
    S0 = jnp.zeros((B, GLA_HEADS, GLA_DK, GLA_DV), f32) if S0 is None else S0.astype(f32)
    C = min(CHUNK, T)
    nc = T // C
    to_chunks = lambda t: t.reshape(B, nc, C, *t.shape[2:]).swapaxes(0, 1)
    S, o = lax.scan(gla_chunk, S0, (to_chunks(q), to_chunks(k), to_chunks(v), to_chunks(g)))
    o = o.swapaxes(0, 1).reshape(B, T, GLA_HEADS, GLA_DV)
    o = o * lax.rsqrt(jnp.mean(o * o, axis=-1, keepdims=True) + EPS) * g_out.astype(f32).reshape(GLA_HEADS, GLA_DV)
    y = (o.reshape(B, T, nv) * jax.nn.silu(r.astype(f32))).astype(h.dtype) @ w_out
    return y, S.astype(h.dtype)


def trunk(x, c, w_ada, b_ada, norm_g, sb_w_in, sb_w_out, gla_w_in, gla_w_a2, gla_b_a,
          gla_norm_g, gla_w_out, final_norm_g, cache_sb_k=None, cache_sb_v=None, state_gla=None):
    new_k, new_v, new_s = [], [], []
    for i in range(DEPTH):
        j = i // N_MIXERS
        shift, scale, gate = jnp.split(c @ w_ada[i] + b_ada[i], 3, axis=-1)
        h = rmsnorm(x, norm_g[i]) * (1 + scale[:, None]) + shift[:, None]
        if i % N_MIXERS == 0:
            if cache_sb_k is None:
                y, k, v = sb_mixer(h, sb_w_in[j], sb_w_out[j])
            else:
                y, k, v = sb_mixer(h, sb_w_in[j], sb_w_out[j], cache_sb_k[j], cache_sb_v[j])
            new_k.append(k)
            new_v.append(v)
        else:
            S0 = None if state_gla is None else state_gla[j]
            y, S = gla_mixer(h, gla_w_in[j], gla_w_a2[j], gla_b_a[j], gla_norm_g[j], gla_w_out[j], S0)
            new_s.append(S)
        x = x + gate[:, None] * y
    return rmsnorm(x, final_norm_g), jnp.stack(new_k), jnp.stack(new_v), jnp.stack(new_s)


def setup_inputs(seed: int = 0) -> dict:
    key = jax.random.key(seed)
    ks = jax.random.split(key, 18)
    nrm = lambda k, shape, s: jax.random.normal(k, shape, jnp.float32) * s
    D = D_MODEL
    return {
        "x_prompt": nrm(ks[0], (BATCH, SEQ, D), 1.0),
        "x_sample": nrm(ks[1], (DEC_BATCH, DEC_SEQ, D), 1.0),
        "cache_sb_k": nrm(ks[2], (N_SB_LAYERS, DEC_BATCH, PAST_LEN, SB_HEADS, SB_HEAD_DIM), 1.0),
        "cache_sb_v": nrm(ks[3], (N_SB_LAYERS, DEC_BATCH, PAST_LEN, SB_HEADS, SB_HEAD_DIM), 1.0),
        "state_gla": nrm(ks[4], (N_GLA_LAYERS, DEC_BATCH, GLA_HEADS, GLA_DK, GLA_DV), 1.0),
        "c_prompt": nrm(ks[5], (BATCH, D), 1.0),
        "c_sample": nrm(ks[6], (DEC_BATCH, D), 1.0),
        "w_ada": nrm(ks[7], (DEPTH, D, 3 * D), 0.5 * D ** -0.5),
        "b_ada": nrm(ks[8], (DEPTH, 3 * D), 0.01),
        "norm_g": 1.0 + nrm(ks[9], (DEPTH, D), 0.01),
        "sb_w_in": nrm(ks[10], (N_SB_LAYERS, D, 4 * SB_WIDTH), D ** -0.5),
        "sb_w_out": nrm(ks[11], (N_SB_LAYERS, SB_WIDTH, D), SB_WIDTH ** -0.5),
        "gla_w_in": nrm(ks[12], (N_GLA_LAYERS, D, GLA_IN), D ** -0.5),
        "gla_w_a2": nrm(ks[13], (N_GLA_LAYERS, GLA_GATE_RANK, GLA_HEADS * GLA_DK), GLA_GATE_RANK ** -0.5),
        "gla_b_a": nrm(ks[14], (N_GLA_LAYERS, GLA_HEADS * GLA_DK), 0.1),
        "gla_norm_g": 1.0 + nrm(ks[15], (N_GLA_LAYERS, GLA_HEADS * GLA_DV), 0.01),
        "gla_w_out": nrm(ks[16], (N_GLA_LAYERS, GLA_HEADS * GLA_DV, D), (GLA_HEADS * GLA_DV) ** -0.5),
        "final_norm_g": 1.0 + nrm(ks[17], (D,), 0.01),
    }


def reference(x_prompt, x_sample, cache_sb_k, cache_sb_v, state_gla, c_prompt, c_sample,
              w_ada, b_ada, norm_g, sb_w_in, sb_w_out, gla_w_in, gla_w_a2, gla_b_a,
              gla_norm_g, gla_w_out, final_norm_g):
    weights = (w_ada, b_ada, norm_g, sb_w_in, sb_w_out, gla_w_in, gla_w_a2, gla_b_a,
               gla_norm_g, gla_w_out, final_norm_g)
    y_prompt, k_prompt, v_prompt, s_prompt = trunk(x_prompt, c_prompt, *weights)
    y_sample, k_sample, v_sample, s_sample = trunk(x_sample, c_sample, *weights,
                                                   cache_sb_k, cache_sb_v, state_gla)
    return (y_prompt, y_sample, k_prompt, v_prompt, k_sample, v_sample, s_prompt, s_sample)
```

```python
import functools
import math

import jax
import jax.numpy as jnp
from jax import lax
from jax.experimental import pallas as pl
from jax.experimental.pallas import tpu as pltpu

F32 = jnp.float32
BF16 = jnp.bfloat16

SB_HEADS = 16
SB_HEAD_DIM = 128
GLA_HEADS = 4
GLA_DK = 256
GLA_DV = 512
GLA_GATE_RANK = 16
GLA_TAU = 16.0
GLA_CHUNK = 64
EPS = 1e-6

V7X_LANES = 128
V7X_VMEM_LIMIT_BYTES = 56 * 1024 * 1024
SB_BLOCK = 256
NORM_ROWS = 16


def _params(semantics):
    return pltpu.CompilerParams(dimension_semantics=semantics,
                                vmem_limit_bytes=V7X_VMEM_LIMIT_BYTES)


def _split_bf16(x):
    hi = x.astype(BF16)
    lo = (x - hi.astype(F32)).astype(BF16)
    return hi, lo


def _softplus(z):
    return jnp.maximum(z, 0.0) + jnp.log(1.0 + jnp.exp(-jnp.abs(z)))


def _log_sigmoid(z):
    return jnp.minimum(z, 0.0) - jnp.log(1.0 + jnp.exp(-jnp.abs(z)))


def _silu(z):
    return z / (1.0 + jnp.exp(-z))


def _ada_kernel(c_ref, w_ref, b_ref, o_ref):
    acc = jnp.dot(c_ref[...].astype(BF16), w_ref[0].astype(BF16), preferred_element_type=F32)
    o_ref[0] = acc + b_ref[0]


def _ada(c_all, w_ada, b_ada, tn=512):
    depth, d, n = w_ada.shape
    rows = c_all.shape[0]
    return pl.pallas_call(
        _ada_kernel,
        grid=(depth, n // tn),
        in_specs=[pl.BlockSpec((rows, d), lambda l, j: (0, 0)),
                  pl.BlockSpec((1, d, tn), lambda l, j: (l, 0, j)),
                  pl.BlockSpec((1, 1, tn), lambda l, j: (l, 0, j))],
        out_specs=pl.BlockSpec((1, rows, tn), lambda l, j: (l, 0, j)),
        out_shape=jax.ShapeDtypeStruct((depth, rows, n), F32),
        compiler_params=_params(("arbitrary", "arbitrary")),
        name="ada",
    )(c_all, w_ada, b_ada.reshape(depth, 1, n))


def _inproj_kernel(*refs, segs, tn, tm, with_decay):
    x_ref, sc_ref, sh_ref, g_ref, w_ref = refs[:5]
    pos = 5
    if with_decay:
        wa_ref, wa2_ref, ba_ref = refs[pos:pos + 3]
        pos += 3
    out_refs = refs[pos:pos + len(segs)]
    pos += len(segs)
    if with_decay:
        dec_ref = refs[pos]
        pos += 1
    h_ref = refs[pos]
    n = pl.program_id(2)

    @pl.when(n == 0)
    def _():
        def norm_rows(r, carry):
            rows = pl.ds(pl.multiple_of(r * NORM_ROWS, NORM_ROWS), NORM_ROWS)
            x = x_ref[0, rows, :]
            y = x * lax.rsqrt(jnp.mean(x * x, axis=-1, keepdims=True) + EPS) * g_ref[...]
            if sc_ref.shape[1] == 1:
                sc, sh = sc_ref[0], sh_ref[0]
            else:
                sc, sh = sc_ref[0, rows, :], sh_ref[0, rows, :]
            h_ref[rows, :] = (y * (1.0 + sc) + sh).astype(BF16)
            return carry
        lax.fori_loop(0, tm // NORM_ROWS, norm_rows, 0)

    acc = jnp.dot(h_ref[...], w_ref[...], preferred_element_type=F32)
    for (start, width, dtype, mult), o_ref in zip(segs, out_refs):
        lo, hi = start // tn, (start + width) // tn

        @pl.when((n >= lo) & (n < hi))
        def _(o_ref=o_ref, dtype=dtype, mult=mult):
            o_ref[0] = (acc if mult is None else acc * mult).astype(dtype)

    if with_decay:
        @pl.when(n == pl.num_programs(2) - 1)
        def _():
            a_lr = jnp.dot(h_ref[...], wa_ref[...], preferred_element_type=F32)
            pre = jnp.dot(a_lr.astype(BF16), wa2_ref[...], preferred_element_type=F32) + ba_ref[...]
            dec_ref[0] = _log_sigmoid(pre) * (1.0 / GLA_TAU)


def _inproj(x, scale, shift, g, w, segs, *, tm, tn, decay=None):
    b, t, d = x.shape
    n_cols = w.shape[1]
    assert t % tm == 0 and n_cols % tn == 0 and tm % NORM_ROWS == 0
    for start, width, _, _ in segs:
        assert start % tn == 0 and width % tn == 0
    mod_rows = scale.shape[1]
    mod_spec = (pl.BlockSpec((1, 1, d), lambda bi, i, j: (bi, 0, 0)) if mod_rows == 1
                else pl.BlockSpec((1, tm, d), lambda bi, i, j: (bi, i, 0)))
    in_specs = [pl.BlockSpec((1, tm, d), lambda bi, i, j: (bi, i, 0)), mod_spec, mod_spec,
                pl.BlockSpec((1, d), lambda bi, i, j: (0, 0)),
                pl.BlockSpec((d, tn), lambda bi, i, j: (0, j))]
    args = [x, scale, shift, g.reshape(1, d), w]
    if decay is not None:
        wa, wa2, ba = decay
        in_specs += [pl.BlockSpec(wa.shape, lambda bi, i, j: (0, 0)),
                     pl.BlockSpec(wa2.shape, lambda bi, i, j: (0, 0)),
                     pl.BlockSpec((1, ba.shape[-1]), lambda bi, i, j: (0, 0))]
        args += [wa, wa2, ba.reshape(1, -1)]

    def seg_spec(start, width):
        first, count = start // tn, width // tn
        return pl.BlockSpec((1, tm, tn),
                            lambda bi, i, j: (bi, i, jnp.clip(j - first, 0, count - 1)))

    out_specs = [seg_spec(s, wd) for s, wd, _, _ in segs]
    out_shape = [jax.ShapeDtypeStruct((b, t, wd), dt) for _, wd, dt, _ in segs]
    if decay is not None:
        nk = decay[1].shape[1]
        out_specs.append(pl.BlockSpec((1, tm, nk), lambda bi, i, j: (bi, i, 0)))
        out_shape.append(jax.ShapeDtypeStruct((b, t, nk), F32))
    return pl.pallas_call(
        functools.partial(_inproj_kernel, segs=tuple(segs), tn=tn, tm=tm, with_decay=decay is not None),
        grid=(b, t // tm, n_cols // tn),
        in_specs=in_specs, out_specs=out_specs, out_shape=out_shape,
        scratch_shapes=[pltpu.VMEM((tm, d), BF16)],
        compiler_params=_params(("arbitrary", "arbitrary", "arbitrary")),
        name="inproj",
    )(*args)


def _sb_block(q, k_blk, v_blk, tri, ones, o, run, mask=None):
    z = lax.dot_general(q, k_blk, (((1,), (1,)), ((), ())), preferred_element_type=F32)
    sp = _softplus(z)
    if mask is not None:
        sp = jnp.where(mask, sp, 0.0)
    hi, lo = _split_bf16(sp)
    incl = (jnp.dot(hi, tri, preferred_element_type=F32) + jnp.dot(lo, tri, preferred_element_type=F32))
    tot = (jnp.dot(hi, ones, preferred_element_type=F32) + jnp.dot(lo, ones, preferred_element_type=F32))
    reps = z.shape[1] // V7X_LANES
    run_wide = run if reps == 1 else jnp.concatenate([run] * reps, axis=1)
    a = jnp.exp(z - incl - run_wide)
    if mask is not None:
        a = jnp.where(mask, a, 0.0)
    o = o + jnp.dot(a.astype(BF16), v_blk, preferred_element_type=F32)
    return o, run + tot


def _sb_prompt_kernel(q_ref, k_ref, v_ref, zg_ref, tri_ref, ones_ref, o_ref, *, blk):
    i = pl.program_id(2)
    q = q_ref[0]
    tri, ones = tri_ref[...], ones_ref[...]

    def keys(j):
        rows = pl.ds(pl.multiple_of(j * blk, blk), blk)
        return k_ref[0, rows, :].astype(BF16), v_ref[0, rows, :].astype(BF16)

    t_idx = lax.broadcasted_iota(jnp.int32, (blk, blk), 0)
    s_idx = lax.broadcasted_iota(jnp.int32, (blk, blk), 1)
    kd, vd = keys(i)
    zero = jnp.zeros((blk, V7X_LANES), F32)
    o, run = _sb_block(q, kd, vd, tri, ones, zero, zero, mask=s_idx < t_idx)

    def past(step, carry):
        kb, vb = keys(i - 1 - step)
        return _sb_block(q, kb, vb, tri, ones, *carry)

    o, run = lax.fori_loop(0, i, past, (o, run))
    o_ref[0] = (o * _silu(zg_ref[0].astype(F32))).astype(o_ref.dtype)


def _sb_consts(blk):
    j = lax.broadcasted_iota(jnp.int32, (blk, blk), 0)
    s = lax.broadcasted_iota(jnp.int32, (blk, blk), 1)
    return (j >= s).astype(BF16), jnp.ones((blk, V7X_LANES), BF16)


def _sb_prompt(q, k, v, zg, blk=SB_BLOCK):
    b, t, _ = q.shape
    assert t % blk == 0
    tri, ones = _sb_consts(blk)
    tile = pl.BlockSpec((1, blk, SB_HEAD_DIM), lambda bi, h, i: (bi, i, h))
    whole = pl.BlockSpec((1, t, SB_HEAD_DIM), lambda bi, h, i: (bi, 0, h))
    return pl.pallas_call(
        functools.partial(_sb_prompt_kernel, blk=blk),
        grid=(b, SB_HEADS, t // blk),
        in_specs=[tile, whole, whole, tile,
                  pl.BlockSpec((blk, blk), lambda bi, h, i: (0, 0)),
                  pl.BlockSpec((blk, V7X_LANES), lambda bi, h, i: (0, 0))],
        out_specs=tile,
        out_shape=jax.ShapeDtypeStruct(q.shape, BF16),
        compiler_params=_params(("arbitrary", "arbitrary", "arbitrary")),
        name="sb_prompt",
    )(q, k, v, zg, tri, ones)


def _sb_sample_kernel(q_ref, kn_ref, vn_ref, kp_ref, vp_ref, zg_ref, trin_ref, tri_ref, ones_ref, o_ref,
                      kpad_ref, vpad_ref, *, blk, pad):
    t_new = q_ref.shape[1]
    q = q_ref[0]
    kpad_ref[...] = jnp.zeros_like(kpad_ref)
    vpad_ref[...] = jnp.zeros_like(vpad_ref)
    kpad_ref[0:t_new, :] = kn_ref[0]
    vpad_ref[0:t_new, :] = vn_ref[0]
    t_idx = lax.broadcasted_iota(jnp.int32, (t_new, pad), 0)
    s_idx = lax.broadcasted_iota(jnp.int32, (t_new, pad), 1)
    zero = jnp.zeros((t_new, V7X_LANES), F32)
    carry = _sb_block(q, kpad_ref[...].astype(BF16), vpad_ref[...].astype(BF16), trin_ref[...],
                      ones_ref[0:pad, :], zero, zero, mask=s_idx < t_idx)
    tri, ones = tri_ref[...], ones_ref[...]
    n_past = kp_ref.shape[1] // blk
    for j in range(n_past - 1, -1, -1):
        rows = pl.ds(j * blk, blk)
        carry = _sb_block(q, kp_ref[0, rows, :].astype(BF16), vp_ref[0, rows, :].astype(BF16),
                          tri, ones, *carry)
    o_ref[0] = (carry[0] * _silu(zg_ref[0].astype(F32))).astype(o_ref.dtype)


def _sb_sample(q, k_new, v_new, k_past, v_past, zg, blk=SB_BLOCK, pad=V7X_LANES):
    b, t, _ = q.shape
    p = k_past.shape[1]
    assert p % blk == 0 and t <= pad
    tri, ones = _sb_consts(blk)
    tri_new, _ = _sb_consts(pad)
    new = pl.BlockSpec((1, t, SB_HEAD_DIM), lambda bi, h: (bi, 0, h))
    past = pl.BlockSpec((1, p, SB_HEAD_DIM), lambda bi, h: (bi, 0, h))
    return pl.pallas_call(
        functools.partial(_sb_sample_kernel, blk=blk, pad=pad),
        grid=(b, SB_HEADS),
        in_specs=[new, new, new, past, past, new,
                  pl.BlockSpec((pad, pad), lambda bi, h: (0, 0)),
                  pl.BlockSpec((blk, blk), lambda bi, h: (0, 0)),
                  pl.BlockSpec((blk, V7X_LANES), lambda bi, h: (0, 0))],
        out_specs=new,
        out_shape=jax.ShapeDtypeStruct(q.shape, BF16),
        scratch_shapes=[pltpu.VMEM((pad, SB_HEAD_DIM), F32), pltpu.VMEM((pad, SB_HEAD_DIM), F32)],
        compiler_params=_params(("arbitrary", "arbitrary")),
        name="sb_sample",
    )(q, k_new, v_new, k_past, v_past, zg, tri_new, tri, ones)


def _gla_kernel(*refs, chunk, n_chunks, has_state):
    if has_state:
        q_ref, k_ref, v_ref, dec_ref, r_ref, gn_ref, tril_ref, ones_ref, s0_ref, o_ref, sf_ref, s_ref = refs
    else:
        q_ref, k_ref, v_ref, dec_ref, r_ref, gn_ref, tril_ref, ones_ref, o_ref, sf_ref, s_ref = refs
    step = pl.program_id(2)

    @pl.when(step == 0)
    def _():
        s_ref[...] = s0_ref[0, 0] if has_state else jnp.zeros_like(s_ref)

    tril = tril_ref[...]
    ones = ones_ref[...]
    causal = (lax.broadcasted_iota(jnp.int32, (chunk, chunk), 1)
              <= lax.broadcasted_iota(jnp.int32, (chunk, chunk), 0))
    q_scale = GLA_DK ** -0.5
    tdot = lambda a, b: lax.dot_general(a, b, (((0,), (0,)), ((), ())), preferred_element_type=F32)

    def one_chunk(c, carry):
        rows = pl.ds(pl.multiple_of(c * chunk, chunk), chunk)
        q = q_ref[0, rows, :] * q_scale
        k = k_ref[0, rows, :]
        v = v_ref[0, rows, :]
        g_hi, g_lo = _split_bf16(dec_ref[0, rows, :])
        b = jnp.dot(tril, g_hi, preferred_element_type=F32) + jnp.dot(tril, g_lo, preferred_element_type=F32)
        b_last = b[chunk - 1:chunk, :]
        b_last_rows = tdot(g_hi, ones) + tdot(g_lo, ones)
        q_in = (q * jnp.exp(b)).astype(BF16)
        k_out = (k * jnp.exp(-b)).astype(BF16)
        k_end = (k * jnp.exp(b_last - b)).astype(BF16)
        a = lax.dot_general(q_in, k_out, (((1,), (1,)), ((), ())), preferred_element_type=F32)
        a = jnp.where(causal, a, 0.0).astype(BF16)
        s_old = s_ref[...]
        o = (jnp.dot(a, v, preferred_element_type=F32)
             + jnp.dot(q_in, s_old.astype(BF16), preferred_element_type=F32))
        decay = jnp.exp(b_last_rows)
        decay = jnp.concatenate([decay] * (GLA_DV // V7X_LANES), axis=1)
        s_ref[...] = decay * s_old + tdot(k_end, v)
        o = o * lax.rsqrt(jnp.mean(o * o, axis=-1, keepdims=True) + EPS) * gn_ref[...]
        o_ref[0, rows, :] = (o * _silu(r_ref[0, rows, :].astype(F32))).astype(o_ref.dtype)
        return carry

    lax.fori_loop(0, n_chunks, one_chunk, 0)

    @pl.when(step == pl.num_programs(2) - 1)
    def _():
        sf_ref[0, 0] = s_ref[...]


def _gla(qk, v, dec, r, gn, s0, *, chunk, tt):
    b, t, _ = v.shape
    assert t % tt == 0 and tt % chunk == 0
    tril = (lax.broadcasted_iota(jnp.int32, (chunk, chunk), 1)
            <= lax.broadcasted_iota(jnp.int32, (chunk, chunk), 0)).astype(BF16)
    ones = jnp.ones((chunk, V7X_LANES), BF16)
    kspec = lambda off: pl.BlockSpec((1, tt, GLA_DK), lambda bi, h, s: (bi, s, h + off))
    vspec = pl.BlockSpec((1, tt, GLA_DV), lambda bi, h, s: (bi, s, h))
    sspec = pl.BlockSpec((1, 1, GLA_DK, GLA_DV), lambda bi, h, s: (bi, h, 0, 0))
    in_specs = [kspec(0), kspec(GLA_HEADS), vspec, kspec(0), vspec,
                pl.BlockSpec((1, GLA_DV), lambda bi, h, s: (0, h)),
                pl.BlockSpec((chunk, chunk), lambda bi, h, s: (0, 0)),
                pl.BlockSpec((chunk, V7X_LANES), lambda bi, h, s: (0, 0))]
    args = [qk, qk, v, dec, r, gn.reshape(1, -1), tril, ones]
    if s0 is not None:
        in_specs.append(sspec)
        args.append(s0)
    return pl.pallas_call(
        functools.partial(_gla_kernel, chunk=chunk, n_chunks=tt // chunk, has_state=s0 is not None),
        grid=(b, GLA_HEADS, t // tt),
        in_specs=in_specs,
        out_specs=[vspec, sspec],
        out_shape=[jax.ShapeDtypeStruct(v.shape, BF16),
                   jax.ShapeDtypeStruct((b, GLA_HEADS, GLA_DK, GLA_DV), F32)],
        scratch_shapes=[pltpu.VMEM((GLA_DK, GLA_DV), F32)],
        compiler_params=_params(("arbitrary", "arbitrary", "arbitrary")),
        name="gla",
    )(*args)


def _outproj_kernel(y_ref, w_ref, x_ref, gate_ref, *rest, final_norm):
    if final_norm:
        gf_ref, o_ref = rest
    else:
        (o_ref,) = rest
    y = jnp.dot(y_ref[0], w_ref[...], preferred_element_type=F32)
    x = x_ref[0] + gate_ref[0] * y
    if final_norm:
        x = x * lax.rsqrt(jnp.mean(x * x, axis=-1, keepdims=True) + EPS) * gf_ref[...]
    o_ref[0] = x


def _outproj(y, w, x, gate, gf, *, tm):
    b, t, d = x.shape
    kdim = y.shape[-1]
    assert t % tm == 0
    gate_spec = (pl.BlockSpec((1, 1, d), lambda bi, i: (bi, 0, 0)) if gate.shape[1] == 1
                 else pl.BlockSpec((1, tm, d), lambda bi, i: (bi, i, 0)))
    in_specs = [pl.BlockSpec((1, tm, kdim), lambda bi, i: (bi, i, 0)),
                pl.BlockSpec((kdim, d), lambda bi, i: (0, 0)),
                pl.BlockSpec((1, tm, d), lambda bi, i: (bi, i, 0)),
                gate_spec]
    args = [y, w, x, gate]
    if gf is not None:
        in_specs.append(pl.BlockSpec((1, d), lambda bi, i: (0, 0)))
        args.append(gf.reshape(1, d))
    return pl.pallas_call(
        functools.partial(_outproj_kernel, final_norm=gf is not None),
        grid=(b, t // tm),
        in_specs=in_specs,
        out_specs=pl.BlockSpec((1, tm, d), lambda bi, i: (bi, i, 0)),
        out_shape=jax.ShapeDtypeStruct(x.shape, F32),
        compiler_params=_params(("arbitrary", "arbitrary")),
        name="outproj",
    )(*args)


def _trunk(x, mods, weights, *, per_row, tm_in, tm_out, gla_tt, cache=None, state=None):
    (norm_g, sb_w_in, sb_w_out, gla_w_in, gla_wa, gla_wa2, gla_b_a, gla_norm_g, gla_w_out,
     final_norm_g) = weights
    b, t, d = x.shape
    if per_row:
        fold = lambda a: a.reshape(1, b * t, a.shape[-1])
        rows = lambda m: jnp.broadcast_to(m[:, None, :], (b, t, d)).reshape(1, b * t, d)
    else:
        fold = lambda a: a
        rows = lambda m: m[:, None, :]
    unfold = lambda a: a.reshape(b, t, a.shape[-1])
    width = SB_HEADS * SB_HEAD_DIM

    shift, scale, gate = mods[0]
    q, k, v, zg = _inproj(
        fold(x), rows(scale), rows(shift), norm_g[0], sb_w_in,
        [(0, width, BF16, SB_HEAD_DIM ** -0.5), (width, width, F32, None),
         (2 * width, width, F32, None), (3 * width, width, BF16, None)],
        tm=tm_in, tn=512)
    q, k, v, zg = unfold(q), unfold(k), unfold(v), unfold(zg)
    if cache is None:
        branch = _sb_prompt(q, k, v, zg)
    else:
        branch = _sb_sample(q, k, v, cache[0], cache[1], zg)
    x1 = _outproj(fold(branch), sb_w_out, fold(x), rows(gate), None, tm=tm_out)

    shift, scale, gate = mods[1]
    nk, nv = GLA_HEADS * GLA_DK, GLA_HEADS * GLA_DV
    qk, gv, r, dec = _inproj(
        x1, rows(scale), rows(shift), norm_g[1], gla_w_in,
        [(0, 2 * nk, F32, None), (2 * nk, nv, BF16, None), (2 * nk + nv, nv, BF16, None)],
        tm=tm_in, tn=512, decay=(gla_wa, gla_wa2, gla_b_a))
    chunk = min(GLA_CHUNK, t)
    branch, s_new = _gla(unfold(qk), unfold(gv), unfold(dec), unfold(r), gla_norm_g, state,
                         chunk=chunk, tt=min(gla_tt, t))
    y = _outproj(fold(branch), gla_w_out, x1, rows(gate), final_norm_g, tm=tm_out)
    return unfold(y), k, v, s_new


def kernel(x_prompt, x_sample, cache_sb_k, cache_sb_v, state_gla, c_prompt, c_sample, w_ada, b_ada, norm_g,
           sb_w_in, sb_w_out, gla_w_in, gla_w_a2, gla_b_a, gla_norm_g, gla_w_out, final_norm_g):
    depth, d, _ = w_ada.shape
    assert depth == 2 and sb_w_in.shape[0] == 1 and gla_w_in.shape[0] == 1
    bp, tp, _ = x_prompt.shape
    bs, ts, _ = x_sample.shape
    past = cache_sb_k.shape[2]
    nk, nv = GLA_HEADS * GLA_DK, GLA_HEADS * GLA_DV
    main = 2 * nk + 2 * nv

    mod = _ada(jnp.concatenate([c_prompt, c_sample], axis=0), w_ada, b_ada)
    split = lambda m: (m[:, :d], m[:, d:2 * d], m[:, 2 * d:])
    mods_p = [split(mod[l, :bp]) for l in range(depth)]
    mods_s = [split(mod[l, bp:]) for l in range(depth)]

    gla_w = gla_w_in[0]
    gla_wa = jnp.pad(gla_w[:, main:], ((0, 0), (0, V7X_LANES - GLA_GATE_RANK))).astype(BF16)
    gla_wa2 = jnp.pad(gla_w_a2[0], ((0, V7X_LANES - GLA_GATE_RANK), (0, 0))).astype(BF16)
    weights = (norm_g, sb_w_in[0].astype(BF16), sb_w_out[0].astype(BF16), gla_w[:, :main].astype(BF16),
               gla_wa, gla_wa2, gla_b_a[0], gla_norm_g[0], gla_w_out[0].astype(BF16), final_norm_g)

    y_p, k_p, v_p, s_p = _trunk(x_prompt, mods_p, weights, per_row=False,
                                tm_in=min(1024, tp), tm_out=min(256, tp), gla_tt=512)
    cache = (cache_sb_k[0].reshape(bs, past, -1), cache_sb_v[0].reshape(bs, past, -1))
    y_s, k_s, v_s, s_s = _trunk(x_sample, mods_s, weights, per_row=True,
                                tm_in=bs * ts, tm_out=bs * ts, gla_tt=ts, cache=cache, state=state_gla[0])

    heads = lambda a: a.reshape(1, a.shape[0], a.shape[1], SB_HEADS, SB_HEAD_DIM)
    return (y_p, y_s, heads(k_p), heads(v_p), heads(k_s), heads(v_s), s_p[None], s_s[None])
```

```python
import functools
import math

import jax
import jax.numpy as jnp
from jax import lax
from jax.experimental import pallas as pl
from jax.experimental.pallas import tpu as pltpu

F32 = jnp.float32
BF16 = jnp.bfloat16

SB_HEADS = 16
SB_HEAD_DIM = 128
GLA_HEADS = 4
GLA_DK = 256
GLA_DV = 512
GLA_GATE_RANK = 16
GLA_TAU = 16.0
GLA_CHUNK = 64
EPS = 1e-6

V7X_LANES = 128
V7X_VMEM_LIMIT_BYTES = 56 * 1024 * 1024
SB_BLOCK = 256
SB_QUERY_TILE = 1024
NORM_ROWS = 16


def _params(semantics):
    return pltpu.CompilerParams(dimension_semantics=semantics,
                                vmem_limit_bytes=V7X_VMEM_LIMIT_BYTES)


def _split_bf16(x):
    hi = x.astype(BF16)
    lo = (x - hi.astype(F32)).astype(BF16)
    return hi, lo


def _log_sigmoid(z):
    return jnp.minimum(z, 0.0) - jnp.log(1.0 + jnp.exp(-jnp.abs(z)))


def _silu(z):
    return z / (1.0 + jnp.exp(-z))


def _ada_kernel(c_ref, w_ref, b_ref, o_ref):
    acc = jnp.dot(c_ref[...].astype(BF16), w_ref[0].astype(BF16), preferred_element_type=F32)
    o_ref[0] = acc + b_ref[0]


def _ada(c_all, w_ada, b_ada, tn=512):
    depth, d, n = w_ada.shape
    rows = c_all.shape[0]
    return pl.pallas_call(
        _ada_kernel,
        grid=(depth, n // tn),
        in_specs=[pl.BlockSpec((rows, d), lambda l, j: (0, 0)),
                  pl.BlockSpec((1, d, tn), lambda l, j: (l, 0, j)),
                  pl.BlockSpec((1, 1, tn), lambda l, j: (l, 0, j))],
        out_specs=pl.BlockSpec((1, rows, tn), lambda l, j: (l, 0, j)),
        out_shape=jax.ShapeDtypeStruct((depth, rows, n), F32),
        compiler_params=_params(("arbitrary", "arbitrary")),
        name="ada",
    )(c_all, w_ada, b_ada.reshape(depth, 1, n))


def _inproj_kernel(*refs, segs, tn, tm, with_decay):
    x_ref, sc_ref, sh_ref, g_ref, w_ref = refs[:5]
    pos = 5
    if with_decay:
        wa_ref, wa2_ref, ba_ref = refs[pos:pos + 3]
        pos += 3
    out_refs = refs[pos:pos + len(segs)]
    pos += len(segs)
    if with_decay:
        dec_ref = refs[pos]
        pos += 1
    h_ref = refs[pos]
    n = pl.program_id(2)

    @pl.when(n == 0)
    def _():
        def norm_rows(r, carry):
            rows = pl.ds(pl.multiple_of(r * NORM_ROWS, NORM_ROWS), NORM_ROWS)
            x = x_ref[0, rows, :]
            y = x * lax.rsqrt(jnp.mean(x * x, axis=-1, keepdims=True) + EPS) * g_ref[...]
            if sc_ref.shape[1] == 1:
                sc, sh = sc_ref[0], sh_ref[0]
            else:
                sc, sh = sc_ref[0, rows, :], sh_ref[0, rows, :]
            h_ref[rows, :] = (y * (1.0 + sc) + sh).astype(BF16)
            return carry
        lax.fori_loop(0, tm // NORM_ROWS, norm_rows, 0)

    acc = jnp.dot(h_ref[...], w_ref[...], preferred_element_type=F32)
    for (start, width, dtype, mult), o_ref in zip(segs, out_refs):
        lo, hi = start // tn, (start + width) // tn

        @pl.when((n >= lo) & (n < hi))
        def _(o_ref=o_ref, dtype=dtype, mult=mult):
            o_ref[0] = (acc if mult is None else acc * mult).astype(dtype)

    if with_decay:
        @pl.when(n == pl.num_programs(2) - 1)
        def _():
            a_lr = jnp.dot(h_ref[...], wa_ref[...], preferred_element_type=F32)
            pre = jnp.dot(a_lr.astype(BF16), wa2_ref[...], preferred_element_type=F32) + ba_ref[...]
            dec_ref[0] = _log_sigmoid(pre) * (1.0 / GLA_TAU)


def _inproj(x, scale, shift, g, w, segs, *, tm, tn, decay=None):
    b, t, d = x.shape
    n_cols = w.shape[1]
    assert t % tm == 0 and n_cols % tn == 0 and tm % NORM_ROWS == 0
    for start, width, _, _ in segs:
        assert start % tn == 0 and width % tn == 0
    mod_rows = scale.shape[1]
    mod_spec = (pl.BlockSpec((1, 1, d), lambda bi, i, j: (bi, 0, 0)) if mod_rows == 1
                else pl.BlockSpec((1, tm, d), lambda bi, i, j: (bi, i, 0)))
    in_specs = [pl.BlockSpec((1, tm, d), lambda bi, i, j: (bi, i, 0)), mod_spec, mod_spec,
                pl.BlockSpec((1, d), lambda bi, i, j: (0, 0)),
                pl.BlockSpec((d, tn), lambda bi, i, j: (0, j))]
    args = [x, scale, shift, g.reshape(1, d), w]
    if decay is not None:
        wa, wa2, ba = decay
        in_specs += [pl.BlockSpec(wa.shape, lambda bi, i, j: (0, 0)),
                     pl.BlockSpec(wa2.shape, lambda bi, i, j: (0, 0)),
                     pl.BlockSpec((1, ba.shape[-1]), lambda bi, i, j: (0, 0))]
        args += [wa, wa2, ba.reshape(1, -1)]

    def seg_spec(start, width):
        first, count = start // tn, width // tn
        return pl.BlockSpec((1, tm, tn),
                            lambda bi, i, j: (bi, i, jnp.clip(j - first, 0, count - 1)))

    out_specs = [seg_spec(s, wd) for s, wd, _, _ in segs]
    out_shape = [jax.ShapeDtypeStruct((b, t, wd), dt) for _, wd, dt, _ in segs]
    if decay is not None:
        nk = decay[1].shape[1]
        out_specs.append(pl.BlockSpec((1, tm, nk), lambda bi, i, j: (bi, i, 0)))
        out_shape.append(jax.ShapeDtypeStruct((b, t, nk), F32))
    return pl.pallas_call(
        functools.partial(_inproj_kernel, segs=tuple(segs), tn=tn, tm=tm, with_decay=decay is not None),
        grid=(b, t // tm, n_cols // tn),
        in_specs=in_specs, out_specs=out_specs, out_shape=out_shape,
        scratch_shapes=[pltpu.VMEM((tm, d), BF16)],
        compiler_params=_params(("arbitrary", "arbitrary", "arbitrary")),
        name="inproj",
    )(*args)


def _sb_block(q, k_blk, v_blk, w, run, mask=None):
    half = V7X_LANES
    z = lax.dot_general(q, k_blk, (((1,), (1,)), ((), ())), preferred_element_type=F32)
    sp = jnp.maximum(z, 0.0) + jnp.log2(1.0 + jnp.exp2(-jnp.abs(z)))
    if mask is not None:
        sp = jnp.where(mask, sp, 0.0)
    hi, lo = _split_bf16(sp)
    cs_r = jnp.dot(jnp.concatenate([hi[:, half:], lo[:, half:]], axis=1), w, preferred_element_type=F32)
    cs_l = jnp.dot(jnp.concatenate([hi[:, :half], lo[:, :half]], axis=1), w, preferred_element_type=F32)
    a_r = jnp.exp2(z[:, half:] - cs_r[:, :half] - run)
    run = run + cs_r[:, half:]
    a_l = jnp.exp2(z[:, :half] - cs_l[:, :half] - run)
    run = run + cs_l[:, half:]
    a = jnp.concatenate([a_l, a_r], axis=1)
    if mask is not None:
        a = jnp.where(mask, a, 0.0)
    return jnp.dot(a.astype(BF16), v_blk, preferred_element_type=F32), run


def _sb_consts():
    j = lax.broadcasted_iota(jnp.int32, (V7X_LANES, V7X_LANES), 0)
    s = lax.broadcasted_iota(jnp.int32, (V7X_LANES, V7X_LANES), 1)
    half = jnp.concatenate([(j >= s).astype(BF16), jnp.ones((V7X_LANES, V7X_LANES), BF16)], axis=1)
    return jnp.concatenate([half, half], axis=0)


def _causal_mask(rows):
    t_idx = lax.broadcasted_iota(jnp.int32, (rows, SB_BLOCK), 0)
    s_idx = lax.broadcasted_iota(jnp.int32, (rows, SB_BLOCK), 1)
    return s_idx < t_idx


def _sb_prompt_kernel(q_ref, k_ref, v_ref, zg_ref, w_ref, o_ref, acc_ref, run_ref, *, tq):
    i = pl.program_id(2)
    groups = tq // SB_BLOCK
    w = w_ref[...]

    def keys(first):
        rows = pl.ds(pl.multiple_of(first, SB_BLOCK), SB_BLOCK)
        return k_ref[0, rows, :].astype(BF16), v_ref[0, rows, :].astype(BF16)

    for c in range(groups - 1, -1, -1):
        rows = slice(c * SB_BLOCK, tq)
        later = slice((c + 1) * SB_BLOCK, tq)
        kb, vb = keys(i * tq + c * SB_BLOCK)
        run = jnp.zeros((SB_BLOCK, V7X_LANES), F32)
        if c < groups - 1:
            run = jnp.concatenate([run, run_ref[later, :]], axis=0)
        pv, run = _sb_block(q_ref[0, rows, :], kb, vb, w, run, mask=_causal_mask(tq - c * SB_BLOCK))
        acc_ref[c * SB_BLOCK:(c + 1) * SB_BLOCK, :] = pv[:SB_BLOCK]
        if c < groups - 1:
            acc_ref[later, :] += pv[SB_BLOCK:]
        run_ref[rows, :] = run

    def past(step, carry):
        kb, vb = keys((i * groups - 1 - step) * SB_BLOCK)
        pv, run = _sb_block(q_ref[0], kb, vb, w, run_ref[...])
        acc_ref[...] += pv
        run_ref[...] = run
        return carry

    lax.fori_loop(0, i * groups, past, 0)
    o_ref[0] = (acc_ref[...] * _silu(zg_ref[0].astype(F32))).astype(o_ref.dtype)


def _sb_prompt(q, k, v, zg, tq):
    b, t, _ = q.shape
    assert t % tq == 0 and tq % SB_BLOCK == 0
    tile = pl.BlockSpec((1, tq, SB_HEAD_DIM), lambda bi, h, i: (bi, i, h))
    whole = pl.BlockSpec((1, t, SB_HEAD_DIM), lambda bi, h, i: (bi, 0, h))
    return pl.pallas_call(
        functools.partial(_sb_prompt_kernel, tq=tq),
        grid=(b, SB_HEADS, t // tq),
        in_specs=[tile, whole, whole, tile, pl.BlockSpec((SB_BLOCK, SB_BLOCK), lambda bi, h, i: (0, 0))],
        out_specs=tile,
        out_shape=jax.ShapeDtypeStruct(q.shape, BF16),
        scratch_shapes=[pltpu.VMEM((tq, SB_HEAD_DIM), F32), pltpu.VMEM((tq, V7X_LANES), F32)],
        compiler_params=_params(("arbitrary", "arbitrary", "arbitrary")),
        name="sb_prompt",
    )(q, k, v, zg, _sb_consts())


def _sb_sample_kernel(q_ref, kn_ref, vn_ref, kp_ref, vp_ref, zg_ref, w_ref, o_ref, kpad_ref, vpad_ref):
    t_new = q_ref.shape[1]
    q = q_ref[0]
    w = w_ref[...]
    kpad_ref[...] = jnp.zeros_like(kpad_ref)
    vpad_ref[...] = jnp.zeros_like(vpad_ref)
    kpad_ref[0:t_new, :] = kn_ref[0]
    vpad_ref[0:t_new, :] = vn_ref[0]
    acc, run = _sb_block(q, kpad_ref[...].astype(BF16), vpad_ref[...].astype(BF16), w,
                         jnp.zeros((t_new, V7X_LANES), F32), mask=_causal_mask(t_new))
    for j in range(kp_ref.shape[1] // SB_BLOCK - 1, -1, -1):
        rows = pl.ds(j * SB_BLOCK, SB_BLOCK)
        pv, run = _sb_block(q, kp_ref[0, rows, :].astype(BF16), vp_ref[0, rows, :].astype(BF16), w, run)
        acc = acc + pv
    o_ref[0] = (acc * _silu(zg_ref[0].astype(F32))).astype(o_ref.dtype)


def _sb_sample(q, k_new, v_new, k_past, v_past, zg):
    b, t, _ = q.shape
    p = k_past.shape[1]
    assert p % SB_BLOCK == 0 and t <= SB_BLOCK
    new = pl.BlockSpec((1, t, SB_HEAD_DIM), lambda bi, h: (bi, 0, h))
    past = pl.BlockSpec((1, p, SB_HEAD_DIM), lambda bi, h: (bi, 0, h))
    return pl.pallas_call(
        _sb_sample_kernel,
        grid=(b, SB_HEADS),
        in_specs=[new, new, new, past, past, new, pl.BlockSpec((SB_BLOCK, SB_BLOCK), lambda bi, h: (0, 0))],
        out_specs=new,
        out_shape=jax.ShapeDtypeStruct(q.shape, BF16),
        scratch_shapes=[pltpu.VMEM((SB_BLOCK, SB_HEAD_DIM), F32), pltpu.VMEM((SB_BLOCK, SB_HEAD_DIM), F32)],
        compiler_params=_params(("arbitrary", "arbitrary")),
        name="sb_sample",
    )(q, k_new, v_new, k_past, v_past, zg, _sb_consts())


def _gla_kernel(*refs, chunk, n_chunks, has_state):
    if has_state:
        q_ref, k_ref, v_ref, dec_ref, r_ref, gn_ref, tril_ref, ones_ref, s0_ref, o_ref, sf_ref, s_ref = refs
    else:
        q_ref, k_ref, v_ref, dec_ref, r_ref, gn_ref, tril_ref, ones_ref, o_ref, sf_ref, s_ref = refs
    step = pl.program_id(2)

    @pl.when(step == 0)
    def _():
        s_ref[...] = s0_ref[0, 0] if has_state else jnp.zeros_like(s_ref)

    tril = tril_ref[...]
    ones = ones_ref[...]
    causal = (lax.broadcasted_iota(jnp.int32, (chunk, chunk), 1)
              <= lax.broadcasted_iota(jnp.int32, (chunk, chunk), 0))
    q_scale = GLA_DK ** -0.5
    tdot = lambda a, b: lax.dot_general(a, b, (((0,), (0,)), ((), ())), preferred_element_type=F32)

    def one_chunk(c, carry):
        rows = pl.ds(pl.multiple_of(c * chunk, chunk), chunk)
        q = q_ref[0, rows, :] * q_scale
        k = k_ref[0, rows, :]
        v = v_ref[0, rows, :]
        g_hi, g_lo = _split_bf16(dec_ref[0, rows, :])
        b = jnp.dot(tril, g_hi, preferred_element_type=F32) + jnp.dot(tril, g_lo, preferred_element_type=F32)
        b_last = b[chunk - 1:chunk, :]
        b_last_rows = tdot(g_hi, ones) + tdot(g_lo, ones)
        q_in = (q * jnp.exp(b)).astype(BF16)
        k_out = (k * jnp.exp(-b)).astype(BF16)
        k_end = (k * jnp.exp(b_last - b)).astype(BF16)
        a = lax.dot_general(q_in, k_out, (((1,), (1,)), ((), ())), preferred_element_type=F32)
        a = jnp.where(causal, a, 0.0).astype(BF16)
        s_old = s_ref[...]
        o = (jnp.dot(a, v, preferred_element_type=F32)
             + jnp.dot(q_in, s_old.astype(BF16), preferred_element_type=F32))
        decay = jnp.exp(b_last_rows)
        decay = jnp.concatenate([decay] * (GLA_DV // V7X_LANES), axis=1)
        s_ref[...] = decay * s_old + tdot(k_end, v)
        o = o * lax.rsqrt(jnp.mean(o * o, axis=-1, keepdims=True) + EPS) * gn_ref[...]
        o_ref[0, rows, :] = (o * _silu(r_ref[0, rows, :].astype(F32))).astype(o_ref.dtype)
        return carry

    lax.fori_loop(0, n_chunks, one_chunk, 0)

    @pl.when(step == pl.num_programs(2) - 1)
    def _():
        sf_ref[0, 0] = s_ref[...]


def _gla(qk, v, dec, r, gn, s0, *, chunk, tt):
    b, t, _ = v.shape
    assert t % tt == 0 and tt % chunk == 0
    tril = (lax.broadcasted_iota(jnp.int32, (chunk, chunk), 1)
            <= lax.broadcasted_iota(jnp.int32, (chunk, chunk), 0)).astype(BF16)
    ones = jnp.ones((chunk, V7X_LANES), BF16)
    kspec = lambda off: pl.BlockSpec((1, tt, GLA_DK), lambda bi, h, s: (bi, s, h + off))
    vspec = pl.BlockSpec((1, tt, GLA_DV), lambda bi, h, s: (bi, s, h))
    sspec = pl.BlockSpec((1, 1, GLA_DK, GLA_DV), lambda bi, h, s: (bi, h, 0, 0))
    in_specs = [kspec(0), kspec(GLA_HEADS), vspec, kspec(0), vspec,
                pl.BlockSpec((1, GLA_DV), lambda bi, h, s: (0, h)),
                pl.BlockSpec((chunk, chunk), lambda bi, h, s: (0, 0)),
                pl.BlockSpec((chunk, V7X_LANES), lambda bi, h, s: (0, 0))]
    args = [qk, qk, v, dec, r, gn.reshape(1, -1), tril, ones]
    if s0 is not None:
        in_specs.append(sspec)
        args.append(s0)
    return pl.pallas_call(
        functools.partial(_gla_kernel, chunk=chunk, n_chunks=tt // chunk, has_state=s0 is not None),
        grid=(b, GLA_HEADS, t // tt),
        in_specs=in_specs,
        out_specs=[vspec, sspec],
        out_shape=[jax.ShapeDtypeStruct(v.shape, BF16),
                   jax.ShapeDtypeStruct((b, GLA_HEADS, GLA_DK, GLA_DV), F32)],
        scratch_shapes=[pltpu.VMEM((GLA_DK, GLA_DV), F32)],
        compiler_params=_params(("arbitrary", "arbitrary", "arbitrary")),
        name="gla",
    )(*args)


def _outproj_kernel(y_ref, w_ref, x_ref, gate_ref, *rest, final_norm):
    if final_norm:
        gf_ref, o_ref = rest
    else:
        (o_ref,) = rest
    y = jnp.dot(y_ref[0], w_ref[...], preferred_element_type=F32)
    x = x_ref[0] + gate_ref[0] * y
    if final_norm:
        x = x * lax.rsqrt(jnp.mean(x * x, axis=-1, keepdims=True) + EPS) * gf_ref[...]
    o_ref[0] = x


def _outproj(y, w, x, gate, gf, *, tm):
    b, t, d = x.shape
    kdim = y.shape[-1]
    assert t % tm == 0
    gate_spec = (pl.BlockSpec((1, 1, d), lambda bi, i: (bi, 0, 0)) if gate.shape[1] == 1
                 else pl.BlockSpec((1, tm, d), lambda bi, i: (bi, i, 0)))
    in_specs = [pl.BlockSpec((1, tm, kdim), lambda bi, i: (bi, i, 0)),
                pl.BlockSpec((kdim, d), lambda bi, i: (0, 0)),
                pl.BlockSpec((1, tm, d), lambda bi, i: (bi, i, 0)),
                gate_spec]
    args = [y, w, x, gate]
    if gf is not None:
        in_specs.append(pl.BlockSpec((1, d), lambda bi, i: (0, 0)))
        args.append(gf.reshape(1, d))
    return pl.pallas_call(
        functools.partial(_outproj_kernel, final_norm=gf is not None),
        grid=(b, t // tm),
        in_specs=in_specs,
        out_specs=pl.BlockSpec((1, tm, d), lambda bi, i: (bi, i, 0)),
        out_shape=jax.ShapeDtypeStruct(x.shape, F32),
        compiler_params=_params(("arbitrary", "arbitrary")),
        name="outproj",
    )(*args)


def _trunk(x, mods, weights, *, per_row, tm_in, tm_out, gla_tt, cache=None, state=None):
    (norm_g, sb_w_in, sb_w_out, gla_w_in, gla_wa, gla_wa2, gla_b_a, gla_norm_g, gla_w_out,
     final_norm_g) = weights
    b, t, d = x.shape
    if per_row:
        fold = lambda a: a.reshape(1, b * t, a.shape[-1])
        rows = lambda m: jnp.broadcast_to(m[:, None, :], (b, t, d)).reshape(1, b * t, d)
    else:
        fold = lambda a: a
        rows = lambda m: m[:, None, :]
    unfold = lambda a: a.reshape(b, t, a.shape[-1])
    width = SB_HEADS * SB_HEAD_DIM

    shift, scale, gate = mods[0]
    q, k, v, zg = _inproj(
        fold(x), rows(scale), rows(shift), norm_g[0], sb_w_in,
        [(0, width, BF16, SB_HEAD_DIM ** -0.5 * math.log2(math.e)), (width, width, F32, None),
         (2 * width, width, F32, None), (3 * width, width, BF16, None)],
        tm=tm_in, tn=512)
    q, k, v, zg = unfold(q), unfold(k), unfold(v), unfold(zg)
    if cache is None:
        branch = _sb_prompt(q, k, v, zg, tq=min(SB_QUERY_TILE, t))
    else:
        branch = _sb_sample(q, k, v, cache[0], cache[1], zg)
    x1 = _outproj(fold(branch), sb_w_out, fold(x), rows(gate), None, tm=tm_out)

    shift, scale, gate = mods[1]
    nk, nv = GLA_HEADS * GLA_DK, GLA_HEADS * GLA_DV
    qk, gv, r, dec = _inproj(
        x1, rows(scale), rows(shift), norm_g[1], gla_w_in,
        [(0, 2 * nk, F32, None), (2 * nk, nv, BF16, None), (2 * nk + nv, nv, BF16, None)],
        tm=tm_in, tn=512, decay=(gla_wa, gla_wa2, gla_b_a))
    chunk = min(GLA_CHUNK, t)
    branch, s_new = _gla(unfold(qk), unfold(gv), unfold(dec), unfold(r), gla_norm_g, state,
                         chunk=chunk, tt=min(gla_tt, t))
    y = _outproj(fold(branch), gla_w_out, x1, rows(gate), final_norm_g, tm=tm_out)
    return unfold(y), k, v, s_new


def kernel(x_prompt, x_sample, cache_sb_k, cache_sb_v, state_gla, c_prompt, c_sample, w_ada, b_ada, norm_g,
           sb_w_in, sb_w_out, gla_w_in, gla_w_a2, gla_b_a, gla_norm_g, gla_w_out, final_norm_g):
    depth, d, _ = w_ada.shape
    assert depth == 2 and sb_w_in.shape[0] == 1 and gla_w_in.shape[0] == 1
    bp, tp, _ = x_prompt.shape
    bs, ts, _ = x_sample.shape
    past = cache_sb_k.shape[2]
    nk, nv = GLA_HEADS * GLA_DK, GLA_HEADS * GLA_DV
    main = 2 * nk + 2 * nv

    mod = _ada(jnp.concatenate([c_prompt, c_sample], axis=0), w_ada, b_ada)
    split = lambda m: (m[:, :d], m[:, d:2 * d], m[:, 2 * d:])
    mods_p = [split(mod[l, :bp]) for l in range(depth)]
    mods_s = [split(mod[l, bp:]) for l in range(depth)]

    gla_w = gla_w_in[0]
    gla_wa = jnp.pad(gla_w[:, main:], ((0, 0), (0, V7X_LANES - GLA_GATE_RANK))).astype(BF16)
    gla_wa2 = jnp.pad(gla_w_a2[0], ((0, V7X_LANES - GLA_GATE_RANK), (0, 0))).astype(BF16)
    weights = (norm_g, sb_w_in[0].astype(BF16), sb_w_out[0].astype(BF16), gla_w[:, :main].astype(BF16),
               gla_wa, gla_wa2, gla_b_a[0], gla_norm_g[0], gla_w_out[0].astype(BF16), final_norm_g)

    y_p, k_p, v_p, s_p = _trunk(x_prompt, mods_p, weights, per_row=False,
                                tm_in=min(1024, tp), tm_out=min(256, tp), gla_tt=512)
    cache = (cache_sb_k[0].reshape(bs, past, -1), cache_sb_v[0].reshape(bs, past, -1))
    y_s, k_s, v_s, s_s = _trunk(x_sample, mods_s, weights, per_row=True,
                                tm_in=bs * ts, tm_out=bs * ts, gla_tt=ts, cache=cache, state=state_gla[0])

    heads = lambda a: a.reshape(1, a.shape[0], a.shape[1], SB_HEADS, SB_HEAD_DIM)
    return (y_p, y_s, heads(k_p), heads(v_p), heads(k_s), heads(v_s), s_p[None], s_s[None])
```

```python
import functools
import math

import jax
import jax.numpy as jnp
from jax import lax
from jax.experimental import pallas as pl
from jax.experimental.pallas import tpu as pltpu

F32 = jnp.float32
BF16 = jnp.bfloat16

SB_HEADS = 16
SB_HEAD_DIM = 128
GLA_HEADS = 4
GLA_DK = 256
GLA_DV = 512
GLA_GATE_RANK = 16
GLA_TAU = 16.0
GLA_CHUNK = 64
EPS = 1e-6

V7X_LANES = 128
V7X_VMEM_LIMIT_BYTES = 56 * 1024 * 1024
SB_BLOCK = 256
SB_QUERY_TILE = 1024
SB_UNDERFLOW_LOG2 = 160.0
SB_FINISHED_RUN = 1e30
NORM_ROWS = 16


def _params(semantics):
    return pltpu.CompilerParams(dimension_semantics=semantics,
                                vmem_limit_bytes=V7X_VMEM_LIMIT_BYTES)


def _split_bf16(x):
    hi = x.astype(BF16)
    lo = (x - hi.astype(F32)).astype(BF16)
    return hi, lo


def _log_sigmoid(z):
    return jnp.minimum(z, 0.0) - jnp.log(1.0 + jnp.exp(-jnp.abs(z)))


def _silu(z):
    return z / (1.0 + jnp.exp(-z))


def _ada_kernel(c_ref, w_ref, b_ref, o_ref):
    acc = jnp.dot(c_ref[...].astype(BF16), w_ref[0].astype(BF16), preferred_element_type=F32)
    o_ref[0] = acc + b_ref[0]


def _ada(c_all, w_ada, b_ada, tn=512):
    depth, d, n = w_ada.shape
    rows = c_all.shape[0]
    return pl.pallas_call(
        _ada_kernel,
        grid=(depth, n // tn),
        in_specs=[pl.BlockSpec((rows, d), lambda l, j: (0, 0)),
                  pl.BlockSpec((1, d, tn), lambda l, j: (l, 0, j)),
                  pl.BlockSpec((1, 1, tn), lambda l, j: (l, 0, j))],
        out_specs=pl.BlockSpec((1, rows, tn), lambda l, j: (l, 0, j)),
        out_shape=jax.ShapeDtypeStruct((depth, rows, n), F32),
        compiler_params=_params(("arbitrary", "arbitrary")),
        name="ada",
    )(c_all, w_ada, b_ada.reshape(depth, 1, n))


def _inproj_kernel(*refs, segs, tn, tm, with_decay):
    x_ref, sc_ref, sh_ref, g_ref, w_ref = refs[:5]
    pos = 5
    if with_decay:
        wa_ref, wa2_ref, ba_ref = refs[pos:pos + 3]
        pos += 3
    out_refs = refs[pos:pos + len(segs)]
    pos += len(segs)
    if with_decay:
        dec_ref = refs[pos]
        pos += 1
    h_ref = refs[pos]
    n = pl.program_id(2)

    @pl.when(n == 0)
    def _():
        def norm_rows(r, carry):
            rows = pl.ds(pl.multiple_of(r * NORM_ROWS, NORM_ROWS), NORM_ROWS)
            x = x_ref[0, rows, :]
            y = x * lax.rsqrt(jnp.mean(x * x, axis=-1, keepdims=True) + EPS) * g_ref[...]
            if sc_ref.shape[1] == 1:
                sc, sh = sc_ref[0], sh_ref[0]
            else:
                sc, sh = sc_ref[0, rows, :], sh_ref[0, rows, :]
            h_ref[rows, :] = (y * (1.0 + sc) + sh).astype(BF16)
            return carry
        lax.fori_loop(0, tm // NORM_ROWS, norm_rows, 0)

    acc = jnp.dot(h_ref[...], w_ref[...], preferred_element_type=F32)
    for (start, width, dtype, mult), o_ref in zip(segs, out_refs):
        lo, hi = start // tn, (start + width) // tn

        @pl.when((n >= lo) & (n < hi))
        def _(o_ref=o_ref, dtype=dtype, mult=mult):
            o_ref[0] = (acc if mult is None else acc * mult).astype(dtype)

    if with_decay:
        @pl.when(n == pl.num_programs(2) - 1)
        def _():
            a_lr = jnp.dot(h_ref[...], wa_ref[...], preferred_element_type=F32)
            pre = jnp.dot(a_lr.astype(BF16), wa2_ref[...], preferred_element_type=F32) + ba_ref[...]
            dec_ref[0] = _log_sigmoid(pre) * (1.0 / GLA_TAU)


def _inproj(x, scale, shift, g, w, segs, *, tm, tn, decay=None):
    b, t, d = x.shape
    n_cols = w.shape[1]
    assert t % tm == 0 and n_cols % tn == 0 and tm % NORM_ROWS == 0
    for start, width, _, _ in segs:
        assert start % tn == 0 and width % tn == 0
    mod_rows = scale.shape[1]
    mod_spec = (pl.BlockSpec((1, 1, d), lambda bi, i, j: (bi, 0, 0)) if mod_rows == 1
                else pl.BlockSpec((1, tm, d), lambda bi, i, j: (bi, i, 0)))
    in_specs = [pl.BlockSpec((1, tm, d), lambda bi, i, j: (bi, i, 0)), mod_spec, mod_spec,
                pl.BlockSpec((1, d), lambda bi, i, j: (0, 0)),
                pl.BlockSpec((d, tn), lambda bi, i, j: (0, j))]
    args = [x, scale, shift, g.reshape(1, d), w]
    if decay is not None:
        wa, wa2, ba = decay
        in_specs += [pl.BlockSpec(wa.shape, lambda bi, i, j: (0, 0)),
                     pl.BlockSpec(wa2.shape, lambda bi, i, j: (0, 0)),
                     pl.BlockSpec((1, ba.shape[-1]), lambda bi, i, j: (0, 0))]
        args += [wa, wa2, ba.reshape(1, -1)]

    def seg_spec(start, width):
        first, count = start // tn, width // tn
        return pl.BlockSpec((1, tm, tn),
                            lambda bi, i, j: (bi, i, jnp.clip(j - first, 0, count - 1)))

    out_specs = [seg_spec(s, wd) for s, wd, _, _ in segs]
    out_shape = [jax.ShapeDtypeStruct((b, t, wd), dt) for _, wd, dt, _ in segs]
    if decay is not None:
        nk = decay[1].shape[1]
        out_specs.append(pl.BlockSpec((1, tm, nk), lambda bi, i, j: (bi, i, 0)))
        out_shape.append(jax.ShapeDtypeStruct((b, t, nk), F32))
    return pl.pallas_call(
        functools.partial(_inproj_kernel, segs=tuple(segs), tn=tn, tm=tm, with_decay=decay is not None),
        grid=(b, t // tm, n_cols // tn),
        in_specs=in_specs, out_specs=out_specs, out_shape=out_shape,
        scratch_shapes=[pltpu.VMEM((tm, d), BF16)],
        compiler_params=_params(("arbitrary", "arbitrary", "arbitrary")),
        name="inproj",
    )(*args)


def _sb_block(q, k_blk, v_blk, w, run, mask=None):
    half = V7X_LANES
    z = lax.dot_general(q, k_blk, (((1,), (1,)), ((), ())), preferred_element_type=F32)
    sp = jnp.maximum(z, 0.0) + jnp.log2(1.0 + jnp.exp2(-jnp.abs(z)))
    if mask is not None:
        sp = jnp.where(mask, sp, 0.0)
    hi, lo = _split_bf16(sp)
    cs_r = jnp.dot(jnp.concatenate([hi[:, half:], lo[:, half:]], axis=1), w, preferred_element_type=F32)
    cs_l = jnp.dot(jnp.concatenate([hi[:, :half], lo[:, :half]], axis=1), w, preferred_element_type=F32)
    a_r = jnp.exp2(z[:, half:] - cs_r[:, :half] - run)
    run = run + cs_r[:, half:]
    a_l = jnp.exp2(z[:, :half] - cs_l[:, :half] - run)
    run = run + cs_l[:, half:]
    a = jnp.concatenate([a_l, a_r], axis=1)
    if mask is not None:
        a = jnp.where(mask, a, 0.0)
    return jnp.dot(a.astype(BF16), v_blk, preferred_element_type=F32), run


def _sb_consts():
    j = lax.broadcasted_iota(jnp.int32, (V7X_LANES, V7X_LANES), 0)
    s = lax.broadcasted_iota(jnp.int32, (V7X_LANES, V7X_LANES), 1)
    half = jnp.concatenate([(j >= s).astype(BF16), jnp.ones((V7X_LANES, V7X_LANES), BF16)], axis=1)
    return jnp.concatenate([half, half], axis=0)


def _causal_mask(rows):
    t_idx = lax.broadcasted_iota(jnp.int32, (rows, SB_BLOCK), 0)
    s_idx = lax.broadcasted_iota(jnp.int32, (rows, SB_BLOCK), 1)
    return s_idx < t_idx


def _sb_prompt_kernel(q_ref, k_ref, v_ref, zg_ref, w_ref, o_ref, acc_ref, run_ref, *, tq):
    i = pl.program_id(2)
    groups = tq // SB_BLOCK
    w = w_ref[...]

    def keys(first):
        rows = pl.ds(pl.multiple_of(first, SB_BLOCK), SB_BLOCK)
        return k_ref[0, rows, :].astype(BF16), v_ref[0, rows, :].astype(BF16)

    mask = _causal_mask(SB_BLOCK)
    for g in range(groups):
        rows = slice(g * SB_BLOCK, (g + 1) * SB_BLOCK)
        kb, vb = keys((i * groups + g) * SB_BLOCK)
        pv, run = _sb_block(q_ref[0, rows, :], kb, vb, w, jnp.zeros((SB_BLOCK, V7X_LANES), F32), mask=mask)
        acc_ref[rows, :] = pv
        run_ref[rows, :] = run

    def pending(d):
        low = jnp.float32(jnp.inf)
        for g in range(groups):
            rows = slice(g * SB_BLOCK, (g + 1) * SB_BLOCK)
            low = jnp.minimum(low, jnp.where(i * groups + g - d >= 0, jnp.min(run_ref[rows, :]), jnp.inf))
        return low

    def more(carry):
        return carry[1] < SB_UNDERFLOW_LOG2

    def diagonal(carry):
        d = carry[0]
        for g in range(groups):
            rows = slice(g * SB_BLOCK, (g + 1) * SB_BLOCK)
            blk = i * groups + g - d
            kb, vb = keys(jnp.maximum(blk, 0) * SB_BLOCK)
            run = jnp.where(blk >= 0, run_ref[rows, :], SB_FINISHED_RUN)
            pv, run = _sb_block(q_ref[0, rows, :], kb, vb, w, run)
            acc_ref[rows, :] += pv
            run_ref[rows, :] = run
        return d + 1, pending(d + 1)

    lax.while_loop(more, diagonal, (jnp.int32(1), pending(1)))
    o_ref[0] = (acc_ref[...] * _silu(zg_ref[0].astype(F32))).astype(o_ref.dtype)


def _sb_prompt(q, k, v, zg, tq):
    b, t, _ = q.shape
    assert t % tq == 0 and tq % SB_BLOCK == 0
    tile = pl.BlockSpec((1, tq, SB_HEAD_DIM), lambda bi, h, i: (bi, i, h))
    whole = pl.BlockSpec((1, t, SB_HEAD_DIM), lambda bi, h, i: (bi, 0, h))
    return pl.pallas_call(
        functools.partial(_sb_prompt_kernel, tq=tq),
        grid=(b, SB_HEADS, t // tq),
        in_specs=[tile, whole, whole, tile, pl.BlockSpec((SB_BLOCK, SB_BLOCK), lambda bi, h, i: (0, 0))],
        out_specs=tile,
        out_shape=jax.ShapeDtypeStruct(q.shape, BF16),
        scratch_shapes=[pltpu.VMEM((tq, SB_HEAD_DIM), F32), pltpu.VMEM((tq, V7X_LANES), F32)],
        compiler_params=_params(("arbitrary", "arbitrary", "arbitrary")),
        name="sb_prompt",
    )(q, k, v, zg, _sb_consts())


def _sb_sample_kernel(q_ref, kn_ref, vn_ref, kp_ref, vp_ref, zg_ref, w_ref, o_ref, kpad_ref, vpad_ref):
    t_new = q_ref.shape[1]
    q = q_ref[0]
    w = w_ref[...]
    kpad_ref[...] = jnp.zeros_like(kpad_ref)
    vpad_ref[...] = jnp.zeros_like(vpad_ref)
    kpad_ref[0:t_new, :] = kn_ref[0]
    vpad_ref[0:t_new, :] = vn_ref[0]
    acc, run = _sb_block(q, kpad_ref[...].astype(BF16), vpad_ref[...].astype(BF16), w,
                         jnp.zeros((t_new, V7X_LANES), F32), mask=_causal_mask(t_new))
    for j in range(kp_ref.shape[1] // SB_BLOCK - 1, -1, -1):
        rows = pl.ds(j * SB_BLOCK, SB_BLOCK)
        pv, run = _sb_block(q, kp_ref[0, rows, :].astype(BF16), vp_ref[0, rows, :].astype(BF16), w, run)
        acc = acc + pv
    o_ref[0] = (acc * _silu(zg_ref[0].astype(F32))).astype(o_ref.dtype)


def _sb_sample(q, k_new, v_new, k_past, v_past, zg):
    b, t, _ = q.shape
    p = k_past.shape[1]
    assert p % SB_BLOCK == 0 and t <= SB_BLOCK
    new = pl.BlockSpec((1, t, SB_HEAD_DIM), lambda bi, h: (bi, 0, h))
    past = pl.BlockSpec((1, p, SB_HEAD_DIM), lambda bi, h: (bi, 0, h))
    return pl.pallas_call(
        _sb_sample_kernel,
        grid=(b, SB_HEADS),
        in_specs=[new, new, new, past, past, new, pl.BlockSpec((SB_BLOCK, SB_BLOCK), lambda bi, h: (0, 0))],
        out_specs=new,
        out_shape=jax.ShapeDtypeStruct(q.shape, BF16),
        scratch_shapes=[pltpu.VMEM((SB_BLOCK, SB_HEAD_DIM), F32), pltpu.VMEM((SB_BLOCK, SB_HEAD_DIM), F32)],
        compiler_params=_params(("arbitrary", "arbitrary")),
        name="sb_sample",
    )(q, k_new, v_new, k_past, v_past, zg, _sb_consts())


def _gla_kernel(*refs, chunk, n_chunks, has_state):
    if has_state:
        q_ref, k_ref, v_ref, dec_ref, r_ref, gn_ref, tril_ref, ones_ref, s0_ref, o_ref, sf_ref, s_ref = refs
    else:
        q_ref, k_ref, v_ref, dec_ref, r_ref, gn_ref, tril_ref, ones_ref, o_ref, sf_ref, s_ref = refs
    step = pl.program_id(2)

    @pl.when(step == 0)
    def _():
        s_ref[...] = s0_ref[0, 0] if has_state else jnp.zeros_like(s_ref)

    tril = tril_ref[...]
    ones = ones_ref[...]
    causal = (lax.broadcasted_iota(jnp.int32, (chunk, chunk), 1)
              <= lax.broadcasted_iota(jnp.int32, (chunk, chunk), 0))
    q_scale = GLA_DK ** -0.5
    tdot = lambda a, b: lax.dot_general(a, b, (((0,), (0,)), ((), ())), preferred_element_type=F32)

    def one_chunk(c, carry):
        rows = pl.ds(pl.multiple_of(c * chunk, chunk), chunk)
        q = q_ref[0, rows, :] * q_scale
        k = k_ref[0, rows, :]
        v = v_ref[0, rows, :]
        g_hi, g_lo = _split_bf16(dec_ref[0, rows, :])
        b = jnp.dot(tril, g_hi, preferred_element_type=F32) + jnp.dot(tril, g_lo, preferred_element_type=F32)
        b_last = b[chunk - 1:chunk, :]
        b_last_rows = tdot(g_hi, ones) + tdot(g_lo, ones)
        q_in = (q * jnp.exp(b)).astype(BF16)
        k_out = (k * jnp.exp(-b)).astype(BF16)
        k_end = (k * jnp.exp(b_last - b)).astype(BF16)
        a = lax.dot_general(q_in, k_out, (((1,), (1,)), ((), ())), preferred_element_type=F32)
        a = jnp.where(causal, a, 0.0).astype(BF16)
        s_old = s_ref[...]
        o = (jnp.dot(a, v, preferred_element_type=F32)
             + jnp.dot(q_in, s_old.astype(BF16), preferred_element_type=F32))
        decay = jnp.exp(b_last_rows)
        decay = jnp.concatenate([decay] * (GLA_DV // V7X_LANES), axis=1)
        s_ref[...] = decay * s_old + tdot(k_end, v)
        o = o * lax.rsqrt(jnp.mean(o * o, axis=-1, keepdims=True) + EPS) * gn_ref[...]
        o_ref[0, rows, :] = (o * _silu(r_ref[0, rows, :].astype(F32))).astype(o_ref.dtype)
        return carry

    lax.fori_loop(0, n_chunks, one_chunk, 0)

    @pl.when(step == pl.num_programs(2) - 1)
    def _():
        sf_ref[0, 0] = s_ref[...]


def _gla(qk, v, dec, r, gn, s0, *, chunk, tt):
    b, t, _ = v.shape
    assert t % tt == 0 and tt % chunk == 0
    tril = (lax.broadcasted_iota(jnp.int32, (chunk, chunk), 1)
            <= lax.broadcasted_iota(jnp.int32, (chunk, chunk), 0)).astype(BF16)
    ones = jnp.ones((chunk, V7X_LANES), BF16)
    kspec = lambda off: pl.BlockSpec((1, tt, GLA_DK), lambda bi, h, s: (bi, s, h + off))
    vspec = pl.BlockSpec((1, tt, GLA_DV), lambda bi, h, s: (bi, s, h))
    sspec = pl.BlockSpec((1, 1, GLA_DK, GLA_DV), lambda bi, h, s: (bi, h, 0, 0))
    in_specs = [kspec(0), kspec(GLA_HEADS), vspec, kspec(0), vspec,
                pl.BlockSpec((1, GLA_DV), lambda bi, h, s: (0, h)),
                pl.BlockSpec((chunk, chunk), lambda bi, h, s: (0, 0)),
                pl.BlockSpec((chunk, V7X_LANES), lambda bi, h, s: (0, 0))]
    args = [qk, qk, v, dec, r, gn.reshape(1, -1), tril, ones]
    if s0 is not None:
        in_specs.append(sspec)
        args.append(s0)
    return pl.pallas_call(
        functools.partial(_gla_kernel, chunk=chunk, n_chunks=tt // chunk, has_state=s0 is not None),
        grid=(b, GLA_HEADS, t // tt),
        in_specs=in_specs,
        out_specs=[vspec, sspec],
        out_shape=[jax.ShapeDtypeStruct(v.shape, BF16),
                   jax.ShapeDtypeStruct((b, GLA_HEADS, GLA_DK, GLA_DV), F32)],
        scratch_shapes=[pltpu.VMEM((GLA_DK, GLA_DV), F32)],
        compiler_params=_params(("arbitrary", "arbitrary", "arbitrary")),
        name="gla",
    )(*args)


def _outproj_kernel(y_ref, w_ref, x_ref, gate_ref, *rest, final_norm):
    if final_norm:
        gf_ref, o_ref = rest
    else:
        (o_ref,) = rest
    y = jnp.dot(y_ref[0], w_ref[...], preferred_element_type=F32)
    x = x_ref[0] + gate_ref[0] * y
    if final_norm:
        x = x * lax.rsqrt(jnp.mean(x * x, axis=-1, keepdims=True) + EPS) * gf_ref[...]
    o_ref[0] = x


def _outproj(y, w, x, gate, gf, *, tm):
    b, t, d = x.shape
    kdim = y.shape[-1]
    assert t % tm == 0
    gate_spec = (pl.BlockSpec((1, 1, d), lambda bi, i: (bi, 0, 0)) if gate.shape[1] == 1
                 else pl.BlockSpec((1, tm, d), lambda bi, i: (bi, i, 0)))
    in_specs = [pl.BlockSpec((1, tm, kdim), lambda bi, i: (bi, i, 0)),
                pl.BlockSpec((kdim, d), lambda bi, i: (0, 0)),
                pl.BlockSpec((1, tm, d), lambda bi, i: (bi, i, 0)),
                gate_spec]
    args = [y, w, x, gate]
    if gf is not None:
        in_specs.append(pl.BlockSpec((1, d), lambda bi, i: (0, 0)))
        args.append(gf.reshape(1, d))
    return pl.pallas_call(
        functools.partial(_outproj_kernel, final_norm=gf is not None),
        grid=(b, t // tm),
        in_specs=in_specs,
        out_specs=pl.BlockSpec((1, tm, d), lambda bi, i: (bi, i, 0)),
        out_shape=jax.ShapeDtypeStruct(x.shape, F32),
        compiler_params=_params(("arbitrary", "arbitrary")),
        name="outproj",
    )(*args)


def _trunk(x, mods, weights, *, per_row, tm_in, tm_out, gla_tt, cache=None, state=None):
    (norm_g, sb_w_in, sb_w_out, gla_w_in, gla_wa, gla_wa2, gla_b_a, gla_norm_g, gla_w_out,
     final_norm_g) = weights
    b, t, d = x.shape
    if per_row:
        fold = lambda a: a.reshape(1, b * t, a.shape[-1])
        rows = lambda m: jnp.broadcast_to(m[:, None, :], (b, t, d)).reshape(1, b * t, d)
    else:
        fold = lambda a: a
        rows = lambda m: m[:, None, :]
    unfold = lambda a: a.reshape(b, t, a.shape[-1])
    width = SB_HEADS * SB_HEAD_DIM

    shift, scale, gate = mods[0]
    q, k, v, zg = _inproj(
        fold(x), rows(scale), rows(shift), norm_g[0], sb_w_in,
        [(0, width, BF16, SB_HEAD_DIM ** -0.5 * math.log2(math.e)), (width, width, F32, None),
         (2 * width, width, F32, None), (3 * width, width, BF16, None)],
        tm=tm_in, tn=512)
    q, k, v, zg = unfold(q), unfold(k), unfold(v), unfold(zg)
    if cache is None:
        branch = _sb_prompt(q, k, v, zg, tq=min(SB_QUERY_TILE, t))
    else:
        branch = _sb_sample(q, k, v, cache[0], cache[1], zg)
    x1 = _outproj(fold(branch), sb_w_out, fold(x), rows(gate), None, tm=tm_out)

    shift, scale, gate = mods[1]
    nk, nv = GLA_HEADS * GLA_DK, GLA_HEADS * GLA_DV
    qk, gv, r, dec = _inproj(
        x1, rows(scale), rows(shift), norm_g[1], gla_w_in,
        [(0, 2 * nk, F32, None), (2 * nk, nv, BF16, None), (2 * nk + nv, nv, BF16, None)],
        tm=tm_in, tn=512, decay=(gla_wa, gla_wa2, gla_b_a))
    chunk = min(GLA_CHUNK, t)
    branch, s_new = _gla(unfold(qk), unfold(gv), unfold(dec), unfold(r), gla_norm_g, state,
                         chunk=chunk, tt=min(gla_tt, t))
    y = _outproj(fold(branch), gla_w_out, x1, rows(gate), final_norm_g, tm=tm_out)
    return unfold(y), k, v, s_new


def kernel(x_prompt, x_sample, cache_sb_k, cache_sb_v, state_gla, c_prompt, c_sample, w_ada, b_ada, norm_g,
           sb_w_in, sb_w_out, gla_w_in, gla_w_a2, gla_b_a, gla_norm_g, gla_w_out, final_norm_g):
    depth, d, _ = w_ada.shape
    assert depth == 2 and sb_w_in.shape[0] == 1 and gla_w_in.shape[0] == 1
    bp, tp, _ = x_prompt.shape
    bs, ts, _ = x_sample.shape
    past = cache_sb_k.shape[2]
    nk, nv = GLA_HEADS * GLA_DK, GLA_HEADS * GLA_DV
    main = 2 * nk + 2 * nv

    mod = _ada(jnp.concatenate([c_prompt, c_sample], axis=0), w_ada, b_ada)
    split = lambda m: (m[:, :d], m[:, d:2 * d], m[:, 2 * d:])
    mods_p = [split(mod[l, :bp]) for l in range(depth)]
    mods_s = [split(mod[l, bp:]) for l in range(depth)]

    gla_w = gla_w_in[0]
    gla_wa = jnp.pad(gla_w[:, main:], ((0, 0), (0, V7X_LANES - GLA_GATE_RANK))).astype(BF16)
    gla_wa2 = jnp.pad(gla_w_a2[0], ((0, V7X_LANES - GLA_GATE_RANK), (0, 0))).astype(BF16)
    weights = (norm_g, sb_w_in[0].astype(BF16), sb_w_out[0].astype(BF16), gla_w[:, :main].astype(BF16),
               gla_wa, gla_wa2, gla_b_a[0], gla_norm_g[0], gla_w_out[0].astype(BF16), final_norm_g)

    y_p, k_p, v_p, s_p = _trunk(x_prompt, mods_p, weights, per_row=False,
                                tm_in=min(1024, tp), tm_out=min(256, tp), gla_tt=512)
    cache = (cache_sb_k[0].reshape(bs, past, -1), cache_sb_v[0].reshape(bs, past, -1))
    y_s, k_s, v_s, s_s = _trunk(x_sample, mods_s, weights, per_row=True,
                                tm_in=bs * ts, tm_out=bs * ts, gla_tt=ts, cache=cache, state=state_gla[0])

    heads = lambda a: a.reshape(1, a.shape[0], a.shape[1], SB_HEADS, SB_HEAD_DIM)
    return (y_p, y_s, heads(k_p), heads(v_p), heads(k_s), heads(v_s), s_p[None], s_s[None])
```

```python
import functools
import math

import jax
import jax.numpy as jnp
from jax import lax
from jax.experimental import pallas as pl
from jax.experimental.pallas import tpu as pltpu

F32 = jnp.float32
BF16 = jnp.bfloat16

SB_HEADS = 16
SB_HEAD_DIM = 128
GLA_HEADS = 4
GLA_DK = 256
GLA_DV = 512
GLA_GATE_RANK = 16
GLA_TAU = 16.0
GLA_CHUNK = 64
EPS = 1e-6

V7X_LANES = 128
V7X_VMEM_LIMIT_BYTES = 56 * 1024 * 1024
SB_BLOCK = 256
SB_QUERY_TILE = 1024
SB_UNDERFLOW_LOG2 = 160.0
SB_FINISHED_RUN = 1e30
NORM_ROWS = 16
NORM_UNROLL = 4


def _params(semantics):
    return pltpu.CompilerParams(dimension_semantics=semantics,
                                vmem_limit_bytes=V7X_VMEM_LIMIT_BYTES)


def _split_bf16(x):
    hi = x.astype(BF16)
    lo = (x - hi.astype(F32)).astype(BF16)
    return hi, lo


def _log_sigmoid(z):
    return jnp.minimum(z, 0.0) - jnp.log(1.0 + jnp.exp(-jnp.abs(z)))


def _silu(z):
    return z / (1.0 + jnp.exp(-z))


def _ada_kernel(c_ref, w_ref, b_ref, o_ref):
    acc = jnp.dot(c_ref[...].astype(BF16), w_ref[0].astype(BF16), preferred_element_type=F32)
    o_ref[0] = acc + b_ref[0]


def _ada(c_all, w_ada, b_ada, tn=512):
    depth, d, n = w_ada.shape
    rows = c_all.shape[0]
    return pl.pallas_call(
        _ada_kernel,
        grid=(depth, n // tn),
        in_specs=[pl.BlockSpec((rows, d), lambda l, j: (0, 0)),
                  pl.BlockSpec((1, d, tn), lambda l, j: (l, 0, j)),
                  pl.BlockSpec((1, 1, tn), lambda l, j: (l, 0, j))],
        out_specs=pl.BlockSpec((1, rows, tn), lambda l, j: (l, 0, j)),
        out_shape=jax.ShapeDtypeStruct((depth, rows, n), F32),
        compiler_params=_params(("arbitrary", "arbitrary")),
        name="ada",
    )(c_all, w_ada, b_ada.reshape(depth, 1, n))


def _inproj_kernel(*refs, n_f32, bf16_mults, tm, with_decay):
    x_ref, sc_ref, sh_ref, g_ref, w_ref = refs[:5]
    pos = 5
    if with_decay:
        wa_ref, wa2_ref, ba_ref = refs[pos:pos + 3]
        pos += 3
    of_ref, ob_ref = refs[pos:pos + 2]
    pos += 2
    if with_decay:
        dec_ref = refs[pos]
        pos += 1
    h_ref = refs[pos]
    n = pl.program_id(2)

    @pl.when(n == 0)
    def _():
        def norm_rows(r, carry):
            rows = pl.ds(pl.multiple_of(r * NORM_ROWS, NORM_ROWS), NORM_ROWS)
            x = x_ref[0, rows, :]
            y = x * lax.rsqrt(jnp.mean(x * x, axis=-1, keepdims=True) + EPS) * g_ref[...]
            if sc_ref.shape[1] == 1:
                sc, sh = sc_ref[0], sh_ref[0]
            else:
                sc, sh = sc_ref[0, rows, :], sh_ref[0, rows, :]
            h_ref[rows, :] = (y * (1.0 + sc) + sh).astype(BF16)
            return carry
        steps = tm // NORM_ROWS
        lax.fori_loop(0, steps, norm_rows, 0, unroll=NORM_UNROLL if steps % NORM_UNROLL == 0 else 1)

    @pl.when(n < n_f32)
    def _():
        of_ref[0, 0] = jnp.dot(h_ref[...], w_ref[...], preferred_element_type=F32)

    @pl.when(n >= n_f32)
    def _():
        acc = jnp.dot(h_ref[...], w_ref[...], preferred_element_type=F32)
        if any(m is not None for m in bf16_mults):
            mult = jnp.float32(1.0)
            for idx, m in enumerate(bf16_mults):
                if m is not None:
                    mult = jnp.where(n == n_f32 + idx, jnp.float32(m), mult)
            acc = acc * mult
        ob_ref[0, 0] = acc.astype(BF16)

    if with_decay:
        @pl.when(n == pl.num_programs(2) - 1)
        def _():
            a_lr = jnp.dot(h_ref[...], wa_ref[...], preferred_element_type=F32)
            pre = jnp.dot(a_lr.astype(BF16), wa2_ref[...], preferred_element_type=F32) + ba_ref[...]
            dec_ref[0] = _log_sigmoid(pre) * (1.0 / GLA_TAU)


def _inproj(x, scale, shift, g, w, n_f32, bf16_mults, *, tm, decay=None):
    b, t, d = x.shape
    n_seg = w.shape[1] // d
    n_bf16 = len(bf16_mults)
    assert t % tm == 0 and w.shape[1] == n_seg * d and n_seg == n_f32 + n_bf16
    assert tm % NORM_ROWS == 0
    mod_rows = scale.shape[1]
    mod_spec = (pl.BlockSpec((1, 1, d), lambda bi, i, j: (bi, 0, 0)) if mod_rows == 1
                else pl.BlockSpec((1, tm, d), lambda bi, i, j: (bi, i, 0)))
    in_specs = [pl.BlockSpec((1, tm, d), lambda bi, i, j: (bi, i, 0)), mod_spec, mod_spec,
                pl.BlockSpec((1, d), lambda bi, i, j: (0, 0)),
                pl.BlockSpec((d, d), lambda bi, i, j: (0, j))]
    args = [x, scale, shift, g.reshape(1, d), w]
    if decay is not None:
        wa, wa2, ba = decay
        in_specs += [pl.BlockSpec(wa.shape, lambda bi, i, j: (0, 0)),
                     pl.BlockSpec(wa2.shape, lambda bi, i, j: (0, 0)),
                     pl.BlockSpec((1, ba.shape[-1]), lambda bi, i, j: (0, 0))]
        args += [wa, wa2, ba.reshape(1, -1)]

    out_specs = [pl.BlockSpec((1, 1, tm, d), lambda bi, i, j: (jnp.minimum(j, n_f32 - 1), bi, i, 0)),
                 pl.BlockSpec((1, 1, tm, d), lambda bi, i, j: (jnp.maximum(j - n_f32, 0), bi, i, 0))]
    out_shape = [jax.ShapeDtypeStruct((n_f32, b, t, d), F32), jax.ShapeDtypeStruct((n_bf16, b, t, d), BF16)]
    if decay is not None:
        nk = decay[1].shape[1]
        out_specs.append(pl.BlockSpec((1, tm, nk), lambda bi, i, j: (bi, i, 0)))
        out_shape.append(jax.ShapeDtypeStruct((b, t, nk), F32))
    return pl.pallas_call(
        functools.partial(_inproj_kernel, n_f32=n_f32, bf16_mults=tuple(bf16_mults), tm=tm,
                          with_decay=decay is not None),
        grid=(b, t // tm, n_seg),
        in_specs=in_specs, out_specs=out_specs, out_shape=out_shape,
        scratch_shapes=[pltpu.VMEM((tm, d), BF16)],
        compiler_params=_params(("arbitrary", "arbitrary", "arbitrary")),
        name="inproj",
    )(*args)


def _sb_block(q, k_blk, v_blk, w, run, mask=None):
    half = V7X_LANES
    z = lax.dot_general(q, k_blk, (((1,), (1,)), ((), ())), preferred_element_type=F32)
    sp = jnp.maximum(z, 0.0) + jnp.log2(1.0 + jnp.exp2(-jnp.abs(z)))
    if mask is not None:
        sp = jnp.where(mask, sp, 0.0)
    hi, lo = _split_bf16(sp)
    cs_r = jnp.dot(jnp.concatenate([hi[:, half:], lo[:, half:]], axis=1), w, preferred_element_type=F32)
    cs_l = jnp.dot(jnp.concatenate([hi[:, :half], lo[:, :half]], axis=1), w, preferred_element_type=F32)
    a_r = jnp.exp2(z[:, half:] - cs_r[:, :half] - run)
    run = run + cs_r[:, half:]
    a_l = jnp.exp2(z[:, :half] - cs_l[:, :half] - run)
    run = run + cs_l[:, half:]
    a = jnp.concatenate([a_l, a_r], axis=1)
    if mask is not None:
        a = jnp.where(mask, a, 0.0)
    return jnp.dot(a.astype(BF16), v_blk, preferred_element_type=F32), run


def _sb_consts():
    j = lax.broadcasted_iota(jnp.int32, (V7X_LANES, V7X_LANES), 0)
    s = lax.broadcasted_iota(jnp.int32, (V7X_LANES, V7X_LANES), 1)
    half = jnp.concatenate([(j >= s).astype(BF16), jnp.ones((V7X_LANES, V7X_LANES), BF16)], axis=1)
    return jnp.concatenate([half, half], axis=0)


def _causal_mask(rows):
    t_idx = lax.broadcasted_iota(jnp.int32, (rows, SB_BLOCK), 0)
    s_idx = lax.broadcasted_iota(jnp.int32, (rows, SB_BLOCK), 1)
    return s_idx < t_idx


def _sb_prompt_kernel(q_ref, k_ref, v_ref, zg_ref, w_ref, o_ref, acc_ref, run_ref, *, tq):
    i = pl.program_id(2)
    groups = tq // SB_BLOCK
    w = w_ref[...]

    def keys(first):
        rows = pl.ds(pl.multiple_of(first, SB_BLOCK), SB_BLOCK)
        return k_ref[0, rows, :].astype(BF16), v_ref[0, rows, :].astype(BF16)

    mask = _causal_mask(SB_BLOCK)
    for g in range(groups):
        rows = slice(g * SB_BLOCK, (g + 1) * SB_BLOCK)
        kb, vb = keys((i * groups + g) * SB_BLOCK)
        pv, run = _sb_block(q_ref[0, rows, :], kb, vb, w, jnp.zeros((SB_BLOCK, V7X_LANES), F32), mask=mask)
        acc_ref[rows, :] = pv
        run_ref[rows, :] = run

    def pending(d):
        low = jnp.float32(jnp.inf)
        for g in range(groups):
            rows = slice(g * SB_BLOCK, (g + 1) * SB_BLOCK)
            low = jnp.minimum(low, jnp.where(i * groups + g - d >= 0, jnp.min(run_ref[rows, :]), jnp.inf))
        return low

    def more(carry):
        return carry[1] < SB_UNDERFLOW_LOG2

    def diagonal(carry):
        d = carry[0]
        for g in range(groups):
            rows = slice(g * SB_BLOCK, (g + 1) * SB_BLOCK)
            blk = i * groups + g - d
            kb, vb = keys(jnp.maximum(blk, 0) * SB_BLOCK)
            run = jnp.where(blk >= 0, run_ref[rows, :], SB_FINISHED_RUN)
            pv, run = _sb_block(q_ref[0, rows, :], kb, vb, w, run)
            acc_ref[rows, :] += pv
            run_ref[rows, :] = run
        return d + 1, pending(d + 1)

    lax.while_loop(more, diagonal, (jnp.int32(1), pending(1)))
    o_ref[0] = (acc_ref[...] * _silu(zg_ref[0].astype(F32))).astype(o_ref.dtype)


def _sb_prompt(q, k, v, zg, tq):
    b, t, _ = q.shape
    assert t % tq == 0 and tq % SB_BLOCK == 0
    tile = pl.BlockSpec((1, tq, SB_HEAD_DIM), lambda bi, h, i: (bi, i, h))
    whole = pl.BlockSpec((1, t, SB_HEAD_DIM), lambda bi, h, i: (bi, 0, h))
    return pl.pallas_call(
        functools.partial(_sb_prompt_kernel, tq=tq),
        grid=(b, SB_HEADS, t // tq),
        in_specs=[tile, whole, whole, tile, pl.BlockSpec((SB_BLOCK, SB_BLOCK), lambda bi, h, i: (0, 0))],
        out_specs=tile,
        out_shape=jax.ShapeDtypeStruct(q.shape, BF16),
        scratch_shapes=[pltpu.VMEM((tq, SB_HEAD_DIM), F32), pltpu.VMEM((tq, V7X_LANES), F32)],
        compiler_params=_params(("arbitrary", "arbitrary", "arbitrary")),
        name="sb_prompt",
    )(q, k, v, zg, _sb_consts())


def _sb_sample_kernel(q_ref, kn_ref, vn_ref, kp_ref, vp_ref, zg_ref, w_ref, o_ref, kpad_ref, vpad_ref):
    t_new = q_ref.shape[1]
    q = q_ref[0]
    w = w_ref[...]
    kpad_ref[...] = jnp.zeros_like(kpad_ref)
    vpad_ref[...] = jnp.zeros_like(vpad_ref)
    kpad_ref[0:t_new, :] = kn_ref[0]
    vpad_ref[0:t_new, :] = vn_ref[0]
    acc, run = _sb_block(q, kpad_ref[...].astype(BF16), vpad_ref[...].astype(BF16), w,
                         jnp.zeros((t_new, V7X_LANES), F32), mask=_causal_mask(t_new))
    for j in range(kp_ref.shape[1] // SB_BLOCK - 1, -1, -1):
        rows = pl.ds(j * SB_BLOCK, SB_BLOCK)
        pv, run = _sb_block(q, kp_ref[0, rows, :].astype(BF16), vp_ref[0, rows, :].astype(BF16), w, run)
        acc = acc + pv
    o_ref[0] = (acc * _silu(zg_ref[0].astype(F32))).astype(o_ref.dtype)


def _sb_sample(q, k_new, v_new, k_past, v_past, zg):
    b, t, _ = q.shape
    p = k_past.shape[1]
    assert p % SB_BLOCK == 0 and t <= SB_BLOCK
    new = pl.BlockSpec((1, t, SB_HEAD_DIM), lambda bi, h: (bi, 0, h))
    past = pl.BlockSpec((1, p, SB_HEAD_DIM), lambda bi, h: (bi, 0, h))
    return pl.pallas_call(
        _sb_sample_kernel,
        grid=(b, SB_HEADS),
        in_specs=[new, new, new, past, past, new, pl.BlockSpec((SB_BLOCK, SB_BLOCK), lambda bi, h: (0, 0))],
        out_specs=new,
        out_shape=jax.ShapeDtypeStruct(q.shape, BF16),
        scratch_shapes=[pltpu.VMEM((SB_BLOCK, SB_HEAD_DIM), F32), pltpu.VMEM((SB_BLOCK, SB_HEAD_DIM), F32)],
        compiler_params=_params(("arbitrary", "arbitrary")),
        name="sb_sample",
    )(q, k_new, v_new, k_past, v_past, zg, _sb_consts())


def _gla_kernel(*refs, chunk, n_chunks, has_state):
    if has_state:
        q_ref, k_ref, v_ref, dec_ref, r_ref, gn_ref, tril_ref, s0_ref, o_ref, sf_ref, st_ref, upd_ref = refs
    else:
        q_ref, k_ref, v_ref, dec_ref, r_ref, gn_ref, tril_ref, o_ref, sf_ref, st_ref, upd_ref = refs
    step = pl.program_id(2)

    @pl.when(step == 0)
    def _():
        st_ref[...] = s0_ref[0, 0].T if has_state else jnp.zeros_like(st_ref)

    tril2 = tril_ref[...]
    causal = (lax.broadcasted_iota(jnp.int32, (chunk, chunk), 1)
              <= lax.broadcasted_iota(jnp.int32, (chunk, chunk), 0))
    q_scale = GLA_DK ** -0.5
    nt_dot = lambda a, b: lax.dot_general(a, b, (((1,), (1,)), ((), ())), preferred_element_type=F32)
    chunks = [pl.ds(c * chunk, chunk) for c in range(n_chunks)]

    q_in, k_out, k_end, decay = [], [], [], []
    for rows in chunks:
        g_hi, g_lo = _split_bf16(dec_ref[0, rows, :])
        b = jnp.dot(tril2, jnp.concatenate([g_hi, g_lo], axis=0), preferred_element_type=F32)
        b_last = b[chunk - 1:chunk, :]
        k = k_ref[0, rows, :]
        q_in.append((q_ref[0, rows, :] * q_scale * jnp.exp(b)).astype(BF16))
        k_out.append((k * jnp.exp(-b)).astype(BF16))
        k_end.append((k * jnp.exp(b_last - b)).astype(BF16))
        decay.append(jnp.exp(b_last))
    intra = []
    for c, rows in enumerate(chunks):
        v = v_ref[0, rows, :]
        a = jnp.where(causal, nt_dot(q_in[c], k_out[c]), 0.0).astype(BF16)
        intra.append(jnp.dot(a, v, preferred_element_type=F32))
        upd_ref[c] = lax.dot_general(v, k_end[c], (((0,), (0,)), ((), ())), preferred_element_type=F32)
    for c, rows in enumerate(chunks):
        state_t = st_ref[...]
        o = intra[c] + nt_dot(q_in[c], state_t.astype(BF16))
        st_ref[...] = state_t * decay[c] + upd_ref[c]
        o = o * lax.rsqrt(jnp.mean(o * o, axis=-1, keepdims=True) + EPS) * gn_ref[...]
        o_ref[0, rows, :] = (o * _silu(r_ref[0, rows, :].astype(F32))).astype(o_ref.dtype)

    @pl.when(step == pl.num_programs(2) - 1)
    def _():
        sf_ref[0, 0] = st_ref[...].T


def _gla(qk, v, dec, r, gn, s0, *, chunk, tt):
    b, t, _ = v.shape
    assert t % tt == 0 and tt % chunk == 0
    tril = (lax.broadcasted_iota(jnp.int32, (chunk, chunk), 1)
            <= lax.broadcasted_iota(jnp.int32, (chunk, chunk), 0)).astype(BF16)
    tril2 = jnp.concatenate([tril, tril], axis=1)
    n_chunks = tt // chunk
    kspec = lambda off: pl.BlockSpec((1, tt, GLA_DK), lambda bi, h, s: (bi, s, h + off))
    vspec = pl.BlockSpec((1, tt, GLA_DV), lambda bi, h, s: (bi, s, h))
    sspec = pl.BlockSpec((1, 1, GLA_DK, GLA_DV), lambda bi, h, s: (bi, h, 0, 0))
    in_specs = [kspec(0), kspec(GLA_HEADS), vspec, kspec(0), vspec,
                pl.BlockSpec((1, GLA_DV), lambda bi, h, s: (0, h)),
                pl.BlockSpec((chunk, 2 * chunk), lambda bi, h, s: (0, 0))]
    args = [qk, qk, v, dec, r, gn.reshape(1, -1), tril2]
    if s0 is not None:
        in_specs.append(sspec)
        args.append(s0)
    return pl.pallas_call(
        functools.partial(_gla_kernel, chunk=chunk, n_chunks=n_chunks, has_state=s0 is not None),
        grid=(b, GLA_HEADS, t // tt),
        in_specs=in_specs,
        out_specs=[vspec, sspec],
        out_shape=[jax.ShapeDtypeStruct(v.shape, BF16),
                   jax.ShapeDtypeStruct((b, GLA_HEADS, GLA_DK, GLA_DV), F32)],
        scratch_shapes=[pltpu.VMEM((GLA_DV, GLA_DK), F32), pltpu.VMEM((n_chunks, GLA_DV, GLA_DK), F32)],
        compiler_params=_params(("arbitrary", "arbitrary", "arbitrary")),
        name="gla",
    )(*args)


def _outproj_kernel(y_ref, w_ref, x_ref, gate_ref, *rest, final_norm):
    if final_norm:
        gf_ref, o_ref = rest
    else:
        (o_ref,) = rest
    y = jnp.dot(y_ref[0], w_ref[...], preferred_element_type=F32)
    x = x_ref[0] + gate_ref[0] * y
    if final_norm:
        x = x * lax.rsqrt(jnp.mean(x * x, axis=-1, keepdims=True) + EPS) * gf_ref[...]
    o_ref[0] = x


def _outproj(y, w, x, gate, gf, *, tm):
    b, t, d = x.shape
    kdim = y.shape[-1]
    assert t % tm == 0
    gate_spec = (pl.BlockSpec((1, 1, d), lambda bi, i: (bi, 0, 0)) if gate.shape[1] == 1
                 else pl.BlockSpec((1, tm, d), lambda bi, i: (bi, i, 0)))
    in_specs = [pl.BlockSpec((1, tm, kdim), lambda bi, i: (bi, i, 0)),
                pl.BlockSpec((kdim, d), lambda bi, i: (0, 0)),
                pl.BlockSpec((1, tm, d), lambda bi, i: (bi, i, 0)),
                gate_spec]
    args = [y, w, x, gate]
    if gf is not None:
        in_specs.append(pl.BlockSpec((1, d), lambda bi, i: (0, 0)))
        args.append(gf.reshape(1, d))
    return pl.pallas_call(
        functools.partial(_outproj_kernel, final_norm=gf is not None),
        grid=(b, t // tm),
        in_specs=in_specs,
        out_specs=pl.BlockSpec((1, tm, d), lambda bi, i: (bi, i, 0)),
        out_shape=jax.ShapeDtypeStruct(x.shape, F32),
        compiler_params=_params(("arbitrary", "arbitrary")),
        name="outproj",
    )(*args)


def _trunk(x, mods, weights, *, per_row, tm_in, tm_out, gla_tt, cache=None, state=None):
    (norm_g, sb_w_in, sb_w_out, gla_w_in, gla_wa, gla_wa2, gla_b_a, gla_norm_g, gla_w_out,
     final_norm_g) = weights
    b, t, d = x.shape
    if per_row:
        fold = lambda a: a.reshape(1, b * t, a.shape[-1])
        rows = lambda m: jnp.broadcast_to(m[:, None, :], (b, t, d)).reshape(1, b * t, d)
    else:
        fold = lambda a: a
        rows = lambda m: m[:, None, :]
    unfold = lambda a: a.reshape(b, t, a.shape[-1])
    width = SB_HEADS * SB_HEAD_DIM

    shift, scale, gate = mods[0]
    assert width == d
    kv, qz = _inproj(fold(x), rows(scale), rows(shift), norm_g[0], sb_w_in, 2,
                     (SB_HEAD_DIM ** -0.5 * math.log2(math.e), None), tm=tm_in)
    q, k, v, zg = unfold(qz[0]), unfold(kv[0]), unfold(kv[1]), unfold(qz[1])
    if cache is None:
        branch = _sb_prompt(q, k, v, zg, tq=min(SB_QUERY_TILE, t))
    else:
        branch = _sb_sample(q, k, v, cache[0], cache[1], zg)
    x1 = _outproj(fold(branch), sb_w_out, fold(x), rows(gate), None, tm=tm_out)

    shift, scale, gate = mods[1]
    nk, nv = GLA_HEADS * GLA_DK, GLA_HEADS * GLA_DV
    assert 2 * nk == d and nv == d
    qk, vr, dec = _inproj(x1, rows(scale), rows(shift), norm_g[1], gla_w_in, 1, (None, None),
                          tm=tm_in, decay=(gla_wa, gla_wa2, gla_b_a))
    chunk = min(GLA_CHUNK, t)
    branch, s_new = _gla(unfold(qk[0]), unfold(vr[0]), unfold(dec), unfold(vr[1]), gla_norm_g, state,
                         chunk=chunk, tt=min(gla_tt, t))
    y = _outproj(fold(branch), gla_w_out, x1, rows(gate), final_norm_g, tm=tm_out)
    return unfold(y), k, v, s_new


def kernel(x_prompt, x_sample, cache_sb_k, cache_sb_v, state_gla, c_prompt, c_sample, w_ada, b_ada, norm_g,
           sb_w_in, sb_w_out, gla_w_in, gla_w_a2, gla_b_a, gla_norm_g, gla_w_out, final_norm_g):
    depth, d, _ = w_ada.shape
    assert depth == 2 and sb_w_in.shape[0] == 1 and gla_w_in.shape[0] == 1
    bp, tp, _ = x_prompt.shape
    bs, ts, _ = x_sample.shape
    past = cache_sb_k.shape[2]
    nk, nv = GLA_HEADS * GLA_DK, GLA_HEADS * GLA_DV
    main = 2 * nk + 2 * nv

    mod = _ada(jnp.concatenate([c_prompt, c_sample], axis=0), w_ada, b_ada)
    split = lambda m: (m[:, :d], m[:, d:2 * d], m[:, 2 * d:])
    mods_p = [split(mod[l, :bp]) for l in range(depth)]
    mods_s = [split(mod[l, bp:]) for l in range(depth)]

    gla_w = gla_w_in[0]
    gla_wa = jnp.pad(gla_w[:, main:], ((0, 0), (0, V7X_LANES - GLA_GATE_RANK))).astype(BF16)
    gla_wa2 = jnp.pad(gla_w_a2[0], ((0, V7X_LANES - GLA_GATE_RANK), (0, 0))).astype(BF16)
    sb_w = sb_w_in[0]
    sb_w = jnp.concatenate([sb_w[:, d:3 * d], sb_w[:, :d], sb_w[:, 3 * d:]], axis=1).astype(BF16)
    weights = (norm_g, sb_w, sb_w_out[0].astype(BF16), gla_w[:, :main].astype(BF16),
               gla_wa, gla_wa2, gla_b_a[0], gla_norm_g[0], gla_w_out[0].astype(BF16), final_norm_g)

    y_p, k_p, v_p, s_p = _trunk(x_prompt, mods_p, weights, per_row=False,
                                tm_in=min(512, tp), tm_out=min(256, tp), gla_tt=512)
    cache = (cache_sb_k[0].reshape(bs, past, -1), cache_sb_v[0].reshape(bs, past, -1))
    y_s, k_s, v_s, s_s = _trunk(x_sample, mods_s, weights, per_row=True,
                                tm_in=bs * ts, tm_out=bs * ts, gla_tt=ts, cache=cache, state=state_gla[0])

    heads = lambda a: a.reshape(1, a.shape[0], a.shape[1], SB_HEADS, SB_HEAD_DIM)
    return (y_p, y_s, heads(k_p), heads(v_p), heads(k_s), heads(v_s), s_p[None], s_s[None])
```

```python
import functools
import math

import jax
import jax.numpy as jnp
from jax import lax
from jax.experimental import pallas as pl
from jax.experimental.pallas import tpu as pltpu

F32 = jnp.float32
BF16 = jnp.bfloat16

SB_HEADS = 16
SB_HEAD_DIM = 128
GLA_HEADS = 4
GLA_DK = 256
GLA_DV = 512
GLA_GATE_RANK = 16
GLA_TAU = 16.0
GLA_CHUNK = 64
EPS = 1e-6

V7X_LANES = 128
V7X_VMEM_LIMIT_BYTES = 56 * 1024 * 1024
SB_BLOCK = 256
SB_QUERY_TILE = 1024
SB_UNDERFLOW_LOG2 = 160.0
SB_FINISHED_RUN = 1e30
NORM_ROWS = 16
NORM_UNROLL = 4


def _params(semantics):
    return pltpu.CompilerParams(dimension_semantics=semantics,
                                vmem_limit_bytes=V7X_VMEM_LIMIT_BYTES)


def _split_bf16(x):
    hi = x.astype(BF16)
    lo = (x - hi.astype(F32)).astype(BF16)
    return hi, lo


def _log_sigmoid(z):
    return jnp.minimum(z, 0.0) - jnp.log(1.0 + jnp.exp(-jnp.abs(z)))


def _silu(z):
    return z / (1.0 + jnp.exp(-z))


def _ada_kernel(c_ref, w_ref, b_ref, o_ref):
    acc = jnp.dot(c_ref[...].astype(BF16), w_ref[0].astype(BF16), preferred_element_type=F32)
    o_ref[0] = acc + b_ref[0]


def _ada(c_all, w_ada, b_ada, tn=512):
    depth, d, n = w_ada.shape
    rows = c_all.shape[0]
    return pl.pallas_call(
        _ada_kernel,
        grid=(depth, n // tn),
        in_specs=[pl.BlockSpec((rows, d), lambda l, j: (0, 0)),
                  pl.BlockSpec((1, d, tn), lambda l, j: (l, 0, j)),
                  pl.BlockSpec((1, 1, tn), lambda l, j: (l, 0, j))],
        out_specs=pl.BlockSpec((1, rows, tn), lambda l, j: (l, 0, j)),
        out_shape=jax.ShapeDtypeStruct((depth, rows, n), F32),
        compiler_params=_params(("arbitrary", "arbitrary")),
        name="ada",
    )(c_all, w_ada, b_ada.reshape(depth, 1, n))


def _inproj_kernel(*refs, n_f32, bf16_mults, tm, with_decay):
    x_ref, sc_ref, sh_ref, g_ref, w_ref = refs[:5]
    pos = 5
    if with_decay:
        wa_ref, wa2_ref, ba_ref = refs[pos:pos + 3]
        pos += 3
    of_refs = refs[pos:pos + n_f32]
    ob_ref = refs[pos + n_f32]
    pos += n_f32 + 1
    if with_decay:
        dec_ref = refs[pos]
        pos += 1
    h_ref = refs[pos]
    n = pl.program_id(2)

    @pl.when(n == 0)
    def _():
        def norm_rows(r, carry):
            rows = pl.ds(pl.multiple_of(r * NORM_ROWS, NORM_ROWS), NORM_ROWS)
            x = x_ref[0, rows, :]
            y = x * lax.rsqrt(jnp.mean(x * x, axis=-1, keepdims=True) + EPS) * g_ref[...]
            if sc_ref.shape[1] == 1:
                sc, sh = sc_ref[0], sh_ref[0]
            else:
                sc, sh = sc_ref[0, rows, :], sh_ref[0, rows, :]
            h_ref[rows, :] = (y * (1.0 + sc) + sh).astype(BF16)
            return carry
        steps = tm // NORM_ROWS
        lax.fori_loop(0, steps, norm_rows, 0, unroll=NORM_UNROLL if steps % NORM_UNROLL == 0 else 1)

    for idx, of_ref in enumerate(of_refs):
        @pl.when(n == idx)
        def _(of_ref=of_ref):
            of_ref[0] = jnp.dot(h_ref[...], w_ref[...], preferred_element_type=F32)

    @pl.when(n >= n_f32)
    def _():
        acc = jnp.dot(h_ref[...], w_ref[...], preferred_element_type=F32)
        if any(m is not None for m in bf16_mults):
            mult = jnp.float32(1.0)
            for idx, m in enumerate(bf16_mults):
                if m is not None:
                    mult = jnp.where(n == n_f32 + idx, jnp.float32(m), mult)
            acc = acc * mult
        ob_ref[0, 0] = acc.astype(BF16)

    if with_decay:
        @pl.when(n == pl.num_programs(2) - 1)
        def _():
            a_lr = jnp.dot(h_ref[...], wa_ref[...], preferred_element_type=F32)
            pre = jnp.dot(a_lr.astype(BF16), wa2_ref[...], preferred_element_type=F32) + ba_ref[...]
            dec_ref[0] = _log_sigmoid(pre) * (1.0 / GLA_TAU)


def _inproj(x, scale, shift, g, w, n_f32, bf16_mults, *, tm, decay=None):
    b, t, d = x.shape
    n_seg = w.shape[1] // d
    n_bf16 = len(bf16_mults)
    assert t % tm == 0 and w.shape[1] == n_seg * d and n_seg == n_f32 + n_bf16
    assert tm % NORM_ROWS == 0
    mod_rows = scale.shape[1]
    mod_spec = (pl.BlockSpec((1, 1, d), lambda bi, i, j: (bi, 0, 0)) if mod_rows == 1
                else pl.BlockSpec((1, tm, d), lambda bi, i, j: (bi, i, 0)))
    in_specs = [pl.BlockSpec((1, tm, d), lambda bi, i, j: (bi, i, 0)), mod_spec, mod_spec,
                pl.BlockSpec((1, d), lambda bi, i, j: (0, 0)),
                pl.BlockSpec((d, d), lambda bi, i, j: (0, j))]
    args = [x, scale, shift, g.reshape(1, d), w]
    if decay is not None:
        wa, wa2, ba = decay
        in_specs += [pl.BlockSpec(wa.shape, lambda bi, i, j: (0, 0)),
                     pl.BlockSpec(wa2.shape, lambda bi, i, j: (0, 0)),
                     pl.BlockSpec((1, ba.shape[-1]), lambda bi, i, j: (0, 0))]
        args += [wa, wa2, ba.reshape(1, -1)]

    out_specs = [pl.BlockSpec((1, tm, d), lambda bi, i, j: (bi, i, 0)) for _ in range(n_f32)]
    out_specs.append(pl.BlockSpec((1, 1, tm, d), lambda bi, i, j: (jnp.maximum(j - n_f32, 0), bi, i, 0)))
    out_shape = [jax.ShapeDtypeStruct((b, t, d), F32) for _ in range(n_f32)]
    out_shape.append(jax.ShapeDtypeStruct((n_bf16, b, t, d), BF16))
    if decay is not None:
        nk = decay[1].shape[1]
        out_specs.append(pl.BlockSpec((1, tm, nk), lambda bi, i, j: (bi, i, 0)))
        out_shape.append(jax.ShapeDtypeStruct((b, t, nk), F32))
    return pl.pallas_call(
        functools.partial(_inproj_kernel, n_f32=n_f32, bf16_mults=tuple(bf16_mults), tm=tm,
                          with_decay=decay is not None),
        grid=(b, t // tm, n_seg),
        in_specs=in_specs, out_specs=out_specs, out_shape=out_shape,
        scratch_shapes=[pltpu.VMEM((tm, d), BF16)],
        compiler_params=_params(("arbitrary", "arbitrary", "arbitrary")),
        name="inproj",
    )(*args)


def _sb_block(q, k_blk, v_blk, w, run, mask=None):
    half = V7X_LANES
    z = lax.dot_general(q, k_blk, (((1,), (1,)), ((), ())), preferred_element_type=F32)
    sp = jnp.maximum(z, 0.0) + jnp.log2(1.0 + jnp.exp2(-jnp.abs(z)))
    if mask is not None:
        sp = jnp.where(mask, sp, 0.0)
    hi, lo = _split_bf16(sp)
    cs_r = jnp.dot(jnp.concatenate([hi[:, half:], lo[:, half:]], axis=1), w, preferred_element_type=F32)
    cs_l = jnp.dot(jnp.concatenate([hi[:, :half], lo[:, :half]], axis=1), w, preferred_element_type=F32)
    a_r = jnp.exp2(z[:, half:] - cs_r[:, :half] - run)
    run = run + cs_r[:, half:]
    a_l = jnp.exp2(z[:, :half] - cs_l[:, :half] - run)
    run = run + cs_l[:, half:]
    a = jnp.concatenate([a_l, a_r], axis=1)
    if mask is not None:
        a = jnp.where(mask, a, 0.0)
    return jnp.dot(a.astype(BF16), v_blk, preferred_element_type=F32), run


def _sb_consts():
    j = lax.broadcasted_iota(jnp.int32, (V7X_LANES, V7X_LANES), 0)
    s = lax.broadcasted_iota(jnp.int32, (V7X_LANES, V7X_LANES), 1)
    half = jnp.concatenate([(j >= s).astype(BF16), jnp.ones((V7X_LANES, V7X_LANES), BF16)], axis=1)
    return jnp.concatenate([half, half], axis=0)


def _causal_mask(rows):
    t_idx = lax.broadcasted_iota(jnp.int32, (rows, SB_BLOCK), 0)
    s_idx = lax.broadcasted_iota(jnp.int32, (rows, SB_BLOCK), 1)
    return s_idx < t_idx


def _sb_prompt_kernel(q_ref, k_ref, v_ref, zg_ref, w_ref, o_ref, acc_ref, run_ref, *, tq):
    i = pl.program_id(2)
    groups = tq // SB_BLOCK
    w = w_ref[...]

    def keys(first):
        rows = pl.ds(pl.multiple_of(first, SB_BLOCK), SB_BLOCK)
        return k_ref[0, rows, :].astype(BF16), v_ref[0, rows, :].astype(BF16)

    mask = _causal_mask(SB_BLOCK)
    for g in range(groups):
        rows = slice(g * SB_BLOCK, (g + 1) * SB_BLOCK)
        kb, vb = keys((i * groups + g) * SB_BLOCK)
        pv, run = _sb_block(q_ref[0, rows, :], kb, vb, w, jnp.zeros((SB_BLOCK, V7X_LANES), F32), mask=mask)
        acc_ref[rows, :] = pv
        run_ref[rows, :] = run

    def pending(d):
        low = jnp.float32(jnp.inf)
        for g in range(groups):
            rows = slice(g * SB_BLOCK, (g + 1) * SB_BLOCK)
            low = jnp.minimum(low, jnp.where(i * groups + g - d >= 0, jnp.min(run_ref[rows, :]), jnp.inf))
        return low

    def more(carry):
        return carry[1] < SB_UNDERFLOW_LOG2

    def diagonal(carry):
        d = carry[0]
        for g in range(groups):
            rows = slice(g * SB_BLOCK, (g + 1) * SB_BLOCK)
            blk = i * groups + g - d
            kb, vb = keys(jnp.maximum(blk, 0) * SB_BLOCK)
            run = jnp.where(blk >= 0, run_ref[rows, :], SB_FINISHED_RUN)
            pv, run = _sb_block(q_ref[0, rows, :], kb, vb, w, run)
            acc_ref[rows, :] += pv
            run_ref[rows, :] = run
        return d + 1, pending(d + 1)

    lax.while_loop(more, diagonal, (jnp.int32(1), pending(1)))
    o_ref[0] = (acc_ref[...] * _silu(zg_ref[0].astype(F32))).astype(o_ref.dtype)


def _sb_prompt(qz, k, v, tq):
    _, b, t, _ = qz.shape
    assert t % tq == 0 and tq % SB_BLOCK == 0
    tile = pl.BlockSpec((1, tq, SB_HEAD_DIM), lambda bi, h, i: (bi, i, h))
    stacked = lambda which: pl.BlockSpec((None, 1, tq, SB_HEAD_DIM), lambda bi, h, i: (which, bi, i, h))
    whole = pl.BlockSpec((1, t, SB_HEAD_DIM), lambda bi, h, i: (bi, 0, h))
    return pl.pallas_call(
        functools.partial(_sb_prompt_kernel, tq=tq),
        grid=(b, SB_HEADS, t // tq),
        in_specs=[stacked(0), whole, whole, stacked(1),
                  pl.BlockSpec((SB_BLOCK, SB_BLOCK), lambda bi, h, i: (0, 0))],
        out_specs=tile,
        out_shape=jax.ShapeDtypeStruct(qz.shape[1:], BF16),
        scratch_shapes=[pltpu.VMEM((tq, SB_HEAD_DIM), F32), pltpu.VMEM((tq, V7X_LANES), F32)],
        compiler_params=_params(("arbitrary", "arbitrary", "arbitrary")),
        name="sb_prompt",
    )(qz, k, v, qz, _sb_consts())


def _sb_sample_kernel(q_ref, kn_ref, vn_ref, kp_ref, vp_ref, zg_ref, w_ref, o_ref, kpad_ref, vpad_ref):
    t_new = q_ref.shape[1]
    q = q_ref[0]
    w = w_ref[...]
    kpad_ref[...] = jnp.zeros_like(kpad_ref)
    vpad_ref[...] = jnp.zeros_like(vpad_ref)
    kpad_ref[0:t_new, :] = kn_ref[0]
    vpad_ref[0:t_new, :] = vn_ref[0]
    acc, run = _sb_block(q, kpad_ref[...].astype(BF16), vpad_ref[...].astype(BF16), w,
                         jnp.zeros((t_new, V7X_LANES), F32), mask=_causal_mask(t_new))
    for j in range(kp_ref.shape[1] // SB_BLOCK - 1, -1, -1):
        rows = pl.ds(j * SB_BLOCK, SB_BLOCK)
        pv, run = _sb_block(q, kp_ref[0, rows, :].astype(BF16), vp_ref[0, rows, :].astype(BF16), w, run)
        acc = acc + pv
    o_ref[0] = (acc * _silu(zg_ref[0].astype(F32))).astype(o_ref.dtype)


def _sb_sample(qz, k_new, v_new, k_past, v_past):
    _, b, t, _ = qz.shape
    p = k_past.shape[1]
    assert p % SB_BLOCK == 0 and t <= SB_BLOCK
    new = pl.BlockSpec((1, t, SB_HEAD_DIM), lambda bi, h: (bi, 0, h))
    stacked = lambda which: pl.BlockSpec((None, 1, t, SB_HEAD_DIM), lambda bi, h: (which, bi, 0, h))
    past = pl.BlockSpec((1, p, SB_HEAD_DIM), lambda bi, h: (bi, 0, h))
    return pl.pallas_call(
        _sb_sample_kernel,
        grid=(b, SB_HEADS),
        in_specs=[stacked(0), new, new, past, past, stacked(1),
                  pl.BlockSpec((SB_BLOCK, SB_BLOCK), lambda bi, h: (0, 0))],
        out_specs=new,
        out_shape=jax.ShapeDtypeStruct(qz.shape[1:], BF16),
        scratch_shapes=[pltpu.VMEM((SB_BLOCK, SB_HEAD_DIM), F32), pltpu.VMEM((SB_BLOCK, SB_HEAD_DIM), F32)],
        compiler_params=_params(("arbitrary", "arbitrary")),
        name="sb_sample",
    )(qz, k_new, v_new, k_past, v_past, qz, _sb_consts())


def _gla_kernel(*refs, chunk, n_chunks, has_state):
    if has_state:
        q_ref, k_ref, v_ref, dec_ref, r_ref, gn_ref, tril_ref, s0_ref, o_ref, sf_ref, st_ref, upd_ref = refs
    else:
        q_ref, k_ref, v_ref, dec_ref, r_ref, gn_ref, tril_ref, o_ref, sf_ref, st_ref, upd_ref = refs
    step = pl.program_id(2)

    @pl.when(step == 0)
    def _():
        st_ref[...] = s0_ref[0, 0].T if has_state else jnp.zeros_like(st_ref)

    tril2 = tril_ref[...]
    causal = (lax.broadcasted_iota(jnp.int32, (chunk, chunk), 1)
              <= lax.broadcasted_iota(jnp.int32, (chunk, chunk), 0))
    q_scale = GLA_DK ** -0.5
    nt_dot = lambda a, b: lax.dot_general(a, b, (((1,), (1,)), ((), ())), preferred_element_type=F32)
    chunks = [pl.ds(c * chunk, chunk) for c in range(n_chunks)]

    q_in, k_out, k_end, decay = [], [], [], []
    for rows in chunks:
        g_hi, g_lo = _split_bf16(dec_ref[0, rows, :])
        b = jnp.dot(tril2, jnp.concatenate([g_hi, g_lo], axis=0), preferred_element_type=F32)
        b_last = b[chunk - 1:chunk, :]
        k = k_ref[0, rows, :]
        q_in.append((q_ref[0, rows, :] * q_scale * jnp.exp(b)).astype(BF16))
        k_out.append((k * jnp.exp(-b)).astype(BF16))
        k_end.append((k * jnp.exp(b_last - b)).astype(BF16))
        decay.append(jnp.exp(b_last))
    intra = []
    for c, rows in enumerate(chunks):
        v = v_ref[0, rows, :]
        a = jnp.where(causal, nt_dot(q_in[c], k_out[c]), 0.0).astype(BF16)
        intra.append(jnp.dot(a, v, preferred_element_type=F32))
        upd_ref[c] = lax.dot_general(v, k_end[c], (((0,), (0,)), ((), ())), preferred_element_type=F32)
    for c, rows in enumerate(chunks):
        state_t = st_ref[...]
        o = intra[c] + nt_dot(q_in[c], state_t.astype(BF16))
        st_ref[...] = state_t * decay[c] + upd_ref[c]
        o = o * lax.rsqrt(jnp.mean(o * o, axis=-1, keepdims=True) + EPS) * gn_ref[...]
        o_ref[0, rows, :] = (o * _silu(r_ref[0, rows, :].astype(F32))).astype(o_ref.dtype)

    @pl.when(step == pl.num_programs(2) - 1)
    def _():
        sf_ref[0, 0] = st_ref[...].T


def _gla(qk, vr, dec, gn, s0, *, chunk, tt):
    _, b, t, _ = vr.shape
    assert t % tt == 0 and tt % chunk == 0
    tril = (lax.broadcasted_iota(jnp.int32, (chunk, chunk), 1)
            <= lax.broadcasted_iota(jnp.int32, (chunk, chunk), 0)).astype(BF16)
    tril2 = jnp.concatenate([tril, tril], axis=1)
    n_chunks = tt // chunk
    kspec = lambda off: pl.BlockSpec((1, tt, GLA_DK), lambda bi, h, s: (bi, s, h + off))
    vspec = pl.BlockSpec((1, tt, GLA_DV), lambda bi, h, s: (bi, s, h))
    sspec = pl.BlockSpec((1, 1, GLA_DK, GLA_DV), lambda bi, h, s: (bi, h, 0, 0))
    stacked = lambda which: pl.BlockSpec((None, 1, tt, GLA_DV), lambda bi, h, s: (which, bi, s, h))
    in_specs = [kspec(0), kspec(GLA_HEADS), stacked(0), kspec(0), stacked(1),
                pl.BlockSpec((1, GLA_DV), lambda bi, h, s: (0, h)),
                pl.BlockSpec((chunk, 2 * chunk), lambda bi, h, s: (0, 0))]
    args = [qk, qk, vr, dec, vr, gn.reshape(1, -1), tril2]
    if s0 is not None:
        in_specs.append(sspec)
        args.append(s0)
    return pl.pallas_call(
        functools.partial(_gla_kernel, chunk=chunk, n_chunks=n_chunks, has_state=s0 is not None),
        grid=(b, GLA_HEADS, t // tt),
        in_specs=in_specs,
        out_specs=[vspec, sspec],
        out_shape=[jax.ShapeDtypeStruct(vr.shape[1:], BF16),
                   jax.ShapeDtypeStruct((b, GLA_HEADS, GLA_DK, GLA_DV), F32)],
        scratch_shapes=[pltpu.VMEM((GLA_DV, GLA_DK), F32), pltpu.VMEM((n_chunks, GLA_DV, GLA_DK), F32)],
        compiler_params=_params(("arbitrary", "arbitrary", "arbitrary")),
        name="gla",
    )(*args)


def _outproj_kernel(y_ref, w_ref, x_ref, gate_ref, *rest, final_norm):
    if final_norm:
        gf_ref, o_ref = rest
    else:
        (o_ref,) = rest
    y = jnp.dot(y_ref[0], w_ref[...], preferred_element_type=F32)
    x = x_ref[0] + gate_ref[0] * y
    if final_norm:
        x = x * lax.rsqrt(jnp.mean(x * x, axis=-1, keepdims=True) + EPS) * gf_ref[...]
    o_ref[0] = x


def _outproj(y, w, x, gate, gf, *, tm):
    b, t, d = x.shape
    kdim = y.shape[-1]
    assert t % tm == 0
    gate_spec = (pl.BlockSpec((1, 1, d), lambda bi, i: (bi, 0, 0)) if gate.shape[1] == 1
                 else pl.BlockSpec((1, tm, d), lambda bi, i: (bi, i, 0)))
    in_specs = [pl.BlockSpec((1, tm, kdim), lambda bi, i: (bi, i, 0)),
                pl.BlockSpec((kdim, d), lambda bi, i: (0, 0)),
                pl.BlockSpec((1, tm, d), lambda bi, i: (bi, i, 0)),
                gate_spec]
    args = [y, w, x, gate]
    if gf is not None:
        in_specs.append(pl.BlockSpec((1, d), lambda bi, i: (0, 0)))
        args.append(gf.reshape(1, d))
    return pl.pallas_call(
        functools.partial(_outproj_kernel, final_norm=gf is not None),
        grid=(b, t // tm),
        in_specs=in_specs,
        out_specs=pl.BlockSpec((1, tm, d), lambda bi, i: (bi, i, 0)),
        out_shape=jax.ShapeDtypeStruct(x.shape, F32),
        compiler_params=_params(("arbitrary", "arbitrary")),
        name="outproj",
    )(*args)


def _trunk(x, mods, weights, *, per_row, tm_in, tm_out, gla_tt, cache=None, state=None):
    (norm_g, sb_w_in, sb_w_out, gla_w_in, gla_wa, gla_wa2, gla_b_a, gla_norm_g, gla_w_out,
     final_norm_g) = weights
    b, t, d = x.shape
    if per_row:
        fold = lambda a: a.reshape(1, b * t, a.shape[-1])
        rows = lambda m: jnp.broadcast_to(m[:, None, :], (b, t, d)).reshape(1, b * t, d)
    else:
        fold = lambda a: a
        rows = lambda m: m[:, None, :]
    unfold = lambda a: a.reshape(b, t, a.shape[-1])
    width = SB_HEADS * SB_HEAD_DIM

    shift, scale, gate = mods[0]
    assert width == d
    k, v, qz = _inproj(fold(x), rows(scale), rows(shift), norm_g[0], sb_w_in, 2,
                       (SB_HEAD_DIM ** -0.5 * math.log2(math.e), None), tm=tm_in)
    k, v, qz = unfold(k), unfold(v), qz.reshape(2, b, t, d)
    if cache is None:
        branch = _sb_prompt(qz, k, v, tq=min(SB_QUERY_TILE, t))
    else:
        branch = _sb_sample(qz, k, v, cache[0], cache[1])
    x1 = _outproj(fold(branch), sb_w_out, fold(x), rows(gate), None, tm=tm_out)

    shift, scale, gate = mods[1]
    nk, nv = GLA_HEADS * GLA_DK, GLA_HEADS * GLA_DV
    assert 2 * nk == d and nv == d
    qk, vr, dec = _inproj(x1, rows(scale), rows(shift), norm_g[1], gla_w_in, 1, (None, None),
                          tm=tm_in, decay=(gla_wa, gla_wa2, gla_b_a))
    chunk = min(GLA_CHUNK, t)
    branch, s_new = _gla(unfold(qk), vr.reshape(2, b, t, d), unfold(dec), gla_norm_g, state,
                         chunk=chunk, tt=min(gla_tt, t))
    y = _outproj(fold(branch), gla_w_out, x1, rows(gate), final_norm_g, tm=tm_out)
    return unfold(y), k, v, s_new


def kernel(x_prompt, x_sample, cache_sb_k, cache_sb_v, state_gla, c_prompt, c_sample, w_ada, b_ada, norm_g,
           sb_w_in, sb_w_out, gla_w_in, gla_w_a2, gla_b_a, gla_norm_g, gla_w_out, final_norm_g):
    depth, d, _ = w_ada.shape
    assert depth == 2 and sb_w_in.shape[0] == 1 and gla_w_in.shape[0] == 1
    bp, tp, _ = x_prompt.shape
    bs, ts, _ = x_sample.shape
    past = cache_sb_k.shape[2]
    nk, nv = GLA_HEADS * GLA_DK, GLA_HEADS * GLA_DV
    main = 2 * nk + 2 * nv

    mod = _ada(jnp.concatenate([c_prompt, c_sample], axis=0), w_ada, b_ada)
    split = lambda m: (m[:, :d], m[:, d:2 * d], m[:, 2 * d:])
    mods_p = [split(mod[l, :bp]) for l in range(depth)]
    mods_s = [split(mod[l, bp:]) for l in range(depth)]

    gla_w = gla_w_in[0]
    gla_wa = jnp.pad(gla_w[:, main:], ((0, 0), (0, V7X_LANES - GLA_GATE_RANK))).astype(BF16)
    gla_wa2 = jnp.pad(gla_w_a2[0], ((0, V7X_LANES - GLA_GATE_RANK), (0, 0))).astype(BF16)
    sb_w = sb_w_in[0]
    sb_w = jnp.concatenate([sb_w[:, d:3 * d], sb_w[:, :d], sb_w[:, 3 * d:]], axis=1).astype(BF16)
    weights = (norm_g, sb_w, sb_w_out[0].astype(BF16), gla_w[:, :main].astype(BF16),
               gla_wa, gla_wa2, gla_b_a[0], gla_norm_g[0], gla_w_out[0].astype(BF16), final_norm_g)

    y_p, k_p, v_p, s_p = _trunk(x_prompt, mods_p, weights, per_row=False,
                                tm_in=min(512, tp), tm_out=min(256, tp), gla_tt=512)
    cache = (cache_sb_k[0].reshape(bs, past, -1), cache_sb_v[0].reshape(bs, past, -1))
    y_s, k_s, v_s, s_s = _trunk(x_sample, mods_s, weights, per_row=True,
                                tm_in=bs * ts, tm_out=bs * ts, gla_tt=ts, cache=cache, state=state_gla[0])

    heads = lambda a: a.reshape(1, a.shape[0], a.shape[1], SB_HEADS, SB_HEAD_DIM)
    return (y_p, y_s, heads(k_p), heads(v_p), heads(k_s), heads(v_s), s_p[None], s_s[None])
```

```python
import functools
import math

import jax
import jax.numpy as jnp
from jax import lax
from jax.experimental import pallas as pl
from jax.experimental.pallas import tpu as pltpu

F32 = jnp.float32
BF16 = jnp.bfloat16

SB_HEADS = 16
SB_HEAD_DIM = 128
GLA_HEADS = 4
GLA_DK = 256
GLA_DV = 512
GLA_GATE_RANK = 16
GLA_TAU = 16.0
GLA_CHUNK = 64
EPS = 1e-6

V7X_LANES = 128
V7X_VMEM_LIMIT_BYTES = 56 * 1024 * 1024
SB_BLOCK = 256
SB_QUERY_TILE = 1024
SB_UNDERFLOW_LOG2 = 160.0
SB_FINISHED_RUN = 1e30
NORM_ROWS = 16
NORM_UNROLL = 4


def _params(semantics):
    return pltpu.CompilerParams(dimension_semantics=semantics,
                                vmem_limit_bytes=V7X_VMEM_LIMIT_BYTES)


def _split_bf16(x):
    hi = x.astype(BF16)
    lo = (x - hi.astype(F32)).astype(BF16)
    return hi, lo


def _log_sigmoid(z):
    return jnp.minimum(z, 0.0) - jnp.log(1.0 + jnp.exp(-jnp.abs(z)))


def _silu(z):
    return z / (1.0 + jnp.exp(-z))


def _ada_kernel(c_ref, w_ref, b_ref, o_ref):
    acc = jnp.dot(c_ref[...].astype(BF16), w_ref[0].astype(BF16), preferred_element_type=F32)
    o_ref[0] = acc + b_ref[0]


def _ada(c_all, w_ada, b_ada, tn=512):
    depth, d, n = w_ada.shape
    rows = c_all.shape[0]
    return pl.pallas_call(
        _ada_kernel,
        grid=(depth, n // tn),
        in_specs=[pl.BlockSpec((rows, d), lambda l, j: (0, 0)),
                  pl.BlockSpec((1, d, tn), lambda l, j: (l, 0, j)),
                  pl.BlockSpec((1, 1, tn), lambda l, j: (l, 0, j))],
        out_specs=pl.BlockSpec((1, rows, tn), lambda l, j: (l, 0, j)),
        out_shape=jax.ShapeDtypeStruct((depth, rows, n), F32),
        compiler_params=_params(("arbitrary", "arbitrary")),
        name="ada",
    )(c_all, w_ada, b_ada.reshape(depth, 1, n))


def _inproj_kernel(*refs, n_f32, bf16_mults, tm, with_decay):
    x_ref, sc_ref, sh_ref, g_ref, w_ref = refs[:5]
    pos = 5
    if with_decay:
        wa_ref, wa2_ref, ba_ref = refs[pos:pos + 3]
        pos += 3
    of_refs = refs[pos:pos + n_f32]
    ob_ref = refs[pos + n_f32]
    pos += n_f32 + 1
    if with_decay:
        dec_ref = refs[pos]
        pos += 1
    h_ref = refs[pos]
    n = pl.program_id(2)

    @pl.when(n == 0)
    def _():
        def norm_rows(r, carry):
            rows = pl.ds(pl.multiple_of(r * NORM_ROWS, NORM_ROWS), NORM_ROWS)
            x = x_ref[0, rows, :]
            y = x * lax.rsqrt(jnp.mean(x * x, axis=-1, keepdims=True) + EPS) * g_ref[...]
            if sc_ref.shape[1] == 1:
                sc, sh = sc_ref[0], sh_ref[0]
            else:
                sc, sh = sc_ref[0, rows, :], sh_ref[0, rows, :]
            h_ref[rows, :] = (y * (1.0 + sc) + sh).astype(BF16)
            return carry
        steps = tm // NORM_ROWS
        lax.fori_loop(0, steps, norm_rows, 0, unroll=NORM_UNROLL if steps % NORM_UNROLL == 0 else 1)

    for idx, of_ref in enumerate(of_refs):
        @pl.when(n == idx)
        def _(of_ref=of_ref):
            of_ref[0] = jnp.dot(h_ref[...], w_ref[...], preferred_element_type=F32)

    @pl.when(n >= n_f32)
    def _():
        acc = jnp.dot(h_ref[...], w_ref[...], preferred_element_type=F32)
        if any(m is not None for m in bf16_mults):
            mult = jnp.float32(1.0)
            for idx, m in enumerate(bf16_mults):
                if m is not None:
                    mult = jnp.where(n == n_f32 + idx, jnp.float32(m), mult)
            acc = acc * mult
        ob_ref[0, 0] = acc.astype(BF16)

    if with_decay:
        @pl.when(n == pl.num_programs(2) - 1)
        def _():
            a_lr = jnp.dot(h_ref[...], wa_ref[...], preferred_element_type=F32)
            pre = jnp.dot(a_lr.astype(BF16), wa2_ref[...], preferred_element_type=F32) + ba_ref[...]
            dec_ref[0] = _log_sigmoid(pre) * (1.0 / GLA_TAU)


def _inproj(x, scale, shift, g, w, n_f32, bf16_mults, *, tm, decay=None):
    b, t, d = x.shape
    n_seg = w.shape[1] // d
    n_bf16 = len(bf16_mults)
    assert t % tm == 0 and w.shape[1] == n_seg * d and n_seg == n_f32 + n_bf16
    assert tm % NORM_ROWS == 0
    mod_rows = scale.shape[1]
    mod_spec = (pl.BlockSpec((1, 1, d), lambda bi, i, j: (bi, 0, 0)) if mod_rows == 1
                else pl.BlockSpec((1, tm, d), lambda bi, i, j: (bi, i, 0)))
    in_specs = [pl.BlockSpec((1, tm, d), lambda bi, i, j: (bi, i, 0)), mod_spec, mod_spec,
                pl.BlockSpec((1, d), lambda bi, i, j: (0, 0)),
                pl.BlockSpec((d, d), lambda bi, i, j: (0, j))]
    args = [x, scale, shift, g.reshape(1, d), w]
    if decay is not None:
        wa, wa2, ba = decay
        in_specs += [pl.BlockSpec(wa.shape, lambda bi, i, j: (0, 0)),
                     pl.BlockSpec(wa2.shape, lambda bi, i, j: (0, 0)),
                     pl.BlockSpec((1, ba.shape[-1]), lambda bi, i, j: (0, 0))]
        args += [wa, wa2, ba.reshape(1, -1)]

    out_specs = [pl.BlockSpec((1, tm, d), lambda bi, i, j: (bi, i, 0)) for _ in range(n_f32)]
    out_specs.append(pl.BlockSpec((1, 1, tm, d), lambda bi, i, j: (jnp.maximum(j - n_f32, 0), bi, i, 0)))
    out_shape = [jax.ShapeDtypeStruct((b, t, d), F32) for _ in range(n_f32)]
    out_shape.append(jax.ShapeDtypeStruct((n_bf16, b, t, d), BF16))
    if decay is not None:
        nk = decay[1].shape[1]
        out_specs.append(pl.BlockSpec((1, tm, nk), lambda bi, i, j: (bi, i, 0)))
        out_shape.append(jax.ShapeDtypeStruct((b, t, nk), F32))
    return pl.pallas_call(
        functools.partial(_inproj_kernel, n_f32=n_f32, bf16_mults=tuple(bf16_mults), tm=tm,
                          with_decay=decay is not None),
        grid=(b, t // tm, n_seg),
        in_specs=in_specs, out_specs=out_specs, out_shape=out_shape,
        scratch_shapes=[pltpu.VMEM((tm, d), BF16)],
        compiler_params=_params(("arbitrary", "arbitrary", "arbitrary")),
        name="inproj",
    )(*args)


def _sb_block(q, k_blk, v_blk, w, run, mask=None):
    half = V7X_LANES
    z = lax.dot_general(q, k_blk, (((1,), (1,)), ((), ())), preferred_element_type=F32)
    sp = jnp.maximum(z, 0.0) + jnp.log2(1.0 + jnp.exp2(-jnp.abs(z)))
    if mask is not None:
        sp = jnp.where(mask, sp, 0.0)
    hi, lo = _split_bf16(sp)
    cs_r = jnp.dot(jnp.concatenate([hi[:, half:], lo[:, half:]], axis=1), w, preferred_element_type=F32)
    cs_l = jnp.dot(jnp.concatenate([hi[:, :half], lo[:, :half]], axis=1), w, preferred_element_type=F32)
    a_r = jnp.exp2(z[:, half:] - cs_r[:, :half] - run)
    run = run + cs_r[:, half:]
    a_l = jnp.exp2(z[:, :half] - cs_l[:, :half] - run)
    run = run + cs_l[:, half:]
    a = jnp.concatenate([a_l, a_r], axis=1)
    if mask is not None:
        a = jnp.where(mask, a, 0.0)
    return jnp.dot(a.astype(BF16), v_blk, preferred_element_type=F32), run


def _sb_consts():
    j = lax.broadcasted_iota(jnp.int32, (V7X_LANES, V7X_LANES), 0)
    s = lax.broadcasted_iota(jnp.int32, (V7X_LANES, V7X_LANES), 1)
    half = jnp.concatenate([(j >= s).astype(BF16), jnp.ones((V7X_LANES, V7X_LANES), BF16)], axis=1)
    return jnp.concatenate([half, half], axis=0)


def _causal_mask(rows):
    t_idx = lax.broadcasted_iota(jnp.int32, (rows, SB_BLOCK), 0)
    s_idx = lax.broadcasted_iota(jnp.int32, (rows, SB_BLOCK), 1)
    return s_idx < t_idx


def _sb_prompt_kernel(q_ref, k_ref, v_ref, zg_ref, w_ref, o_ref, acc_ref, run_ref, *, tq):
    i = pl.program_id(2)
    groups = tq // SB_BLOCK
    w = w_ref[...]

    def keys(first):
        rows = pl.ds(pl.multiple_of(first, SB_BLOCK), SB_BLOCK)
        return k_ref[0, rows, :].astype(BF16), v_ref[0, rows, :].astype(BF16)

    mask = _causal_mask(SB_BLOCK)
    for g in range(groups):
        rows = slice(g * SB_BLOCK, (g + 1) * SB_BLOCK)
        kb, vb = keys((i * groups + g) * SB_BLOCK)
        pv, run = _sb_block(q_ref[0, rows, :], kb, vb, w, jnp.zeros((SB_BLOCK, V7X_LANES), F32), mask=mask)
        acc_ref[rows, :] = pv
        run_ref[rows, :] = run

    def pending(d):
        low = jnp.float32(jnp.inf)
        for g in range(groups):
            rows = slice(g * SB_BLOCK, (g + 1) * SB_BLOCK)
            low = jnp.minimum(low, jnp.where(i * groups + g - d >= 0, jnp.min(run_ref[rows, :]), jnp.inf))
        return low

    def more(carry):
        return carry[1] < SB_UNDERFLOW_LOG2

    def diagonal(carry):
        d = carry[0]
        for g in range(groups):
            rows = slice(g * SB_BLOCK, (g + 1) * SB_BLOCK)
            blk = i * groups + g - d
            kb, vb = keys(jnp.maximum(blk, 0) * SB_BLOCK)
            run = jnp.where(blk >= 0, run_ref[rows, :], SB_FINISHED_RUN)
            pv, run = _sb_block(q_ref[0, rows, :], kb, vb, w, run)
            acc_ref[rows, :] += pv
            run_ref[rows, :] = run
        return d + 1, pending(d + 1)

    lax.while_loop(more, diagonal, (jnp.int32(1), pending(1)))
    o_ref[0] = (acc_ref[...] * _silu(zg_ref[0].astype(F32))).astype(o_ref.dtype)


def _sb_prompt(qz, k, v, tq):
    _, b, t, _ = qz.shape
    assert t % tq == 0 and tq % SB_BLOCK == 0
    tile = pl.BlockSpec((1, tq, SB_HEAD_DIM), lambda bi, h, i: (bi, i, h))
    stacked = lambda which: pl.BlockSpec((None, 1, tq, SB_HEAD_DIM), lambda bi, h, i: (which, bi, i, h))
    whole = pl.BlockSpec((1, t, SB_HEAD_DIM), lambda bi, h, i: (bi, 0, h))
    return pl.pallas_call(
        functools.partial(_sb_prompt_kernel, tq=tq),
        grid=(b, SB_HEADS, t // tq),
        in_specs=[stacked(0), whole, whole, stacked(1),
                  pl.BlockSpec((SB_BLOCK, SB_BLOCK), lambda bi, h, i: (0, 0))],
        out_specs=tile,
        out_shape=jax.ShapeDtypeStruct(qz.shape[1:], BF16),
        scratch_shapes=[pltpu.VMEM((tq, SB_HEAD_DIM), F32), pltpu.VMEM((tq, V7X_LANES), F32)],
        compiler_params=_params(("arbitrary", "arbitrary", "arbitrary")),
        name="sb_prompt",
    )(qz, k, v, qz, _sb_consts())


def _sb_sample_kernel(q_ref, zg_ref, kn_ref, vn_ref, kc_hbm, vc_hbm, w_ref, o_ref,
                      kbuf, vbuf, sem, acc_ref, run_ref, *, n_blocks):
    bi = pl.program_id(0)
    t_new = q_ref.shape[1]
    block_rows = SB_BLOCK * SB_HEADS
    w = w_ref[...]

    def fetch(j, slot):
        src = pl.ds(j * block_rows, block_rows)
        return (pltpu.make_async_copy(kc_hbm.at[bi, src, :], kbuf.at[slot], sem.at[0, slot]),
                pltpu.make_async_copy(vc_hbm.at[bi, src, :], vbuf.at[slot], sem.at[1, slot]))

    for copy in fetch(n_blocks - 1, 0):
        copy.start()

    def head(h):
        return slice(h * t_new, (h + 1) * t_new), slice(h * SB_HEAD_DIM, (h + 1) * SB_HEAD_DIM)

    mask = _causal_mask(t_new)
    padding = jnp.zeros((SB_BLOCK - t_new, SB_HEAD_DIM), BF16)
    for h in range(SB_HEADS):
        rows, cols = head(h)
        kb = jnp.concatenate([kn_ref[0, :, cols].astype(BF16), padding], axis=0)
        vb = jnp.concatenate([vn_ref[0, :, cols].astype(BF16), padding], axis=0)
        pv, run = _sb_block(q_ref[0, :, cols], kb, vb, w, jnp.zeros((t_new, V7X_LANES), F32), mask=mask)
        acc_ref[rows, :] = pv
        run_ref[rows, :] = run

    def more(carry):
        return (carry[0] >= 0) & (carry[1] < SB_UNDERFLOW_LOG2)

    def past_block(carry):
        j = carry[0]
        slot = (n_blocks - 1 - j) % 2
        for copy in fetch(j, slot):
            copy.wait()

        @pl.when(j > 0)
        def _():
            for copy in fetch(j - 1, 1 - slot):
                copy.start()

        for h in range(SB_HEADS):
            rows, cols = head(h)
            keys = pl.ds(h, SB_BLOCK, stride=SB_HEADS)
            pv, run = _sb_block(q_ref[0, :, cols], kbuf[slot, keys, :].astype(BF16),
                                vbuf[slot, keys, :].astype(BF16), w, run_ref[rows, :])
            acc_ref[rows, :] += pv
            run_ref[rows, :] = run
        return j - 1, jnp.min(run_ref[...])

    j_end, _ = lax.while_loop(more, past_block, (jnp.int32(n_blocks - 1), jnp.min(run_ref[...])))

    @pl.when(j_end >= 0)
    def _():
        for copy in fetch(j_end, (n_blocks - 1 - j_end) % 2):
            copy.wait()

    for h in range(SB_HEADS):
        rows, cols = head(h)
        o_ref[0, :, cols] = (acc_ref[rows, :] * _silu(zg_ref[0, :, cols].astype(F32))).astype(o_ref.dtype)


def _sb_sample(qz, k_new, v_new, k_past, v_past):
    _, b, t, d = qz.shape
    n_blocks = k_past.shape[1] // (SB_BLOCK * SB_HEADS)
    assert k_past.shape[1] == n_blocks * SB_BLOCK * SB_HEADS and n_blocks >= 1 and t <= V7X_LANES
    new = pl.BlockSpec((1, t, d), lambda bi: (bi, 0, 0))
    stacked = lambda which: pl.BlockSpec((None, 1, t, d), lambda bi: (which, bi, 0, 0))
    return pl.pallas_call(
        functools.partial(_sb_sample_kernel, n_blocks=n_blocks),
        grid=(b,),
        in_specs=[stacked(0), stacked(1), new, new,
                  pl.BlockSpec(memory_space=pl.ANY), pl.BlockSpec(memory_space=pl.ANY),
                  pl.BlockSpec((SB_BLOCK, SB_BLOCK), lambda bi: (0, 0))],
        out_specs=new,
        out_shape=jax.ShapeDtypeStruct(qz.shape[1:], BF16),
        scratch_shapes=[pltpu.VMEM((2, SB_BLOCK * SB_HEADS, SB_HEAD_DIM), F32),
                        pltpu.VMEM((2, SB_BLOCK * SB_HEADS, SB_HEAD_DIM), F32),
                        pltpu.SemaphoreType.DMA((2, 2)),
                        pltpu.VMEM((SB_HEADS * t, SB_HEAD_DIM), F32),
                        pltpu.VMEM((SB_HEADS * t, V7X_LANES), F32)],
        compiler_params=_params(("arbitrary",)),
        name="sb_sample",
    )(qz, qz, k_new, v_new, k_past, v_past, _sb_consts())


def _gla_kernel(*refs, chunk, n_chunks, has_state):
    if has_state:
        q_ref, k_ref, v_ref, dec_ref, r_ref, gn_ref, tril_ref, s0_ref, o_ref, sf_ref, st_ref, upd_ref = refs
    else:
        q_ref, k_ref, v_ref, dec_ref, r_ref, gn_ref, tril_ref, o_ref, sf_ref, st_ref, upd_ref = refs
    step = pl.program_id(2)

    @pl.when(step == 0)
    def _():
        st_ref[...] = s0_ref[0, 0].T if has_state else jnp.zeros_like(st_ref)

    tril2 = tril_ref[...]
    causal = (lax.broadcasted_iota(jnp.int32, (chunk, chunk), 1)
              <= lax.broadcasted_iota(jnp.int32, (chunk, chunk), 0))
    q_scale = GLA_DK ** -0.5
    nt_dot = lambda a, b: lax.dot_general(a, b, (((1,), (1,)), ((), ())), preferred_element_type=F32)
    chunks = [pl.ds(c * chunk, chunk) for c in range(n_chunks)]

    q_in, k_out, k_end, decay = [], [], [], []
    for rows in chunks:
        g_hi, g_lo = _split_bf16(dec_ref[0, rows, :])
        b = jnp.dot(tril2, jnp.concatenate([g_hi, g_lo], axis=0), preferred_element_type=F32)
        b_last = b[chunk - 1:chunk, :]
        k = k_ref[0, rows, :]
        q_in.append((q_ref[0, rows, :] * q_scale * jnp.exp(b)).astype(BF16))
        k_out.append((k * jnp.exp(-b)).astype(BF16))
        k_end.append((k * jnp.exp(b_last - b)).astype(BF16))
        decay.append(jnp.exp(b_last))
    intra = []
    for c, rows in enumerate(chunks):
        v = v_ref[0, rows, :]
        a = jnp.where(causal, nt_dot(q_in[c], k_out[c]), 0.0).astype(BF16)
        intra.append(jnp.dot(a, v, preferred_element_type=F32))
        upd_ref[c] = lax.dot_general(v, k_end[c], (((0,), (0,)), ((), ())), preferred_element_type=F32)
    for c, rows in enumerate(chunks):
        state_t = st_ref[...]
        o = intra[c] + nt_dot(q_in[c], state_t.astype(BF16))
        st_ref[...] = state_t * decay[c] + upd_ref[c]
        o = o * lax.rsqrt(jnp.mean(o * o, axis=-1, keepdims=True) + EPS) * gn_ref[...]
        o_ref[0, rows, :] = (o * _silu(r_ref[0, rows, :].astype(F32))).astype(o_ref.dtype)

    @pl.when(step == pl.num_programs(2) - 1)
    def _():
        sf_ref[0, 0] = st_ref[...].T


def _gla(qk, vr, dec, gn, s0, *, chunk, tt):
    _, b, t, _ = vr.shape
    assert t % tt == 0 and tt % chunk == 0
    tril = (lax.broadcasted_iota(jnp.int32, (chunk, chunk), 1)
            <= lax.broadcasted_iota(jnp.int32, (chunk, chunk), 0)).astype(BF16)
    tril2 = jnp.concatenate([tril, tril], axis=1)
    n_chunks = tt // chunk
    kspec = lambda off: pl.BlockSpec((1, tt, GLA_DK), lambda bi, h, s: (bi, s, h + off))
    vspec = pl.BlockSpec((1, tt, GLA_DV), lambda bi, h, s: (bi, s, h))
    sspec = pl.BlockSpec((1, 1, GLA_DK, GLA_DV), lambda bi, h, s: (bi, h, 0, 0))
    stacked = lambda which: pl.BlockSpec((None, 1, tt, GLA_DV), lambda bi, h, s: (which, bi, s, h))
    in_specs = [kspec(0), kspec(GLA_HEADS), stacked(0), kspec(0), stacked(1),
                pl.BlockSpec((1, GLA_DV), lambda bi, h, s: (0, h)),
                pl.BlockSpec((chunk, 2 * chunk), lambda bi, h, s: (0, 0))]
    args = [qk, qk, vr, dec, vr, gn.reshape(1, -1), tril2]
    if s0 is not None:
        in_specs.append(sspec)
        args.append(s0)
    return pl.pallas_call(
        functools.partial(_gla_kernel, chunk=chunk, n_chunks=n_chunks, has_state=s0 is not None),
        grid=(b, GLA_HEADS, t // tt),
        in_specs=in_specs,
        out_specs=[vspec, sspec],
        out_shape=[jax.ShapeDtypeStruct(vr.shape[1:], BF16),
                   jax.ShapeDtypeStruct((b, GLA_HEADS, GLA_DK, GLA_DV), F32)],
        scratch_shapes=[pltpu.VMEM((GLA_DV, GLA_DK), F32), pltpu.VMEM((n_chunks, GLA_DV, GLA_DK), F32)],
        compiler_params=_params(("arbitrary", "arbitrary", "arbitrary")),
        name="gla",
    )(*args)


def _outproj_kernel(y_ref, w_ref, x_ref, gate_ref, *rest, final_norm):
    if final_norm:
        gf_ref, o_ref = rest
    else:
        (o_ref,) = rest
    y = jnp.dot(y_ref[0], w_ref[...], preferred_element_type=F32)
    x = x_ref[0] + gate_ref[0] * y
    if final_norm:
        x = x * lax.rsqrt(jnp.mean(x * x, axis=-1, keepdims=True) + EPS) * gf_ref[...]
    o_ref[0] = x


def _outproj(y, w, x, gate, gf, *, tm):
    b, t, d = x.shape
    kdim = y.shape[-1]
    assert t % tm == 0
    gate_spec = (pl.BlockSpec((1, 1, d), lambda bi, i: (bi, 0, 0)) if gate.shape[1] == 1
                 else pl.BlockSpec((1, tm, d), lambda bi, i: (bi, i, 0)))
    in_specs = [pl.BlockSpec((1, tm, kdim), lambda bi, i: (bi, i, 0)),
                pl.BlockSpec((kdim, d), lambda bi, i: (0, 0)),
                pl.BlockSpec((1, tm, d), lambda bi, i: (bi, i, 0)),
                gate_spec]
    args = [y, w, x, gate]
    if gf is not None:
        in_specs.append(pl.BlockSpec((1, d), lambda bi, i: (0, 0)))
        args.append(gf.reshape(1, d))
    return pl.pallas_call(
        functools.partial(_outproj_kernel, final_norm=gf is not None),
        grid=(b, t // tm),
        in_specs=in_specs,
        out_specs=pl.BlockSpec((1, tm, d), lambda bi, i: (bi, i, 0)),
        out_shape=jax.ShapeDtypeStruct(x.shape, F32),
        compiler_params=_params(("arbitrary", "arbitrary")),
        name="outproj",
    )(*args)


def _trunk(x, mods, weights, *, per_row, tm_in, tm_out, gla_tt, cache=None, state=None):
    (norm_g, sb_w_in, sb_w_out, gla_w_in, gla_wa, gla_wa2, gla_b_a, gla_norm_g, gla_w_out,
     final_norm_g) = weights
    b, t, d = x.shape
    if per_row:
        fold = lambda a: a.reshape(1, b * t, a.shape[-1])
        rows = lambda m: jnp.broadcast_to(m[:, None, :], (b, t, d)).reshape(1, b * t, d)
    else:
        fold = lambda a: a
        rows = lambda m: m[:, None, :]
    unfold = lambda a: a.reshape(b, t, a.shape[-1])
    width = SB_HEADS * SB_HEAD_DIM

    shift, scale, gate = mods[0]
    assert width == d
    k, v, qz = _inproj(fold(x), rows(scale), rows(shift), norm_g[0], sb_w_in, 2,
                       (SB_HEAD_DIM ** -0.5 * math.log2(math.e), None), tm=tm_in)
    k, v, qz = unfold(k), unfold(v), qz.reshape(2, b, t, d)
    if cache is None:
        branch = _sb_prompt(qz, k, v, tq=min(SB_QUERY_TILE, t))
    else:
        branch = _sb_sample(qz, k, v, cache[0], cache[1])
    x1 = _outproj(fold(branch), sb_w_out, fold(x), rows(gate), None, tm=tm_out)

    shift, scale, gate = mods[1]
    nk, nv = GLA_HEADS * GLA_DK, GLA_HEADS * GLA_DV
    assert 2 * nk == d and nv == d
    qk, vr, dec = _inproj(x1, rows(scale), rows(shift), norm_g[1], gla_w_in, 1, (None, None),
                          tm=tm_in, decay=(gla_wa, gla_wa2, gla_b_a))
    chunk = min(GLA_CHUNK, t)
    branch, s_new = _gla(unfold(qk), vr.reshape(2, b, t, d), unfold(dec), gla_norm_g, state,
                         chunk=chunk, tt=min(gla_tt, t))
    y = _outproj(fold(branch), gla_w_out, x1, rows(gate), final_norm_g, tm=tm_out)
    return unfold(y), k, v, s_new


def kernel(x_prompt, x_sample, cache_sb_k, cache_sb_v, state_gla, c_prompt, c_sample, w_ada, b_ada, norm_g,
           sb_w_in, sb_w_out, gla_w_in, gla_w_a2, gla_b_a, gla_norm_g, gla_w_out, final_norm_g):
    depth, d, _ = w_ada.shape
    assert depth == 2 and sb_w_in.shape[0] == 1 and gla_w_in.shape[0] == 1
    bp, tp, _ = x_prompt.shape
    bs, ts, _ = x_sample.shape
    past = cache_sb_k.shape[2]
    nk, nv = GLA_HEADS * GLA_DK, GLA_HEADS * GLA_DV
    main = 2 * nk + 2 * nv

    mod = _ada(jnp.concatenate([c_prompt, c_sample], axis=0), w_ada, b_ada)
    split = lambda m: (m[:, :d], m[:, d:2 * d], m[:, 2 * d:])
    mods_p = [split(mod[l, :bp]) for l in range(depth)]
    mods_s = [split(mod[l, bp:]) for l in range(depth)]

    gla_w = gla_w_in[0]
    gla_wa = jnp.pad(gla_w[:, main:], ((0, 0), (0, V7X_LANES - GLA_GATE_RANK))).astype(BF16)
    gla_wa2 = jnp.pad(gla_w_a2[0], ((0, V7X_LANES - GLA_GATE_RANK), (0, 0))).astype(BF16)
    sb_w = sb_w_in[0]
    sb_w = jnp.concatenate([sb_w[:, d:3 * d], sb_w[:, :d], sb_w[:, 3 * d:]], axis=1).astype(BF16)
    weights = (norm_g, sb_w, sb_w_out[0].astype(BF16), gla_w[:, :main].astype(BF16),
               gla_wa, gla_wa2, gla_b_a[0], gla_norm_g[0], gla_w_out[0].astype(BF16), final_norm_g)

    y_p, k_p, v_p, s_p = _trunk(x_prompt, mods_p, weights, per_row=False,
                                tm_in=min(512, tp), tm_out=min(512, tp), gla_tt=512)
    cache = (cache_sb_k[0].reshape(bs, past * SB_HEADS, SB_HEAD_DIM),
             cache_sb_v[0].reshape(bs, past * SB_HEADS, SB_HEAD_DIM))
    y_s, k_s, v_s, s_s = _trunk(x_sample, mods_s, weights, per_row=True,
                                tm_in=bs * ts, tm_out=bs * ts, gla_tt=ts, cache=cache, state=state_gla[0])

    heads = lambda a: a.reshape(1, a.shape[0], a.shape[1], SB_HEADS, SB_HEAD_DIM)
    return (y_p, y_s, heads(k_p), heads(v_p), heads(k_s), heads(v_s), s_p[None], s_s[None])
```

```python
import functools
import math

import jax
import jax.numpy as jnp
from jax import lax
from jax.experimental import pallas as pl
from jax.experimental.pallas import tpu as pltpu

F32 = jnp.float32
BF16 = jnp.bfloat16

SB_HEADS = 16
SB_HEAD_DIM = 128
GLA_HEADS = 4
GLA_DK = 256
GLA_DV = 512
GLA_GATE_RANK = 16
GLA_TAU = 16.0
GLA_CHUNK = 64
EPS = 1e-6

V7X_LANES = 128
V7X_VMEM_LIMIT_BYTES = 56 * 1024 * 1024
SB_BLOCK = 256
SB_QUERY_TILE = 2048
SB_UNDERFLOW_LOG2 = 160.0
SB_FINISHED_RUN = 1e30
NORM_ROWS = 16
NORM_UNROLL = 4


def _params(semantics):
    return pltpu.CompilerParams(dimension_semantics=semantics,
                                vmem_limit_bytes=V7X_VMEM_LIMIT_BYTES)


def _split_bf16(x):
    hi = x.astype(BF16)
    lo = (x - hi.astype(F32)).astype(BF16)
    return hi, lo


def _log_sigmoid(z):
    return jnp.minimum(z, 0.0) - jnp.log(1.0 + jnp.exp(-jnp.abs(z)))


def _silu(z):
    return z / (1.0 + jnp.exp(-z))


def _ada_kernel(c_ref, w_ref, b_ref, o_ref):
    acc = jnp.dot(c_ref[...].astype(BF16), w_ref[0].astype(BF16), preferred_element_type=F32)
    o_ref[0] = acc + b_ref[0]


def _ada(c_all, w_ada, b_ada, tn=512):
    depth, d, n = w_ada.shape
    rows = c_all.shape[0]
    return pl.pallas_call(
        _ada_kernel,
        grid=(depth, n // tn),
        in_specs=[pl.BlockSpec((rows, d), lambda l, j: (0, 0)),
                  pl.BlockSpec((1, d, tn), lambda l, j: (l, 0, j)),
                  pl.BlockSpec((1, 1, tn), lambda l, j: (l, 0, j))],
        out_specs=pl.BlockSpec((1, rows, tn), lambda l, j: (l, 0, j)),
        out_shape=jax.ShapeDtypeStruct((depth, rows, n), F32),
        compiler_params=_params(("arbitrary", "arbitrary")),
        name="ada",
    )(c_all, w_ada, b_ada.reshape(depth, 1, n))


def _inproj_kernel(*refs, n_f32, bf16_mults, tm, with_decay):
    x_ref, sc_ref, sh_ref, g_ref, w_ref = refs[:5]
    pos = 5
    if with_decay:
        wa_ref, wa2_ref, ba_ref = refs[pos:pos + 3]
        pos += 3
    of_refs = refs[pos:pos + n_f32]
    ob_ref = refs[pos + n_f32]
    pos += n_f32 + 1
    if with_decay:
        dec_ref = refs[pos]
        pos += 1
    h_ref = refs[pos]
    n = pl.program_id(2)

    @pl.when(n == 0)
    def _():
        def norm_rows(r, carry):
            rows = pl.ds(pl.multiple_of(r * NORM_ROWS, NORM_ROWS), NORM_ROWS)
            x = x_ref[0, rows, :]
            y = x * lax.rsqrt(jnp.mean(x * x, axis=-1, keepdims=True) + EPS) * g_ref[...]
            if sc_ref.shape[1] == 1:
                sc, sh = sc_ref[0], sh_ref[0]
            else:
                sc, sh = sc_ref[0, rows, :], sh_ref[0, rows, :]
            h_ref[rows, :] = (y * (1.0 + sc) + sh).astype(BF16)
            return carry
        steps = tm // NORM_ROWS
        lax.fori_loop(0, steps, norm_rows, 0, unroll=NORM_UNROLL if steps % NORM_UNROLL == 0 else 1)

    for idx, of_ref in enumerate(of_refs):
        @pl.when(n == idx)
        def _(of_ref=of_ref):
            of_ref[0] = jnp.dot(h_ref[...], w_ref[...], preferred_element_type=F32)

    @pl.when(n >= n_f32)
    def _():
        acc = jnp.dot(h_ref[...], w_ref[...], preferred_element_type=F32)
        if any(m is not None for m in bf16_mults):
            mult = jnp.float32(1.0)
            for idx, m in enumerate(bf16_mults):
                if m is not None:
                    mult = jnp.where(n == n_f32 + idx, jnp.float32(m), mult)
            acc = acc * mult
        ob_ref[0, 0] = acc.astype(BF16)

    if with_decay:
        @pl.when(n == pl.num_programs(2) - 1)
        def _():
            a_lr = jnp.dot(h_ref[...], wa_ref[...], preferred_element_type=F32)
            pre = jnp.dot(a_lr.astype(BF16), wa2_ref[...], preferred_element_type=F32) + ba_ref[...]
            dec_ref[0] = _log_sigmoid(pre) * (1.0 / GLA_TAU)


def _inproj(x, scale, shift, g, w, n_f32, bf16_mults, *, tm, decay=None):
    b, t, d = x.shape
    n_seg = w.shape[1] // d
    n_bf16 = len(bf16_mults)
    assert t % tm == 0 and w.shape[1] == n_seg * d and n_seg == n_f32 + n_bf16
    assert tm % NORM_ROWS == 0
    mod_rows = scale.shape[1]
    mod_spec = (pl.BlockSpec((1, 1, d), lambda bi, i, j: (bi, 0, 0)) if mod_rows == 1
                else pl.BlockSpec((1, tm, d), lambda bi, i, j: (bi, i, 0)))
    in_specs = [pl.BlockSpec((1, tm, d), lambda bi, i, j: (bi, i, 0)), mod_spec, mod_spec,
                pl.BlockSpec((1, d), lambda bi, i, j: (0, 0)),
                pl.BlockSpec((d, d), lambda bi, i, j: (0, j))]
    args = [x, scale, shift, g.reshape(1, d), w]
    if decay is not None:
        wa, wa2, ba = decay
        in_specs += [pl.BlockSpec(wa.shape, lambda bi, i, j: (0, 0)),
                     pl.BlockSpec(wa2.shape, lambda bi, i, j: (0, 0)),
                     pl.BlockSpec((1, ba.shape[-1]), lambda bi, i, j: (0, 0))]
        args += [wa, wa2, ba.reshape(1, -1)]

    out_specs = [pl.BlockSpec((1, tm, d), lambda bi, i, j: (bi, i, 0)) for _ in range(n_f32)]
    out_specs.append(pl.BlockSpec((1, 1, tm, d), lambda bi, i, j: (jnp.maximum(j - n_f32, 0), bi, i, 0)))
    out_shape = [jax.ShapeDtypeStruct((b, t, d), F32) for _ in range(n_f32)]
    out_shape.append(jax.ShapeDtypeStruct((n_bf16, b, t, d), BF16))
    if decay is not None:
        nk = decay[1].shape[1]
        out_specs.append(pl.BlockSpec((1, tm, nk), lambda bi, i, j: (bi, i, 0)))
        out_shape.append(jax.ShapeDtypeStruct((b, t, nk), F32))
    return pl.pallas_call(
        functools.partial(_inproj_kernel, n_f32=n_f32, bf16_mults=tuple(bf16_mults), tm=tm,
                          with_decay=decay is not None),
        grid=(b, t // tm, n_seg),
        in_specs=in_specs, out_specs=out_specs, out_shape=out_shape,
        scratch_shapes=[pltpu.VMEM((tm, d), BF16)],
        compiler_params=_params(("arbitrary", "arbitrary", "arbitrary")),
        name="inproj",
    )(*args)


def _sb_blocks(qs, k_blks, v_blks, w, runs, mask=None):
    half = V7X_LANES
    zs = [lax.dot_general(q, k_blk, (((1,), (1,)), ((), ())), preferred_element_type=F32)
          for q, k_blk in zip(qs, k_blks)]
    sums = []
    for z in zs:
        sp = jnp.maximum(z, 0.0) + jnp.log2(1.0 + jnp.exp2(-jnp.abs(z)))
        if mask is not None:
            sp = jnp.where(mask, sp, 0.0)
        hi, lo = _split_bf16(sp)
        sums.append((jnp.dot(jnp.concatenate([hi[:, half:], lo[:, half:]], axis=1), w,
                             preferred_element_type=F32),
                     jnp.dot(jnp.concatenate([hi[:, :half], lo[:, :half]], axis=1), w,
                             preferred_element_type=F32)))
    outs = []
    for z, (cs_r, cs_l), v_blk, run in zip(zs, sums, v_blks, runs):
        a_r = jnp.exp2(z[:, half:] - cs_r[:, :half] - run)
        run = run + cs_r[:, half:]
        a_l = jnp.exp2(z[:, :half] - cs_l[:, :half] - run)
        run = run + cs_l[:, half:]
        a = jnp.concatenate([a_l, a_r], axis=1)
        if mask is not None:
            a = jnp.where(mask, a, 0.0)
        outs.append((jnp.dot(a.astype(BF16), v_blk, preferred_element_type=F32), run))
    return outs


def _sb_consts():
    j = lax.broadcasted_iota(jnp.int32, (V7X_LANES, V7X_LANES), 0)
    s = lax.broadcasted_iota(jnp.int32, (V7X_LANES, V7X_LANES), 1)
    half = jnp.concatenate([(j >= s).astype(BF16), jnp.ones((V7X_LANES, V7X_LANES), BF16)], axis=1)
    return jnp.concatenate([half, half], axis=0)


def _causal_mask(rows):
    t_idx = lax.broadcasted_iota(jnp.int32, (rows, SB_BLOCK), 0)
    s_idx = lax.broadcasted_iota(jnp.int32, (rows, SB_BLOCK), 1)
    return s_idx < t_idx


def _sb_prompt_kernel(q_ref, k_ref, v_ref, zg_ref, w_ref, o_ref, acc_ref, run_ref, *, tq):
    i = pl.program_id(2)
    groups = tq // SB_BLOCK
    w = w_ref[...]

    def keys(first):
        rows = pl.ds(pl.multiple_of(first, SB_BLOCK), SB_BLOCK)
        return k_ref[0, rows, :].astype(BF16), v_ref[0, rows, :].astype(BF16)

    group_rows = [slice(g * SB_BLOCK, (g + 1) * SB_BLOCK) for g in range(groups)]
    queries = lambda: [q_ref[0, rows, :] for rows in group_rows]
    kvs = [keys((i * groups + g) * SB_BLOCK) for g in range(groups)]
    outs = _sb_blocks(queries(), [kv[0] for kv in kvs], [kv[1] for kv in kvs], w,
                      [jnp.zeros((SB_BLOCK, V7X_LANES), F32)] * groups, mask=_causal_mask(SB_BLOCK))
    for rows, (pv, run) in zip(group_rows, outs):
        acc_ref[rows, :] = pv
        run_ref[rows, :] = run

    def pending(d):
        low = jnp.float32(jnp.inf)
        for g in range(groups):
            rows = slice(g * SB_BLOCK, (g + 1) * SB_BLOCK)
            low = jnp.minimum(low, jnp.where(i * groups + g - d >= 0, jnp.min(run_ref[rows, :]), jnp.inf))
        return low

    def more(carry):
        return carry[1] < SB_UNDERFLOW_LOG2

    def diagonal(carry):
        d = carry[0]
        blks = [i * groups + g - d for g in range(groups)]
        kvs = [keys(jnp.maximum(blk, 0) * SB_BLOCK) for blk in blks]
        runs = [jnp.where(blk >= 0, run_ref[rows, :], SB_FINISHED_RUN) for blk, rows in zip(blks, group_rows)]
        outs = _sb_blocks(queries(), [kv[0] for kv in kvs], [kv[1] for kv in kvs], w, runs)
        for rows, (pv, run) in zip(group_rows, outs):
            acc_ref[rows, :] += pv
            run_ref[rows, :] = run
        return d + 1, pending(d + 1)

    lax.while_loop(more, diagonal, (jnp.int32(1), pending(1)))
    o_ref[0] = (acc_ref[...] * _silu(zg_ref[0].astype(F32))).astype(o_ref.dtype)


def _sb_prompt(qz, k, v, tq):
    _, b, t, _ = qz.shape
    assert t % tq == 0 and tq % SB_BLOCK == 0
    tile = pl.BlockSpec((1, tq, SB_HEAD_DIM), lambda bi, h, i: (bi, i, h))
    stacked = lambda which: pl.BlockSpec((None, 1, tq, SB_HEAD_DIM), lambda bi, h, i: (which, bi, i, h))
    whole = pl.BlockSpec((1, t, SB_HEAD_DIM), lambda bi, h, i: (bi, 0, h))
    return pl.pallas_call(
        functools.partial(_sb_prompt_kernel, tq=tq),
        grid=(b, SB_HEADS, t // tq),
        in_specs=[stacked(0), whole, whole, stacked(1),
                  pl.BlockSpec((SB_BLOCK, SB_BLOCK), lambda bi, h, i: (0, 0))],
        out_specs=tile,
        out_shape=jax.ShapeDtypeStruct(qz.shape[1:], BF16),
        scratch_shapes=[pltpu.VMEM((tq, SB_HEAD_DIM), F32), pltpu.VMEM((tq, V7X_LANES), F32)],
        compiler_params=_params(("arbitrary", "arbitrary", "arbitrary")),
        name="sb_prompt",
    )(qz, k, v, qz, _sb_consts())


def _sb_sample_kernel(q_ref, zg_ref, kn_ref, vn_ref, kc_hbm, vc_hbm, w_ref, o_ref,
                      kbuf, vbuf, sem, acc_ref, run_ref, *, n_blocks):
    bi = pl.program_id(0)
    t_new = q_ref.shape[1]
    block_rows = SB_BLOCK * SB_HEADS
    w = w_ref[...]

    def fetch(j, slot):
        src = pl.ds(j * block_rows, block_rows)
        return (pltpu.make_async_copy(kc_hbm.at[bi, src, :], kbuf.at[slot], sem.at[0, slot]),
                pltpu.make_async_copy(vc_hbm.at[bi, src, :], vbuf.at[slot], sem.at[1, slot]))

    for copy in fetch(n_blocks - 1, 0):
        copy.start()

    def head(h):
        return slice(h * t_new, (h + 1) * t_new), slice(h * SB_HEAD_DIM, (h + 1) * SB_HEAD_DIM)

    mask = _causal_mask(t_new)
    padding = jnp.zeros((SB_BLOCK - t_new, SB_HEAD_DIM), BF16)
    heads = [head(h) for h in range(SB_HEADS)]
    padded = lambda ref, cols: jnp.concatenate([ref[0, :, cols].astype(BF16), padding], axis=0)
    outs = _sb_blocks([q_ref[0, :, cols] for _, cols in heads], [padded(kn_ref, cols) for _, cols in heads],
                      [padded(vn_ref, cols) for _, cols in heads], w,
                      [jnp.zeros((t_new, V7X_LANES), F32)] * SB_HEADS, mask=mask)
    for (rows, _), (pv, run) in zip(heads, outs):
        acc_ref[rows, :] = pv
        run_ref[rows, :] = run

    def more(carry):
        return (carry[0] >= 0) & (carry[1] < SB_UNDERFLOW_LOG2)

    def past_block(carry):
        j = carry[0]
        slot = (n_blocks - 1 - j) % 2
        for copy in fetch(j, slot):
            copy.wait()

        @pl.when(j > 0)
        def _():
            for copy in fetch(j - 1, 1 - slot):
                copy.start()

        keys = [pl.ds(h, SB_BLOCK, stride=SB_HEADS) for h in range(SB_HEADS)]
        outs = _sb_blocks([q_ref[0, :, cols] for _, cols in heads],
                          [kbuf[slot, rows, :].astype(BF16) for rows in keys],
                          [vbuf[slot, rows, :].astype(BF16) for rows in keys], w,
                          [run_ref[rows, :] for rows, _ in heads])
        for (rows, _), (pv, run) in zip(heads, outs):
            acc_ref[rows, :] += pv
            run_ref[rows, :] = run
        return j - 1, jnp.min(run_ref[...])

    j_end, _ = lax.while_loop(more, past_block, (jnp.int32(n_blocks - 1), jnp.min(run_ref[...])))

    @pl.when(j_end >= 0)
    def _():
        for copy in fetch(j_end, (n_blocks - 1 - j_end) % 2):
            copy.wait()

    for h in range(SB_HEADS):
        rows, cols = head(h)
        o_ref[0, :, cols] = (acc_ref[rows, :] * _silu(zg_ref[0, :, cols].astype(F32))).astype(o_ref.dtype)


def _sb_sample(qz, k_new, v_new, k_past, v_past):
    _, b, t, d = qz.shape
    n_blocks = k_past.shape[1] // (SB_BLOCK * SB_HEADS)
    assert k_past.shape[1] == n_blocks * SB_BLOCK * SB_HEADS and n_blocks >= 1 and t <= V7X_LANES
    new = pl.BlockSpec((1, t, d), lambda bi: (bi, 0, 0))
    stacked = lambda which: pl.BlockSpec((None, 1, t, d), lambda bi: (which, bi, 0, 0))
    return pl.pallas_call(
        functools.partial(_sb_sample_kernel, n_blocks=n_blocks),
        grid=(b,),
        in_specs=[stacked(0), stacked(1), new, new,
                  pl.BlockSpec(memory_space=pl.ANY), pl.BlockSpec(memory_space=pl.ANY),
                  pl.BlockSpec((SB_BLOCK, SB_BLOCK), lambda bi: (0, 0))],
        out_specs=new,
        out_shape=jax.ShapeDtypeStruct(qz.shape[1:], BF16),
        scratch_shapes=[pltpu.VMEM((2, SB_BLOCK * SB_HEADS, SB_HEAD_DIM), F32),
                        pltpu.VMEM((2, SB_BLOCK * SB_HEADS, SB_HEAD_DIM), F32),
                        pltpu.SemaphoreType.DMA((2, 2)),
                        pltpu.VMEM((SB_HEADS * t, SB_HEAD_DIM), F32),
                        pltpu.VMEM((SB_HEADS * t, V7X_LANES), F32)],
        compiler_params=_params(("arbitrary",)),
        name="sb_sample",
    )(qz, qz, k_new, v_new, k_past, v_past, _sb_consts())


def _gla_kernel(*refs, chunk, n_chunks, has_state):
    if has_state:
        q_ref, k_ref, v_ref, dec_ref, r_ref, gn_ref, tril_ref, s0_ref, o_ref, sf_ref, st_ref, upd_ref = refs
    else:
        q_ref, k_ref, v_ref, dec_ref, r_ref, gn_ref, tril_ref, o_ref, sf_ref, st_ref, upd_ref = refs
    step = pl.program_id(2)

    @pl.when(step == 0)
    def _():
        st_ref[...] = s0_ref[0, 0].T if has_state else jnp.zeros_like(st_ref)

    tril2 = tril_ref[...]
    causal = (lax.broadcasted_iota(jnp.int32, (chunk, chunk), 1)
              <= lax.broadcasted_iota(jnp.int32, (chunk, chunk), 0))
    q_scale = GLA_DK ** -0.5
    nt_dot = lambda a, b: lax.dot_general(a, b, (((1,), (1,)), ((), ())), preferred_element_type=F32)
    chunks = [pl.ds(c * chunk, chunk) for c in range(n_chunks)]

    q_in, k_out, k_end, decay = [], [], [], []
    for rows in chunks:
        g_hi, g_lo = _split_bf16(dec_ref[0, rows, :])
        b = jnp.dot(tril2, jnp.concatenate([g_hi, g_lo], axis=0), preferred_element_type=F32)
        b_last = b[chunk - 1:chunk, :]
        k = k_ref[0, rows, :]
        q_in.append((q_ref[0, rows, :] * q_scale * jnp.exp(b)).astype(BF16))
        k_out.append((k * jnp.exp(-b)).astype(BF16))
        k_end.append((k * jnp.exp(b_last - b)).astype(BF16))
        decay.append(jnp.exp(b_last))
    intra = []
    for c, rows in enumerate(chunks):
        v = v_ref[0, rows, :]
        a = jnp.where(causal, nt_dot(q_in[c], k_out[c]), 0.0).astype(BF16)
        intra.append(jnp.dot(a, v, preferred_element_type=F32))
        upd_ref[c] = lax.dot_general(v, k_end[c], (((0,), (0,)), ((), ())), preferred_element_type=F32)
    for c, rows in enumerate(chunks):
        state_t = st_ref[...]
        o = intra[c] + nt_dot(q_in[c], state_t.astype(BF16))
        st_ref[...] = state_t * decay[c] + upd_ref[c]
        o = o * lax.rsqrt(jnp.mean(o * o, axis=-1, keepdims=True) + EPS) * gn_ref[...]
        o_ref[0, rows, :] = (o * _silu(r_ref[0, rows, :].astype(F32))).astype(o_ref.dtype)

    @pl.when(step == pl.num_programs(2) - 1)
    def _():
        sf_ref[0, 0] = st_ref[...].T


def _gla(qk, vr, dec, gn, s0, *, chunk, tt):
    _, b, t, _ = vr.shape
    assert t % tt == 0 and tt % chunk == 0
    tril = (lax.broadcasted_iota(jnp.int32, (chunk, chunk), 1)
            <= lax.broadcasted_iota(jnp.int32, (chunk, chunk), 0)).astype(BF16)
    tril2 = jnp.concatenate([tril, tril], axis=1)
    n_chunks = tt // chunk
    kspec = lambda off: pl.BlockSpec((1, tt, GLA_DK), lambda bi, h, s: (bi, s, h + off))
    vspec = pl.BlockSpec((1, tt, GLA_DV), lambda bi, h, s: (bi, s, h))
    sspec = pl.BlockSpec((1, 1, GLA_DK, GLA_DV), lambda bi, h, s: (bi, h, 0, 0))
    stacked = lambda which: pl.BlockSpec((None, 1, tt, GLA_DV), lambda bi, h, s: (which, bi, s, h))
    in_specs = [kspec(0), kspec(GLA_HEADS), stacked(0), kspec(0), stacked(1),
                pl.BlockSpec((1, GLA_DV), lambda bi, h, s: (0, h)),
                pl.BlockSpec((chunk, 2 * chunk), lambda bi, h, s: (0, 0))]
    args = [qk, qk, vr, dec, vr, gn.reshape(1, -1), tril2]
    if s0 is not None:
        in_specs.append(sspec)
        args.append(s0)
    return pl.pallas_call(
        functools.partial(_gla_kernel, chunk=chunk, n_chunks=n_chunks, has_state=s0 is not None),
        grid=(b, GLA_HEADS, t // tt),
        in_specs=in_specs,
        out_specs=[vspec, sspec],
        out_shape=[jax.ShapeDtypeStruct(vr.shape[1:], BF16),
                   jax.ShapeDtypeStruct((b, GLA_HEADS, GLA_DK, GLA_DV), F32)],
        scratch_shapes=[pltpu.VMEM((GLA_DV, GLA_DK), F32), pltpu.VMEM((n_chunks, GLA_DV, GLA_DK), F32)],
        compiler_params=_params(("arbitrary", "arbitrary", "arbitrary")),
        name="gla",
    )(*args)


def _outproj_kernel(y_ref, w_ref, x_ref, gate_ref, *rest, final_norm):
    if final_norm:
        gf_ref, o_ref = rest
    else:
        (o_ref,) = rest
    y = jnp.dot(y_ref[0], w_ref[...], preferred_element_type=F32)
    x = x_ref[0] + gate_ref[0] * y
    if final_norm:
        x = x * lax.rsqrt(jnp.mean(x * x, axis=-1, keepdims=True) + EPS) * gf_ref[...]
    o_ref[0] = x


def _outproj(y, w, x, gate, gf, *, tm):
    b, t, d = x.shape
    kdim = y.shape[-1]
    assert t % tm == 0
    gate_spec = (pl.BlockSpec((1, 1, d), lambda bi, i: (bi, 0, 0)) if gate.shape[1] == 1
                 else pl.BlockSpec((1, tm, d), lambda bi, i: (bi, i, 0)))
    in_specs = [pl.BlockSpec((1, tm, kdim), lambda bi, i: (bi, i, 0)),
                pl.BlockSpec((kdim, d), lambda bi, i: (0, 0)),
                pl.BlockSpec((1, tm, d), lambda bi, i: (bi, i, 0)),
                gate_spec]
    args = [y, w, x, gate]
    if gf is not None:
        in_specs.append(pl.BlockSpec((1, d), lambda bi, i: (0, 0)))
        args.append(gf.reshape(1, d))
    return pl.pallas_call(
        functools.partial(_outproj_kernel, final_norm=gf is not None),
        grid=(b, t // tm),
        in_specs=in_specs,
        out_specs=pl.BlockSpec((1, tm, d), lambda bi, i: (bi, i, 0)),
        out_shape=jax.ShapeDtypeStruct(x.shape, F32),
        compiler_params=_params(("arbitrary", "arbitrary")),
        name="outproj",
    )(*args)


def _trunk(x, mods, weights, *, per_row, tm_in, tm_out, gla_tt, cache=None, state=None):
    (norm_g, sb_w_in, sb_w_out, gla_w_in, gla_wa, gla_wa2, gla_b_a, gla_norm_g, gla_w_out,
     final_norm_g) = weights
    b, t, d = x.shape
    if per_row:
        fold = lambda a: a.reshape(1, b * t, a.shape[-1])
        rows = lambda m: jnp.broadcast_to(m[:, None, :], (b, t, d)).reshape(1, b * t, d)
    else:
        fold = lambda a: a
        rows = lambda m: m[:, None, :]
    unfold = lambda a: a.reshape(b, t, a.shape[-1])
    width = SB_HEADS * SB_HEAD_DIM

    shift, scale, gate = mods[0]
    assert width == d
    k, v, qz = _inproj(fold(x), rows(scale), rows(shift), norm_g[0], sb_w_in, 2,
                       (SB_HEAD_DIM ** -0.5 * math.log2(math.e), None), tm=tm_in)
    k, v, qz = unfold(k), unfold(v), qz.reshape(2, b, t, d)
    if cache is None:
        branch = _sb_prompt(qz, k, v, tq=min(SB_QUERY_TILE, t))
    else:
        branch = _sb_sample(qz, k, v, cache[0], cache[1])
    x1 = _outproj(fold(branch), sb_w_out, fold(x), rows(gate), None, tm=tm_out)

    shift, scale, gate = mods[1]
    nk, nv = GLA_HEADS * GLA_DK, GLA_HEADS * GLA_DV
    assert 2 * nk == d and nv == d
    qk, vr, dec = _inproj(x1, rows(scale), rows(shift), norm_g[1], gla_w_in, 1, (None, None),
                          tm=tm_in, decay=(gla_wa, gla_wa2, gla_b_a))
    chunk = min(GLA_CHUNK, t)
    branch, s_new = _gla(unfold(qk), vr.reshape(2, b, t, d), unfold(dec), gla_norm_g, state,
                         chunk=chunk, tt=min(gla_tt, t))
    y = _outproj(fold(branch), gla_w_out, x1, rows(gate), final_norm_g, tm=tm_out)
    return unfold(y), k, v, s_new


def kernel(x_prompt, x_sample, cache_sb_k, cache_sb_v, state_gla, c_prompt, c_sample, w_ada, b_ada, norm_g,
           sb_w_in, sb_w_out, gla_w_in, gla_w_a2, gla_b_a, gla_norm_g, gla_w_out, final_norm_g):
    depth, d, _ = w_ada.shape
    assert depth == 2 and sb_w_in.shape[0] == 1 and gla_w_in.shape[0] == 1
    bp, tp, _ = x_prompt.shape
    bs, ts, _ = x_sample.shape
    past = cache_sb_k.shape[2]
    nk, nv = GLA_HEADS * GLA_DK, GLA_HEADS * GLA_DV
    main = 2 * nk + 2 * nv

    mod = _ada(jnp.concatenate([c_prompt, c_sample], axis=0), w_ada, b_ada)
    split = lambda m: (m[:, :d], m[:, d:2 * d], m[:, 2 * d:])
    mods_p = [split(mod[l, :bp]) for l in range(depth)]
    mods_s = [split(mod[l, bp:]) for l in range(depth)]

    gla_w = gla_w_in[0]
    gla_wa = jnp.pad(gla_w[:, main:], ((0, 0), (0, V7X_LANES - GLA_GATE_RANK))).astype(BF16)
    gla_wa2 = jnp.pad(gla_w_a2[0], ((0, V7X_LANES - GLA_GATE_RANK), (0, 0))).astype(BF16)
    sb_w = sb_w_in[0]
    sb_w = jnp.concatenate([sb_w[:, d:3 * d], sb_w[:, :d], sb_w[:, 3 * d:]], axis=1).astype(BF16)
    weights = (norm_g, sb_w, sb_w_out[0].astype(BF16), gla_w[:, :main].astype(BF16),
               gla_wa, gla_wa2, gla_b_a[0], gla_norm_g[0], gla_w_out[0].astype(BF16), final_norm_g)

    y_p, k_p, v_p, s_p = _trunk(x_prompt, mods_p, weights, per_row=False,
                                tm_in=min(512, tp), tm_out=min(512, tp), gla_tt=512)
    cache = (cache_sb_k[0].reshape(bs, past * SB_HEADS, SB_HEAD_DIM),
             cache_sb_v[0].reshape(bs, past * SB_HEADS, SB_HEAD_DIM))
    y_s, k_s, v_s, s_s = _trunk(x_sample, mods_s, weights, per_row=True,
                                tm_in=bs * ts, tm_out=bs * ts, gla_tt=ts, cache=cache, state=state_gla[0])

    heads = lambda a: a.reshape(1, a.shape[0], a.shape[1], SB_HEADS, SB_HEAD_DIM)
    return (y_p, y_s, heads(k_p), heads(v_p), heads(k_s), heads(v_s), s_p[None], s_s[None])
```

```python
import functools
import math

import jax
import jax.numpy as jnp
from jax import lax
from jax.experimental import pallas as pl
from jax.experimental.pallas import tpu as pltpu

F32 = jnp.float32
BF16 = jnp.bfloat16

SB_HEADS = 16
SB_HEAD_DIM = 128
GLA_HEADS = 4
GLA_DK = 256
GLA_DV = 512
GLA_GATE_RANK = 16
GLA_TAU = 16.0
GLA_CHUNK = 64
EPS = 1e-6

V7X_LANES = 128
V7X_VMEM_LIMIT_BYTES = 56 * 1024 * 1024
SB_BLOCK = 256
SB_QUERY_TILE = 2048
SB_UNDERFLOW_LOG2 = 160.0
SB_FINISHED_RUN = 1e30
NORM_ROWS = 16
NORM_UNROLL = 4


def _params(semantics):
    return pltpu.CompilerParams(dimension_semantics=semantics,
                                vmem_limit_bytes=V7X_VMEM_LIMIT_BYTES)


def _split_bf16(x):
    hi = x.astype(BF16)
    lo = (x - hi.astype(F32)).astype(BF16)
    return hi, lo


def _log_sigmoid(z):
    return jnp.minimum(z, 0.0) - jnp.log(1.0 + jnp.exp(-jnp.abs(z)))


def _silu(z):
    return z / (1.0 + jnp.exp(-z))


def _ada_kernel(c_ref, w_ref, b_ref, o_ref):
    acc = jnp.dot(c_ref[...].astype(BF16), w_ref[0].astype(BF16), preferred_element_type=F32)
    o_ref[0] = acc + b_ref[0]


def _ada(c_all, w_ada, b_ada, tn=512):
    depth, d, n = w_ada.shape
    rows = c_all.shape[0]
    return pl.pallas_call(
        _ada_kernel,
        grid=(depth, n // tn),
        in_specs=[pl.BlockSpec((rows, d), lambda l, j: (0, 0)),
                  pl.BlockSpec((1, d, tn), lambda l, j: (l, 0, j)),
                  pl.BlockSpec((1, 1, tn), lambda l, j: (l, 0, j))],
        out_specs=pl.BlockSpec((1, rows, tn), lambda l, j: (l, 0, j)),
        out_shape=jax.ShapeDtypeStruct((depth, rows, n), F32),
        compiler_params=_params(("arbitrary", "arbitrary")),
        name="ada",
    )(c_all, w_ada, b_ada.reshape(depth, 1, n))


def _inproj_kernel(*refs, n_f32, bf16_mults, tm, with_decay):
    x_ref, sc_ref, sh_ref, g_ref, w_ref = refs[:5]
    pos = 5
    if with_decay:
        wa_ref, wa2_ref, ba_ref = refs[pos:pos + 3]
        pos += 3
    of_refs = refs[pos:pos + n_f32]
    ob_ref = refs[pos + n_f32]
    pos += n_f32 + 1
    if with_decay:
        dec_ref = refs[pos]
        pos += 1
    h_ref = refs[pos]
    n = pl.program_id(2)
    n_seg = n_f32 + len(bf16_mults)
    tile = pl.program_id(0) * pl.num_programs(1) + pl.program_id(1)
    cur = tile % 2
    steps = tm // NORM_ROWS

    def norm_rows(r, slot):
        first = r * NORM_ROWS
        rows = pl.ds(first if isinstance(first, int) else pl.multiple_of(first, NORM_ROWS), NORM_ROWS)
        x = x_ref[0, rows, :]
        y = x * lax.rsqrt(jnp.mean(x * x, axis=-1, keepdims=True) + EPS) * g_ref[...]
        if sc_ref.shape[1] == 1:
            sc, sh = sc_ref[0], sh_ref[0]
        else:
            sc, sh = sc_ref[0, rows, :], sh_ref[0, rows, :]
        h_ref[slot, rows, :] = (y * (1.0 + sc) + sh).astype(BF16)

    @pl.when((tile == 0) & (n == 0))
    def _():
        def body(r, carry):
            norm_rows(r, 0)
            return carry
        lax.fori_loop(0, steps, body, 0, unroll=NORM_UNROLL if steps % NORM_UNROLL == 0 else 1)

    for seg in range(n_seg):
        @pl.when(n == seg)
        def _(seg=seg):
            if seg == 1:
                for r in range(steps):
                    norm_rows(r, 1 - cur)
            acc = jnp.dot(h_ref[cur], w_ref[...], preferred_element_type=F32)
            if seg < n_f32:
                of_refs[seg][0] = acc
            else:
                mult = bf16_mults[seg - n_f32]
                ob_ref[0, 0] = (acc if mult is None else acc * mult).astype(BF16)
            if with_decay and seg == n_seg - 1:
                a_lr = jnp.dot(h_ref[cur], wa_ref[...], preferred_element_type=F32)
                pre = jnp.dot(a_lr.astype(BF16), wa2_ref[...], preferred_element_type=F32) + ba_ref[...]
                dec_ref[0] = _log_sigmoid(pre) * (1.0 / GLA_TAU)


def _inproj(x, scale, shift, g, w, seg_order, n_f32, bf16_mults, *, tm, decay=None):
    b, t, d = x.shape
    n_seg = len(seg_order)
    n_bf16 = len(bf16_mults)
    assert t % tm == 0 and w.shape[1] >= n_seg * d and n_seg == n_f32 + n_bf16 and n_seg >= 2
    assert tm % NORM_ROWS == 0
    tiles_per_batch = t // tm
    last_tile = b * tiles_per_batch - 1

    def norm_tile(bi, i, j):
        tile = bi * tiles_per_batch + i
        return jnp.where(j == 0, tile, jnp.minimum(tile + 1, last_tile))

    def rows_map(bi, i, j):
        tile = norm_tile(bi, i, j)
        return tile // tiles_per_batch, tile % tiles_per_batch, 0

    def segment(j):
        seg = jnp.int32(seg_order[-1])
        for step, s in enumerate(seg_order[:-1]):
            seg = jnp.where(j == step, s, seg)
        return seg

    mod_rows = scale.shape[1]
    mod_spec = (pl.BlockSpec((1, 1, d), lambda bi, i, j: (norm_tile(bi, i, j) // tiles_per_batch, 0, 0))
                if mod_rows == 1 else pl.BlockSpec((1, tm, d), rows_map))
    in_specs = [pl.BlockSpec((1, tm, d), rows_map), mod_spec, mod_spec,
                pl.BlockSpec((1, d), lambda bi, i, j: (0, 0)),
                pl.BlockSpec((d, d), lambda bi, i, j: (0, segment(j)))]
    args = [x, scale, shift, g.reshape(1, d), w]
    if decay is not None:
        wa, wa2, ba = decay
        in_specs += [pl.BlockSpec(wa.shape, lambda bi, i, j: (0, 0)),
                     pl.BlockSpec(wa2.shape, lambda bi, i, j: (0, 0)),
                     pl.BlockSpec((1, ba.shape[-1]), lambda bi, i, j: (0, 0))]
        args += [wa, wa2, ba.reshape(1, -1)]

    out_specs = [pl.BlockSpec((1, tm, d), lambda bi, i, j: (bi, i, 0)) for _ in range(n_f32)]
    out_specs.append(pl.BlockSpec((1, 1, tm, d), lambda bi, i, j: (jnp.maximum(j - n_f32, 0), bi, i, 0)))
    out_shape = [jax.ShapeDtypeStruct((b, t, d), F32) for _ in range(n_f32)]
    out_shape.append(jax.ShapeDtypeStruct((n_bf16, b, t, d), BF16))
    if decay is not None:
        nk = decay[1].shape[1]
        out_specs.append(pl.BlockSpec((1, tm, nk), lambda bi, i, j: (bi, i, 0)))
        out_shape.append(jax.ShapeDtypeStruct((b, t, nk), F32))
    return pl.pallas_call(
        functools.partial(_inproj_kernel, n_f32=n_f32, bf16_mults=tuple(bf16_mults), tm=tm,
                          with_decay=decay is not None),
        grid=(b, t // tm, n_seg),
        in_specs=in_specs, out_specs=out_specs, out_shape=out_shape,
        scratch_shapes=[pltpu.VMEM((2, tm, d), BF16)],
        compiler_params=_params(("arbitrary", "arbitrary", "arbitrary")),
        name="inproj",
    )(*args)


def _sb_blocks(qs, k_blks, v_blks, w, runs, mask=None):
    half = V7X_LANES
    zs = [lax.dot_general(q, k_blk, (((1,), (1,)), ((), ())), preferred_element_type=F32)
          for q, k_blk in zip(qs, k_blks)]
    sums = []
    for z in zs:
        neg_abs = lax.bitcast_convert_type(lax.bitcast_convert_type(z, jnp.uint32) | jnp.uint32(1 << 31), F32)
        sp = jnp.maximum(z, 0.0) + jnp.log2(1.0 + jnp.exp2(neg_abs))
        if mask is not None:
            sp = jnp.where(mask, sp, 0.0)
        hi, lo = _split_bf16(sp)
        sums.append((jnp.dot(jnp.concatenate([hi[:, half:], lo[:, half:]], axis=1), w,
                             preferred_element_type=F32),
                     jnp.dot(jnp.concatenate([hi[:, :half], lo[:, :half]], axis=1), w,
                             preferred_element_type=F32)))
    outs = []
    for z, (cs_r, cs_l), v_blk, run in zip(zs, sums, v_blks, runs):
        a_r = jnp.exp2(z[:, half:] - cs_r[:, :half] - run)
        run = run + cs_r[:, half:]
        a_l = jnp.exp2(z[:, :half] - cs_l[:, :half] - run)
        run = run + cs_l[:, half:]
        a = jnp.concatenate([a_l, a_r], axis=1)
        if mask is not None:
            a = jnp.where(mask, a, 0.0)
        outs.append((jnp.dot(a.astype(BF16), v_blk, preferred_element_type=F32), run))
    return outs


def _sb_consts():
    j = lax.broadcasted_iota(jnp.int32, (V7X_LANES, V7X_LANES), 0)
    s = lax.broadcasted_iota(jnp.int32, (V7X_LANES, V7X_LANES), 1)
    half = jnp.concatenate([(j >= s).astype(BF16), jnp.ones((V7X_LANES, V7X_LANES), BF16)], axis=1)
    return jnp.concatenate([half, half], axis=0)


def _causal_mask(rows):
    t_idx = lax.broadcasted_iota(jnp.int32, (rows, SB_BLOCK), 0)
    s_idx = lax.broadcasted_iota(jnp.int32, (rows, SB_BLOCK), 1)
    return s_idx < t_idx


def _sb_prompt_kernel(q_ref, k_ref, v_ref, zg_ref, w_ref, o_ref, acc_ref, run_ref, *, tq):
    i = pl.program_id(2)
    groups = tq // SB_BLOCK
    w = w_ref[...]

    def keys(first):
        rows = pl.ds(pl.multiple_of(first, SB_BLOCK), SB_BLOCK)
        return k_ref[0, rows, :].astype(BF16), v_ref[0, rows, :].astype(BF16)

    group_rows = [slice(g * SB_BLOCK, (g + 1) * SB_BLOCK) for g in range(groups)]
    queries = lambda: [q_ref[0, rows, :] for rows in group_rows]
    kvs = [keys((i * groups + g) * SB_BLOCK) for g in range(groups)]
    outs = _sb_blocks(queries(), [kv[0] for kv in kvs], [kv[1] for kv in kvs], w,
                      [jnp.zeros((SB_BLOCK, V7X_LANES), F32)] * groups, mask=_causal_mask(SB_BLOCK))
    for rows, (pv, run) in zip(group_rows, outs):
        acc_ref[rows, :] = pv
        run_ref[rows, :] = run

    def pending(d):
        low = jnp.float32(jnp.inf)
        for g in range(groups):
            rows = slice(g * SB_BLOCK, (g + 1) * SB_BLOCK)
            low = jnp.minimum(low, jnp.where(i * groups + g - d >= 0, jnp.min(run_ref[rows, :]), jnp.inf))
        return low

    def more(carry):
        return carry[1] < SB_UNDERFLOW_LOG2

    def diagonal(carry):
        d = carry[0]
        blks = [i * groups + g - d for g in range(groups)]
        kvs = [keys(jnp.maximum(blk, 0) * SB_BLOCK) for blk in blks]
        runs = [jnp.where(blk >= 0, run_ref[rows, :], SB_FINISHED_RUN) for blk, rows in zip(blks, group_rows)]
        outs = _sb_blocks(queries(), [kv[0] for kv in kvs], [kv[1] for kv in kvs], w, runs)
        for rows, (pv, run) in zip(group_rows, outs):
            acc_ref[rows, :] += pv
            run_ref[rows, :] = run
        return d + 1, pending(d + 1)

    lax.while_loop(more, diagonal, (jnp.int32(1), pending(1)))
    o_ref[0] = (acc_ref[...] * _silu(zg_ref[0].astype(F32))).astype(o_ref.dtype)


def _sb_prompt(qz, k, v, tq):
    _, b, t, _ = qz.shape
    assert t % tq == 0 and tq % SB_BLOCK == 0
    tile = pl.BlockSpec((1, tq, SB_HEAD_DIM), lambda bi, h, i: (bi, i, h))
    stacked = lambda which: pl.BlockSpec((None, 1, tq, SB_HEAD_DIM), lambda bi, h, i: (which, bi, i, h))
    whole = pl.BlockSpec((1, t, SB_HEAD_DIM), lambda bi, h, i: (bi, 0, h))
    return pl.pallas_call(
        functools.partial(_sb_prompt_kernel, tq=tq),
        grid=(b, SB_HEADS, t // tq),
        in_specs=[stacked(0), whole, whole, stacked(1),
                  pl.BlockSpec((SB_BLOCK, SB_BLOCK), lambda bi, h, i: (0, 0))],
        out_specs=tile,
        out_shape=jax.ShapeDtypeStruct(qz.shape[1:], BF16),
        scratch_shapes=[pltpu.VMEM((tq, SB_HEAD_DIM), F32), pltpu.VMEM((tq, V7X_LANES), F32)],
        compiler_params=_params(("arbitrary", "arbitrary", "arbitrary")),
        name="sb_prompt",
    )(qz, k, v, qz, _sb_consts())


def _sb_sample_kernel(q_ref, zg_ref, kn_ref, vn_ref, kc_hbm, vc_hbm, w_ref, o_ref,
                      kbuf, vbuf, sem, acc_ref, run_ref, *, n_blocks):
    bi = pl.program_id(0)
    t_new = q_ref.shape[1]
    block_rows = SB_BLOCK * SB_HEADS
    w = w_ref[...]

    def fetch(j, slot):
        src = pl.ds(j * block_rows, block_rows)
        return (pltpu.make_async_copy(kc_hbm.at[bi, src, :], kbuf.at[slot], sem.at[0, slot]),
                pltpu.make_async_copy(vc_hbm.at[bi, src, :], vbuf.at[slot], sem.at[1, slot]))

    for copy in fetch(n_blocks - 1, 0):
        copy.start()

    def head(h):
        return slice(h * t_new, (h + 1) * t_new), slice(h * SB_HEAD_DIM, (h + 1) * SB_HEAD_DIM)

    mask = _causal_mask(t_new)
    padding = jnp.zeros((SB_BLOCK - t_new, SB_HEAD_DIM), BF16)
    heads = [head(h) for h in range(SB_HEADS)]
    padded = lambda ref, cols: jnp.concatenate([ref[0, :, cols].astype(BF16), padding], axis=0)
    outs = _sb_blocks([q_ref[0, :, cols] for _, cols in heads], [padded(kn_ref, cols) for _, cols in heads],
                      [padded(vn_ref, cols) for _, cols in heads], w,
                      [jnp.zeros((t_new, V7X_LANES), F32)] * SB_HEADS, mask=mask)
    for (rows, _), (pv, run) in zip(heads, outs):
        acc_ref[rows, :] = pv
        run_ref[rows, :] = run

    def more(carry):
        return (carry[0] >= 0) & (carry[1] < SB_UNDERFLOW_LOG2)

    def past_block(carry):
        j = carry[0]
        slot = (n_blocks - 1 - j) % 2
        for copy in fetch(j, slot):
            copy.wait()

        @pl.when(j > 0)
        def _():
            for copy in fetch(j - 1, 1 - slot):
                copy.start()

        keys = [pl.ds(h, SB_BLOCK, stride=SB_HEADS) for h in range(SB_HEADS)]
        outs = _sb_blocks([q_ref[0, :, cols] for _, cols in heads],
                          [kbuf[slot, rows, :].astype(BF16) for rows in keys],
                          [vbuf[slot, rows, :].astype(BF16) for rows in keys], w,
                          [run_ref[rows, :] for rows, _ in heads])
        for (rows, _), (pv, run) in zip(heads, outs):
            acc_ref[rows, :] += pv
            run_ref[rows, :] = run
        return j - 1, jnp.min(run_ref[...])

    j_end, _ = lax.while_loop(more, past_block, (jnp.int32(n_blocks - 1), jnp.min(run_ref[...])))

    @pl.when(j_end >= 0)
    def _():
        for copy in fetch(j_end, (n_blocks - 1 - j_end) % 2):
            copy.wait()

    for h in range(SB_HEADS):
        rows, cols = head(h)
        o_ref[0, :, cols] = (acc_ref[rows, :] * _silu(zg_ref[0, :, cols].astype(F32))).astype(o_ref.dtype)


def _sb_sample(qz, k_new, v_new, k_past, v_past):
    _, b, t, d = qz.shape
    n_blocks = k_past.shape[1] // (SB_BLOCK * SB_HEADS)
    assert k_past.shape[1] == n_blocks * SB_BLOCK * SB_HEADS and n_blocks >= 1 and t <= V7X_LANES
    new = pl.BlockSpec((1, t, d), lambda bi: (bi, 0, 0))
    stacked = lambda which: pl.BlockSpec((None, 1, t, d), lambda bi: (which, bi, 0, 0))
    return pl.pallas_call(
        functools.partial(_sb_sample_kernel, n_blocks=n_blocks),
        grid=(b,),
        in_specs=[stacked(0), stacked(1), new, new,
                  pl.BlockSpec(memory_space=pl.ANY), pl.BlockSpec(memory_space=pl.ANY),
                  pl.BlockSpec((SB_BLOCK, SB_BLOCK), lambda bi: (0, 0))],
        out_specs=new,
        out_shape=jax.ShapeDtypeStruct(qz.shape[1:], BF16),
        scratch_shapes=[pltpu.VMEM((2, SB_BLOCK * SB_HEADS, SB_HEAD_DIM), F32),
                        pltpu.VMEM((2, SB_BLOCK * SB_HEADS, SB_HEAD_DIM), F32),
                        pltpu.SemaphoreType.DMA((2, 2)),
                        pltpu.VMEM((SB_HEADS * t, SB_HEAD_DIM), F32),
                        pltpu.VMEM((SB_HEADS * t, V7X_LANES), F32)],
        compiler_params=_params(("arbitrary",)),
        name="sb_sample",
    )(qz, qz, k_new, v_new, k_past, v_past, _sb_consts())


def _gla_kernel(*refs, chunk, n_chunks, has_state):
    if has_state:
        q_ref, k_ref, v_ref, dec_ref, r_ref, gn_ref, tril_ref, s0_ref, o_ref, sf_ref, st_ref, upd_ref = refs
    else:
        q_ref, k_ref, v_ref, dec_ref, r_ref, gn_ref, tril_ref, o_ref, sf_ref, st_ref, upd_ref = refs
    step = pl.program_id(2)

    @pl.when(step == 0)
    def _():
        st_ref[...] = s0_ref[0, 0].T if has_state else jnp.zeros_like(st_ref)

    tril2 = tril_ref[...]
    causal = (lax.broadcasted_iota(jnp.int32, (chunk, chunk), 1)
              <= lax.broadcasted_iota(jnp.int32, (chunk, chunk), 0))
    q_scale = GLA_DK ** -0.5
    nt_dot = lambda a, b: lax.dot_general(a, b, (((1,), (1,)), ((), ())), preferred_element_type=F32)
    chunks = [pl.ds(c * chunk, chunk) for c in range(n_chunks)]

    q_in, k_out, k_end, decay = [], [], [], []
    for rows in chunks:
        g_hi, g_lo = _split_bf16(dec_ref[0, rows, :])
        b = jnp.dot(tril2, jnp.concatenate([g_hi, g_lo], axis=0), preferred_element_type=F32)
        b_last = b[chunk - 1:chunk, :]
        k = k_ref[0, rows, :]
        q_in.append((q_ref[0, rows, :] * q_scale * jnp.exp(b)).astype(BF16))
        k_out.append((k * jnp.exp(-b)).astype(BF16))
        k_end.append((k * jnp.exp(b_last - b)).astype(BF16))
        decay.append(jnp.exp(b_last))
    intra = []
    for c, rows in enumerate(chunks):
        v = v_ref[0, rows, :]
        a = jnp.where(causal, nt_dot(q_in[c], k_out[c]), 0.0).astype(BF16)
        intra.append(jnp.dot(a, v, preferred_element_type=F32))
        upd_ref[c] = lax.dot_general(v, k_end[c], (((0,), (0,)), ((), ())), preferred_element_type=F32)
    for c, rows in enumerate(chunks):
        state_t = st_ref[...]
        o = intra[c] + nt_dot(q_in[c], state_t.astype(BF16))
        st_ref[...] = state_t * decay[c] + upd_ref[c]
        o = o * lax.rsqrt(jnp.mean(o * o, axis=-1, keepdims=True) + EPS) * gn_ref[...]
        o_ref[0, rows, :] = (o * _silu(r_ref[0, rows, :].astype(F32))).astype(o_ref.dtype)

    @pl.when(step == pl.num_programs(2) - 1)
    def _():
        sf_ref[0, 0] = st_ref[...].T


def _gla(qk, vr, dec, gn, s0, *, chunk, tt):
    _, b, t, _ = vr.shape
    assert t % tt == 0 and tt % chunk == 0
    tril = (lax.broadcasted_iota(jnp.int32, (chunk, chunk), 1)
            <= lax.broadcasted_iota(jnp.int32, (chunk, chunk), 0)).astype(BF16)
    tril2 = jnp.concatenate([tril, tril], axis=1)
    n_chunks = tt // chunk
    kspec = lambda off: pl.BlockSpec((1, tt, GLA_DK), lambda bi, h, s: (bi, s, h + off))
    vspec = pl.BlockSpec((1, tt, GLA_DV), lambda bi, h, s: (bi, s, h))
    sspec = pl.BlockSpec((1, 1, GLA_DK, GLA_DV), lambda bi, h, s: (bi, h, 0, 0))
    stacked = lambda which: pl.BlockSpec((None, 1, tt, GLA_DV), lambda bi, h, s: (which, bi, s, h))
    in_specs = [kspec(0), kspec(GLA_HEADS), stacked(0), kspec(0), stacked(1),
                pl.BlockSpec((1, GLA_DV), lambda bi, h, s: (0, h)),
                pl.BlockSpec((chunk, 2 * chunk), lambda bi, h, s: (0, 0))]
    args = [qk, qk, vr, dec, vr, gn.reshape(1, -1), tril2]
    if s0 is not None:
        in_specs.append(sspec)
        args.append(s0)
    return pl.pallas_call(
        functools.partial(_gla_kernel, chunk=chunk, n_chunks=n_chunks, has_state=s0 is not None),
        grid=(b, GLA_HEADS, t // tt),
        in_specs=in_specs,
        out_specs=[vspec, sspec],
        out_shape=[jax.ShapeDtypeStruct(vr.shape[1:], BF16),
                   jax.ShapeDtypeStruct((b, GLA_HEADS, GLA_DK, GLA_DV), F32)],
        scratch_shapes=[pltpu.VMEM((GLA_DV, GLA_DK), F32), pltpu.VMEM((n_chunks, GLA_DV, GLA_DK), F32)],
        compiler_params=_params(("arbitrary", "arbitrary", "arbitrary")),
        name="gla",
    )(*args)


def _outproj_kernel(y_ref, w_ref, x_ref, gate_ref, *rest, final_norm):
    if final_norm:
        gf_ref, o_ref = rest
    else:
        (o_ref,) = rest
    y = jnp.dot(y_ref[0], w_ref[...], preferred_element_type=F32)
    x = x_ref[0] + gate_ref[0] * y
    if final_norm:
        x = x * lax.rsqrt(jnp.mean(x * x, axis=-1, keepdims=True) + EPS) * gf_ref[...]
    o_ref[0] = x


def _outproj(y, w, x, gate, gf, *, tm):
    b, t, d = x.shape
    kdim = y.shape[-1]
    assert t % tm == 0
    gate_spec = (pl.BlockSpec((1, 1, d), lambda bi, i: (bi, 0, 0)) if gate.shape[1] == 1
                 else pl.BlockSpec((1, tm, d), lambda bi, i: (bi, i, 0)))
    in_specs = [pl.BlockSpec((1, tm, kdim), lambda bi, i: (bi, i, 0)),
                pl.BlockSpec((kdim, d), lambda bi, i: (0, 0)),
                pl.BlockSpec((1, tm, d), lambda bi, i: (bi, i, 0)),
                gate_spec]
    args = [y, w, x, gate]
    if gf is not None:
        in_specs.append(pl.BlockSpec((1, d), lambda bi, i: (0, 0)))
        args.append(gf.reshape(1, d))
    return pl.pallas_call(
        functools.partial(_outproj_kernel, final_norm=gf is not None),
        grid=(b, t // tm),
        in_specs=in_specs,
        out_specs=pl.BlockSpec((1, tm, d), lambda bi, i: (bi, i, 0)),
        out_shape=jax.ShapeDtypeStruct(x.shape, F32),
        compiler_params=_params(("arbitrary", "arbitrary")),
        name="outproj",
    )(*args)


def _trunk(x, mods, weights, *, per_row, tm_in, tm_out, gla_tt, cache=None, state=None):
    (norm_g, sb_w_in, sb_w_out, gla_w_in, gla_wa, gla_wa2, gla_b_a, gla_norm_g, gla_w_out,
     final_norm_g) = weights
    b, t, d = x.shape
    if per_row:
        fold = lambda a: a.reshape(1, b * t, a.shape[-1])
        rows = lambda m: jnp.broadcast_to(m[:, None, :], (b, t, d)).reshape(1, b * t, d)
    else:
        fold = lambda a: a
        rows = lambda m: m[:, None, :]
    unfold = lambda a: a.reshape(b, t, a.shape[-1])
    width = SB_HEADS * SB_HEAD_DIM

    shift, scale, gate = mods[0]
    assert width == d
    k, v, qz = _inproj(fold(x), rows(scale), rows(shift), norm_g[0], sb_w_in, (1, 2, 0, 3), 2,
                       (SB_HEAD_DIM ** -0.5 * math.log2(math.e), None), tm=tm_in)
    k, v, qz = unfold(k), unfold(v), qz.reshape(2, b, t, d)
    if cache is None:
        branch = _sb_prompt(qz, k, v, tq=min(SB_QUERY_TILE, t))
    else:
        branch = _sb_sample(qz, k, v, cache[0], cache[1])
    x1 = _outproj(fold(branch), sb_w_out, fold(x), rows(gate), None, tm=tm_out)

    shift, scale, gate = mods[1]
    nk, nv = GLA_HEADS * GLA_DK, GLA_HEADS * GLA_DV
    assert 2 * nk == d and nv == d
    qk, vr, dec = _inproj(x1, rows(scale), rows(shift), norm_g[1], gla_w_in, (0, 1, 2), 1, (None, None),
                          tm=tm_in, decay=(gla_wa, gla_wa2, gla_b_a))
    chunk = min(GLA_CHUNK, t)
    branch, s_new = _gla(unfold(qk), vr.reshape(2, b, t, d), unfold(dec), gla_norm_g, state,
                         chunk=chunk, tt=min(gla_tt, t))
    y = _outproj(fold(branch), gla_w_out, x1, rows(gate), final_norm_g, tm=tm_out)
    return unfold(y), k, v, s_new


def kernel(x_prompt, x_sample, cache_sb_k, cache_sb_v, state_gla, c_prompt, c_sample, w_ada, b_ada, norm_g,
           sb_w_in, sb_w_out, gla_w_in, gla_w_a2, gla_b_a, gla_norm_g, gla_w_out, final_norm_g):
    depth, d, _ = w_ada.shape
    assert depth == 2 and sb_w_in.shape[0] == 1 and gla_w_in.shape[0] == 1
    bp, tp, _ = x_prompt.shape
    bs, ts, _ = x_sample.shape
    past = cache_sb_k.shape[2]
    nk, nv = GLA_HEADS * GLA_DK, GLA_HEADS * GLA_DV
    main = 2 * nk + 2 * nv

    mod = _ada(jnp.concatenate([c_prompt, c_sample], axis=0), w_ada, b_ada)
    split = lambda m: (m[:, :d], m[:, d:2 * d], m[:, 2 * d:])
    mods_p = [split(mod[l, :bp]) for l in range(depth)]
    mods_s = [split(mod[l, bp:]) for l in range(depth)]

    gla_w = gla_w_in[0]
    gla_wa = jnp.pad(gla_w[:, main:], ((0, 0), (0, V7X_LANES - GLA_GATE_RANK))).astype(BF16)
    gla_wa2 = jnp.pad(gla_w_a2[0], ((0, V7X_LANES - GLA_GATE_RANK), (0, 0))).astype(BF16)
    weights = (norm_g, sb_w_in[0].astype(BF16), sb_w_out[0].astype(BF16), gla_w.astype(BF16),
               gla_wa, gla_wa2, gla_b_a[0], gla_norm_g[0], gla_w_out[0].astype(BF16), final_norm_g)

    y_p, k_p, v_p, s_p = _trunk(x_prompt, mods_p, weights, per_row=False,
                                tm_in=min(512, tp), tm_out=min(512, tp), gla_tt=512)
    cache = (cache_sb_k[0].reshape(bs, past * SB_HEADS, SB_HEAD_DIM),
             cache_sb_v[0].reshape(bs, past * SB_HEADS, SB_HEAD_DIM))
    y_s, k_s, v_s, s_s = _trunk(x_sample, mods_s, weights, per_row=True,
                                tm_in=bs * ts, tm_out=bs * ts, gla_tt=ts, cache=cache, state=state_gla[0])

    heads = lambda a: a.reshape(1, a.shape[0], a.shape[1], SB_HEADS, SB_HEAD_DIM)
    return (y_p, y_s, heads(k_p), heads(v_p), heads(k_s), heads(v_s), s_p[None], s_s[None])
```

```python
import functools
import math

import jax
import jax.numpy as jnp
from jax import lax
from jax.experimental import pallas as pl
from jax.experimental.pallas import tpu as pltpu

F32 = jnp.float32
BF16 = jnp.bfloat16

SB_HEADS = 16
SB_HEAD_DIM = 128
GLA_HEADS = 4
GLA_DK = 256
GLA_DV = 512
GLA_GATE_RANK = 16
GLA_TAU = 16.0
GLA_CHUNK = 64
EPS = 1e-6

V7X_LANES = 128
V7X_VMEM_LIMIT_BYTES = 56 * 1024 * 1024
SB_BLOCK = 256
SB_QUERY_TILE = 4096
SB_UNDERFLOW_LOG2 = 160.0
SB_FINISHED_RUN = 1e30
NORM_ROWS = 16
NORM_UNROLL = 4


def _params(semantics):
    return pltpu.CompilerParams(dimension_semantics=semantics,
                                vmem_limit_bytes=V7X_VMEM_LIMIT_BYTES)


def _split_bf16(x):
    hi = x.astype(BF16)
    lo = (x - hi.astype(F32)).astype(BF16)
    return hi, lo


def _log_sigmoid(z):
    return jnp.minimum(z, 0.0) - jnp.log(1.0 + jnp.exp(-jnp.abs(z)))


def _softplus2(z):
    neg_abs = lax.bitcast_convert_type(lax.bitcast_convert_type(z, jnp.uint32) | jnp.uint32(1 << 31), F32)
    return jnp.maximum(z, 0.0) + jnp.log2(1.0 + jnp.exp2(neg_abs))


def _silu(z):
    return z / (1.0 + jnp.exp(-z))


def _ada_kernel(c_ref, w_ref, b_ref, o_ref):
    acc = jnp.dot(c_ref[...].astype(BF16), w_ref[0].astype(BF16), preferred_element_type=F32)
    o_ref[0] = acc + b_ref[0]


def _ada(c_all, w_ada, b_ada, tn=512):
    depth, d, n = w_ada.shape
    rows = c_all.shape[0]
    return pl.pallas_call(
        _ada_kernel,
        grid=(depth, n // tn),
        in_specs=[pl.BlockSpec((rows, d), lambda l, j: (0, 0)),
                  pl.BlockSpec((1, d, tn), lambda l, j: (l, 0, j)),
                  pl.BlockSpec((1, 1, tn), lambda l, j: (l, 0, j))],
        out_specs=pl.BlockSpec((1, rows, tn), lambda l, j: (l, 0, j)),
        out_shape=jax.ShapeDtypeStruct((depth, rows, n), F32),
        compiler_params=_params(("arbitrary", "arbitrary")),
        name="ada",
    )(c_all, w_ada, b_ada.reshape(depth, 1, n))


def _inproj_kernel(*refs, n_f32, bf16_mults, tm, with_decay):
    x_ref, sc_ref, sh_ref, g_ref, w_ref = refs[:5]
    pos = 5
    if with_decay:
        wa_ref, wa2_ref, ba_ref = refs[pos:pos + 3]
        pos += 3
    of_refs = refs[pos:pos + n_f32]
    ob_ref = refs[pos + n_f32]
    pos += n_f32 + 1
    if with_decay:
        dec_ref = refs[pos]
        pos += 1
    h_ref = refs[pos]
    n = pl.program_id(2)

    @pl.when(n == 0)
    def _():
        def norm_rows(r, carry):
            rows = pl.ds(pl.multiple_of(r * NORM_ROWS, NORM_ROWS), NORM_ROWS)
            x = x_ref[0, rows, :]
            y = x * lax.rsqrt(jnp.mean(x * x, axis=-1, keepdims=True) + EPS) * g_ref[...]
            if sc_ref.shape[1] == 1:
                sc, sh = sc_ref[0], sh_ref[0]
            else:
                sc, sh = sc_ref[0, rows, :], sh_ref[0, rows, :]
            h_ref[rows, :] = (y * (1.0 + sc) + sh).astype(BF16)
            return carry
        steps = tm // NORM_ROWS
        lax.fori_loop(0, steps, norm_rows, 0, unroll=NORM_UNROLL if steps % NORM_UNROLL == 0 else 1)

    for idx, of_ref in enumerate(of_refs):
        @pl.when(n == idx)
        def _(of_ref=of_ref):
            of_ref[0] = jnp.dot(h_ref[...], w_ref[...], preferred_element_type=F32)

    @pl.when(n >= n_f32)
    def _():
        acc = jnp.dot(h_ref[...], w_ref[...], preferred_element_type=F32)
        if any(m is not None for m in bf16_mults):
            mult = jnp.float32(1.0)
            for idx, m in enumerate(bf16_mults):
                if m is not None:
                    mult = jnp.where(n == n_f32 + idx, jnp.float32(m), mult)
            acc = acc * mult
        ob_ref[0, 0] = acc.astype(BF16)

    if with_decay:
        @pl.when(n == pl.num_programs(2) - 1)
        def _():
            a_lr = jnp.dot(h_ref[...], wa_ref[...], preferred_element_type=F32)
            pre = jnp.dot(a_lr.astype(BF16), wa2_ref[...], preferred_element_type=F32) + ba_ref[...]
            dec_ref[0] = _log_sigmoid(pre) * (1.0 / GLA_TAU)


def _inproj(x, scale, shift, g, w, seg_order, n_f32, bf16_mults, *, tm, decay=None):
    b, t, d = x.shape
    n_seg = len(seg_order)
    n_bf16 = len(bf16_mults)
    assert t % tm == 0 and w.shape[1] >= n_seg * d and n_seg == n_f32 + n_bf16
    assert tm % NORM_ROWS == 0

    def segment(j):
        seg = jnp.int32(seg_order[-1])
        for step, s in enumerate(seg_order[:-1]):
            seg = jnp.where(j == step, s, seg)
        return seg

    mod_rows = scale.shape[1]
    mod_spec = (pl.BlockSpec((1, 1, d), lambda bi, i, j: (bi, 0, 0)) if mod_rows == 1
                else pl.BlockSpec((1, tm, d), lambda bi, i, j: (bi, i, 0)))
    in_specs = [pl.BlockSpec((1, tm, d), lambda bi, i, j: (bi, i, 0)), mod_spec, mod_spec,
                pl.BlockSpec((1, d), lambda bi, i, j: (0, 0)),
                pl.BlockSpec((d, d), lambda bi, i, j: (0, segment(j)))]
    args = [x, scale, shift, g.reshape(1, d), w]
    if decay is not None:
        wa, wa2, ba = decay
        in_specs += [pl.BlockSpec(wa.shape, lambda bi, i, j: (0, 0)),
                     pl.BlockSpec(wa2.shape, lambda bi, i, j: (0, 0)),
                     pl.BlockSpec((1, ba.shape[-1]), lambda bi, i, j: (0, 0))]
        args += [wa, wa2, ba.reshape(1, -1)]

    out_specs = [pl.BlockSpec((1, tm, d), lambda bi, i, j: (bi, i, 0)) for _ in range(n_f32)]
    out_specs.append(pl.BlockSpec((1, 1, tm, d), lambda bi, i, j: (jnp.maximum(j - n_f32, 0), bi, i, 0)))
    out_shape = [jax.ShapeDtypeStruct((b, t, d), F32) for _ in range(n_f32)]
    out_shape.append(jax.ShapeDtypeStruct((n_bf16, b, t, d), BF16))
    if decay is not None:
        nk = decay[1].shape[1]
        out_specs.append(pl.BlockSpec((1, tm, nk), lambda bi, i, j: (bi, i, 0)))
        out_shape.append(jax.ShapeDtypeStruct((b, t, nk), F32))
    return pl.pallas_call(
        functools.partial(_inproj_kernel, n_f32=n_f32, bf16_mults=tuple(bf16_mults), tm=tm,
                          with_decay=decay is not None),
        grid=(b, t // tm, n_seg),
        in_specs=in_specs, out_specs=out_specs, out_shape=out_shape,
        scratch_shapes=[pltpu.VMEM((tm, d), BF16)],
        compiler_params=_params(("arbitrary", "arbitrary", "arbitrary")),
        name="inproj",
    )(*args)


def _sb_blocks(qs, k_blks, v_blks, w, runs, mask=None):
    half = V7X_LANES
    zs = [lax.dot_general(q, k_blk, (((1,), (1,)), ((), ())), preferred_element_type=F32)
          for q, k_blk in zip(qs, k_blks)]
    sums = []
    for z in zs:
        sp = _softplus2(z)
        if mask is not None:
            sp = jnp.where(mask, sp, 0.0)
        hi, lo = _split_bf16(sp)
        sums.append((jnp.dot(jnp.concatenate([hi[:, half:], lo[:, half:]], axis=1), w,
                             preferred_element_type=F32),
                     jnp.dot(jnp.concatenate([hi[:, :half], lo[:, :half]], axis=1), w,
                             preferred_element_type=F32)))
    outs = []
    for z, (cs_r, cs_l), v_blk, run in zip(zs, sums, v_blks, runs):
        a_r = jnp.exp2(z[:, half:] - cs_r[:, :half] - run)
        run = run + cs_r[:, half:]
        a_l = jnp.exp2(z[:, :half] - cs_l[:, :half] - run)
        run = run + cs_l[:, half:]
        a = jnp.concatenate([a_l, a_r], axis=1)
        if mask is not None:
            a = jnp.where(mask, a, 0.0)
        outs.append((jnp.dot(a.astype(BF16), v_blk, preferred_element_type=F32), run))
    return outs


def _sb_consts():
    j = lax.broadcasted_iota(jnp.int32, (V7X_LANES, V7X_LANES), 0)
    s = lax.broadcasted_iota(jnp.int32, (V7X_LANES, V7X_LANES), 1)
    half = jnp.concatenate([(j >= s).astype(BF16), jnp.ones((V7X_LANES, V7X_LANES), BF16)], axis=1)
    return jnp.concatenate([half, half], axis=0)


def _causal_mask(rows):
    t_idx = lax.broadcasted_iota(jnp.int32, (rows, SB_BLOCK), 0)
    s_idx = lax.broadcasted_iota(jnp.int32, (rows, SB_BLOCK), 1)
    return s_idx < t_idx


def _sb_prompt_kernel(q_ref, k_ref, v_ref, zg_ref, w_ref, o_ref, acc_ref, run_ref, *, tq):
    i = pl.program_id(2)
    groups = tq // SB_BLOCK
    w = w_ref[...]

    def keys(first):
        rows = pl.ds(pl.multiple_of(first, SB_BLOCK), SB_BLOCK)
        return k_ref[0, rows, :].astype(BF16), v_ref[0, rows, :].astype(BF16)

    group_rows = [slice(g * SB_BLOCK, (g + 1) * SB_BLOCK) for g in range(groups)]
    queries = lambda: [q_ref[0, rows, :] for rows in group_rows]
    kvs = [keys((i * groups + g) * SB_BLOCK) for g in range(groups)]
    outs = _sb_blocks(queries(), [kv[0] for kv in kvs], [kv[1] for kv in kvs], w,
                      [jnp.zeros((SB_BLOCK, V7X_LANES), F32)] * groups, mask=_causal_mask(SB_BLOCK))
    for rows, (pv, run) in zip(group_rows, outs):
        acc_ref[rows, :] = pv
        run_ref[rows, :] = run

    def pending(d):
        low = jnp.float32(jnp.inf)
        for g in range(groups):
            rows = slice(g * SB_BLOCK, (g + 1) * SB_BLOCK)
            low = jnp.minimum(low, jnp.where(i * groups + g - d >= 0, jnp.min(run_ref[rows, :]), jnp.inf))
        return low

    def more(carry):
        return carry[1] < SB_UNDERFLOW_LOG2

    def diagonal(carry):
        d = carry[0]
        blks = [i * groups + g - d for g in range(groups)]
        kvs = [keys(jnp.maximum(blk, 0) * SB_BLOCK) for blk in blks]
        runs = [jnp.where(blk >= 0, run_ref[rows, :], SB_FINISHED_RUN) for blk, rows in zip(blks, group_rows)]
        outs = _sb_blocks(queries(), [kv[0] for kv in kvs], [kv[1] for kv in kvs], w, runs)
        for rows, (pv, run) in zip(group_rows, outs):
            acc_ref[rows, :] += pv
            run_ref[rows, :] = run
        return d + 1, pending(d + 1)

    lax.while_loop(more, diagonal, (jnp.int32(1), pending(1)))
    o_ref[0] = (acc_ref[...] * _silu(zg_ref[0].astype(F32))).astype(o_ref.dtype)


def _sb_prompt(qz, k, v, tq):
    _, b, t, _ = qz.shape
    assert t % tq == 0 and tq % SB_BLOCK == 0
    tile = pl.BlockSpec((1, tq, SB_HEAD_DIM), lambda bi, h, i: (bi, i, h))
    stacked = lambda which: pl.BlockSpec((None, 1, tq, SB_HEAD_DIM), lambda bi, h, i: (which, bi, i, h))
    whole = pl.BlockSpec((1, t, SB_HEAD_DIM), lambda bi, h, i: (bi, 0, h))
    return pl.pallas_call(
        functools.partial(_sb_prompt_kernel, tq=tq),
        grid=(b, SB_HEADS, t // tq),
        in_specs=[stacked(0), whole, whole, stacked(1),
                  pl.BlockSpec((SB_BLOCK, SB_BLOCK), lambda bi, h, i: (0, 0))],
        out_specs=tile,
        out_shape=jax.ShapeDtypeStruct(qz.shape[1:], BF16),
        scratch_shapes=[pltpu.VMEM((tq, SB_HEAD_DIM), F32), pltpu.VMEM((tq, V7X_LANES), F32)],
        compiler_params=_params(("arbitrary", "arbitrary", "arbitrary")),
        name="sb_prompt",
    )(qz, k, v, qz, _sb_consts())


def _sb_sample_kernel(q_ref, zg_ref, kn_ref, vn_ref, kc_hbm, vc_hbm, w_ref, o_ref,
                      kbuf, vbuf, sem, acc_ref, run_ref, *, n_blocks):
    bi = pl.program_id(0)
    t_new = q_ref.shape[1]
    block_rows = SB_BLOCK * SB_HEADS
    w = w_ref[...]

    def fetch(j, slot):
        src = pl.ds(j * block_rows, block_rows)
        return (pltpu.make_async_copy(kc_hbm.at[bi, src, :], kbuf.at[slot], sem.at[0, slot]),
                pltpu.make_async_copy(vc_hbm.at[bi, src, :], vbuf.at[slot], sem.at[1, slot]))

    for copy in fetch(n_blocks - 1, 0):
        copy.start()

    def head(h):
        return slice(h * t_new, (h + 1) * t_new), slice(h * SB_HEAD_DIM, (h + 1) * SB_HEAD_DIM)

    mask = _causal_mask(t_new)
    padding = jnp.zeros((SB_BLOCK - t_new, SB_HEAD_DIM), BF16)
    heads = [head(h) for h in range(SB_HEADS)]
    padded = lambda ref, cols: jnp.concatenate([ref[0, :, cols].astype(BF16), padding], axis=0)
    outs = _sb_blocks([q_ref[0, :, cols] for _, cols in heads], [padded(kn_ref, cols) for _, cols in heads],
                      [padded(vn_ref, cols) for _, cols in heads], w,
                      [jnp.zeros((t_new, V7X_LANES), F32)] * SB_HEADS, mask=mask)
    for (rows, _), (pv, run) in zip(heads, outs):
        acc_ref[rows, :] = pv
        run_ref[rows, :] = run

    def more(carry):
        return (carry[0] >= 0) & (carry[1] < SB_UNDERFLOW_LOG2)

    def past_block(carry):
        j = carry[0]
        slot = (n_blocks - 1 - j) % 2
        for copy in fetch(j, slot):
            copy.wait()

        @pl.when(j > 0)
        def _():
            for copy in fetch(j - 1, 1 - slot):
                copy.start()

        keys = [pl.ds(h, SB_BLOCK, stride=SB_HEADS) for h in range(SB_HEADS)]
        outs = _sb_blocks([q_ref[0, :, cols] for _, cols in heads],
                          [kbuf[slot, rows, :].astype(BF16) for rows in keys],
                          [vbuf[slot, rows, :].astype(BF16) for rows in keys], w,
                          [run_ref[rows, :] for rows, _ in heads])
        for (rows, _), (pv, run) in zip(heads, outs):
            acc_ref[rows, :] += pv
            run_ref[rows, :] = run
        return j - 1, jnp.min(run_ref[...])

    j_end, _ = lax.while_loop(more, past_block, (jnp.int32(n_blocks - 1), jnp.min(run_ref[...])))

    @pl.when(j_end >= 0)
    def _():
        for copy in fetch(j_end, (n_blocks - 1 - j_end) % 2):
            copy.wait()

    for h in range(SB_HEADS):
        rows, cols = head(h)
        o_ref[0, :, cols] = (acc_ref[rows, :] * _silu(zg_ref[0, :, cols].astype(F32))).astype(o_ref.dtype)


def _sb_sample(qz, k_new, v_new, k_past, v_past):
    _, b, t, d = qz.shape
    n_blocks = k_past.shape[1] // (SB_BLOCK * SB_HEADS)
    assert k_past.shape[1] == n_blocks * SB_BLOCK * SB_HEADS and n_blocks >= 1 and t <= V7X_LANES
    new = pl.BlockSpec((1, t, d), lambda bi: (bi, 0, 0))
    stacked = lambda which: pl.BlockSpec((None, 1, t, d), lambda bi: (which, bi, 0, 0))
    return pl.pallas_call(
        functools.partial(_sb_sample_kernel, n_blocks=n_blocks),
        grid=(b,),
        in_specs=[stacked(0), stacked(1), new, new,
                  pl.BlockSpec(memory_space=pl.ANY), pl.BlockSpec(memory_space=pl.ANY),
                  pl.BlockSpec((SB_BLOCK, SB_BLOCK), lambda bi: (0, 0))],
        out_specs=new,
        out_shape=jax.ShapeDtypeStruct(qz.shape[1:], BF16),
        scratch_shapes=[pltpu.VMEM((2, SB_BLOCK * SB_HEADS, SB_HEAD_DIM), F32),
                        pltpu.VMEM((2, SB_BLOCK * SB_HEADS, SB_HEAD_DIM), F32),
                        pltpu.SemaphoreType.DMA((2, 2)),
                        pltpu.VMEM((SB_HEADS * t, SB_HEAD_DIM), F32),
                        pltpu.VMEM((SB_HEADS * t, V7X_LANES), F32)],
        compiler_params=_params(("arbitrary",)),
        name="sb_sample",
    )(qz, qz, k_new, v_new, k_past, v_past, _sb_consts())


def _gla_kernel(*refs, chunk, n_chunks, has_state):
    if has_state:
        q_ref, k_ref, v_ref, dec_ref, r_ref, gn_ref, tril_ref, s0_ref, o_ref, sf_ref, st_ref, upd_ref = refs
    else:
        q_ref, k_ref, v_ref, dec_ref, r_ref, gn_ref, tril_ref, o_ref, sf_ref, st_ref, upd_ref = refs
    step = pl.program_id(2)

    @pl.when(step == 0)
    def _():
        st_ref[...] = s0_ref[0, 0].T if has_state else jnp.zeros_like(st_ref)

    tril2 = tril_ref[...]
    causal = (lax.broadcasted_iota(jnp.int32, (chunk, chunk), 1)
              <= lax.broadcasted_iota(jnp.int32, (chunk, chunk), 0))
    q_scale = GLA_DK ** -0.5
    nt_dot = lambda a, b: lax.dot_general(a, b, (((1,), (1,)), ((), ())), preferred_element_type=F32)
    chunks = [pl.ds(c * chunk, chunk) for c in range(n_chunks)]

    q_in, k_out, k_end, decay = [], [], [], []
    for rows in chunks:
        g_hi, g_lo = _split_bf16(dec_ref[0, rows, :])
        b = jnp.dot(tril2, jnp.concatenate([g_hi, g_lo], axis=0), preferred_element_type=F32)
        b_last = b[chunk - 1:chunk, :]
        k = k_ref[0, rows, :]
        q_in.append((q_ref[0, rows, :] * q_scale * jnp.exp(b)).astype(BF16))
        k_out.append((k * jnp.exp(-b)).astype(BF16))
        k_end.append((k * jnp.exp(b_last - b)).astype(BF16))
        decay.append(jnp.exp(b_last))
    intra = []
    for c, rows in enumerate(chunks):
        v = v_ref[0, rows, :]
        a = jnp.where(causal, nt_dot(q_in[c], k_out[c]), 0.0).astype(BF16)
        intra.append(jnp.dot(a, v, preferred_element_type=F32))
        upd_ref[c] = lax.dot_general(v, k_end[c], (((0,), (0,)), ((), ())), preferred_element_type=F32)
    for c, rows in enumerate(chunks):
        state_t = st_ref[...]
        o = intra[c] + nt_dot(q_in[c], state_t.astype(BF16))
        st_ref[...] = state_t * decay[c] + upd_ref[c]
        o = o * lax.rsqrt(jnp.mean(o * o, axis=-1, keepdims=True) + EPS) * gn_ref[...]
        o_ref[0, rows, :] = (o * _silu(r_ref[0, rows, :].astype(F32))).astype(o_ref.dtype)

    @pl.when(step == pl.num_programs(2) - 1)
    def _():
        sf_ref[0, 0] = st_ref[...].T


def _gla(qk, vr, dec, gn, s0, *, chunk, tt):
    _, b, t, _ = vr.shape
    assert t % tt == 0 and tt % chunk == 0
    tril = (lax.broadcasted_iota(jnp.int32, (chunk, chunk), 1)
            <= lax.broadcasted_iota(jnp.int32, (chunk, chunk), 0)).astype(BF16)
    tril2 = jnp.concatenate([tril, tril], axis=1)
    n_chunks = tt // chunk
    kspec = lambda off: pl.BlockSpec((1, tt, GLA_DK), lambda bi, h, s: (bi, s, h + off))
    vspec = pl.BlockSpec((1, tt, GLA_DV), lambda bi, h, s: (bi, s, h))
    sspec = pl.BlockSpec((1, 1, GLA_DK, GLA_DV), lambda bi, h, s: (bi, h, 0, 0))
    stacked = lambda which: pl.BlockSpec((None, 1, tt, GLA_DV), lambda bi, h, s: (which, bi, s, h))
    in_specs = [kspec(0), kspec(GLA_HEADS), stacked(0), kspec(0), stacked(1),
                pl.BlockSpec((1, GLA_DV), lambda bi, h, s: (0, h)),
                pl.BlockSpec((chunk, 2 * chunk), lambda bi, h, s: (0, 0))]
    args = [qk, qk, vr, dec, vr, gn.reshape(1, -1), tril2]
    if s0 is not None:
        in_specs.append(sspec)
        args.append(s0)
    return pl.pallas_call(
        functools.partial(_gla_kernel, chunk=chunk, n_chunks=n_chunks, has_state=s0 is not None),
        grid=(b, GLA_HEADS, t // tt),
        in_specs=in_specs,
        out_specs=[vspec, sspec],
        out_shape=[jax.ShapeDtypeStruct(vr.shape[1:], BF16),
                   jax.ShapeDtypeStruct((b, GLA_HEADS, GLA_DK, GLA_DV), F32)],
        scratch_shapes=[pltpu.VMEM((GLA_DV, GLA_DK), F32), pltpu.VMEM((n_chunks, GLA_DV, GLA_DK), F32)],
        compiler_params=_params(("arbitrary", "arbitrary", "arbitrary")),
        name="gla",
    )(*args)


def _outproj_kernel(y_ref, w_ref, x_ref, gate_ref, *rest, final_norm):
    if final_norm:
        gf_ref, o_ref = rest
    else:
        (o_ref,) = rest
    y = jnp.dot(y_ref[0], w_ref[...], preferred_element_type=F32)
    x = x_ref[0] + gate_ref[0] * y
    if final_norm:
        x = x * lax.rsqrt(jnp.mean(x * x, axis=-1, keepdims=True) + EPS) * gf_ref[...]
    o_ref[0] = x


def _outproj(y, w, x, gate, gf, *, tm):
    b, t, d = x.shape
    kdim = y.shape[-1]
    assert t % tm == 0
    gate_spec = (pl.BlockSpec((1, 1, d), lambda bi, i: (bi, 0, 0)) if gate.shape[1] == 1
                 else pl.BlockSpec((1, tm, d), lambda bi, i: (bi, i, 0)))
    in_specs = [pl.BlockSpec((1, tm, kdim), lambda bi, i: (bi, i, 0)),
                pl.BlockSpec((kdim, d), lambda bi, i: (0, 0)),
                pl.BlockSpec((1, tm, d), lambda bi, i: (bi, i, 0)),
                gate_spec]
    args = [y, w, x, gate]
    if gf is not None:
        in_specs.append(pl.BlockSpec((1, d), lambda bi, i: (0, 0)))
        args.append(gf.reshape(1, d))
    return pl.pallas_call(
        functools.partial(_outproj_kernel, final_norm=gf is not None),
        grid=(b, t // tm),
        in_specs=in_specs,
        out_specs=pl.BlockSpec((1, tm, d), lambda bi, i: (bi, i, 0)),
        out_shape=jax.ShapeDtypeStruct(x.shape, F32),
        compiler_params=_params(("arbitrary", "arbitrary")),
        name="outproj",
    )(*args)


def _trunk(x, mods, weights, *, per_row, tm_in, tm_out, gla_tt, cache=None, state=None):
    (norm_g, sb_w_in, sb_w_out, gla_w_in, gla_wa, gla_wa2, gla_b_a, gla_norm_g, gla_w_out,
     final_norm_g) = weights
    b, t, d = x.shape
    if per_row:
        fold = lambda a: a.reshape(1, b * t, a.shape[-1])
        rows = lambda m: jnp.broadcast_to(m[:, None, :], (b, t, d)).reshape(1, b * t, d)
    else:
        fold = lambda a: a
        rows = lambda m: m[:, None, :]
    unfold = lambda a: a.reshape(b, t, a.shape[-1])
    width = SB_HEADS * SB_HEAD_DIM

    shift, scale, gate = mods[0]
    assert width == d
    k, v, qz = _inproj(fold(x), rows(scale), rows(shift), norm_g[0], sb_w_in, (1, 2, 0, 3), 2,
                       (SB_HEAD_DIM ** -0.5 * math.log2(math.e), None), tm=tm_in)
    k, v, qz = unfold(k), unfold(v), qz.reshape(2, b, t, d)
    if cache is None:
        branch = _sb_prompt(qz, k, v, tq=min(SB_QUERY_TILE, t))
    else:
        branch = _sb_sample(qz, k, v, cache[0], cache[1])
    x1 = _outproj(fold(branch), sb_w_out, fold(x), rows(gate), None, tm=tm_out)

    shift, scale, gate = mods[1]
    nk, nv = GLA_HEADS * GLA_DK, GLA_HEADS * GLA_DV
    assert 2 * nk == d and nv == d
    qk, vr, dec = _inproj(x1, rows(scale), rows(shift), norm_g[1], gla_w_in, (0, 1, 2), 1, (None, None),
                          tm=tm_in, decay=(gla_wa, gla_wa2, gla_b_a))
    chunk = min(GLA_CHUNK, t)
    branch, s_new = _gla(unfold(qk), vr.reshape(2, b, t, d), unfold(dec), gla_norm_g, state,
                         chunk=chunk, tt=min(gla_tt, t))
    y = _outproj(fold(branch), gla_w_out, x1, rows(gate), final_norm_g, tm=tm_out)
    return unfold(y), k, v, s_new


def kernel(x_prompt, x_sample, cache_sb_k, cache_sb_v, state_gla, c_prompt, c_sample, w_ada, b_ada, norm_g,
           sb_w_in, sb_w_out, gla_w_in, gla_w_a2, gla_b_a, gla_norm_g, gla_w_out, final_norm_g):
    depth, d, _ = w_ada.shape
    assert depth == 2 and sb_w_in.shape[0] == 1 and gla_w_in.shape[0] == 1
    bp, tp, _ = x_prompt.shape
    bs, ts, _ = x_sample.shape
    past = cache_sb_k.shape[2]
    nk, nv = GLA_HEADS * GLA_DK, GLA_HEADS * GLA_DV
    main = 2 * nk + 2 * nv

    mod = _ada(jnp.concatenate([c_prompt, c_sample], axis=0), w_ada, b_ada)
    split = lambda m: (m[:, :d], m[:, d:2 * d], m[:, 2 * d:])
    mods_p = [split(mod[l, :bp]) for l in range(depth)]
    mods_s = [split(mod[l, bp:]) for l in range(depth)]

    gla_w = gla_w_in[0]
    gla_wa = jnp.pad(gla_w[:, main:], ((0, 0), (0, V7X_LANES - GLA_GATE_RANK))).astype(BF16)
    gla_wa2 = jnp.pad(gla_w_a2[0], ((0, V7X_LANES - GLA_GATE_RANK), (0, 0))).astype(BF16)
    weights = (norm_g, sb_w_in[0].astype(BF16), sb_w_out[0].astype(BF16), gla_w.astype(BF16),
               gla_wa, gla_wa2, gla_b_a[0], gla_norm_g[0], gla_w_out[0].astype(BF16), final_norm_g)

    y_p, k_p, v_p, s_p = _trunk(x_prompt, mods_p, weights, per_row=False,
                                tm_in=min(512, tp), tm_out=min(512, tp), gla_tt=1024)
    cache = (cache_sb_k[0].reshape(bs, past * SB_HEADS, SB_HEAD_DIM),
             cache_sb_v[0].reshape(bs, past * SB_HEADS, SB_HEAD_DIM))
    y_s, k_s, v_s, s_s = _trunk(x_sample, mods_s, weights, per_row=True,
                                tm_in=bs * ts, tm_out=bs * ts, gla_tt=ts, cache=cache, state=state_gla[0])

    heads = lambda a: a.reshape(1, a.shape[0], a.shape[1], SB_HEADS, SB_HEAD_DIM)
    return (y_p, y_s, heads(k_p), heads(v_p), heads(k_s), heads(v_s), s_p[None], s_s[None])
```

```python
import functools
import math

import jax
import jax.numpy as jnp
from jax import lax
from jax.experimental import pallas as pl
from jax.experimental.pallas import tpu as pltpu

F32 = jnp.float32
BF16 = jnp.bfloat16

SB_HEADS = 16
SB_HEAD_DIM = 128
GLA_HEADS = 4
GLA_DK = 256
GLA_DV = 512
GLA_GATE_RANK = 16
GLA_TAU = 16.0
GLA_CHUNK = 64
EPS = 1e-6

V7X_LANES = 128
V7X_VMEM_LIMIT_BYTES = 56 * 1024 * 1024
SB_BLOCK = 256
SB_QUERY_TILE = 4096
SB_UNDERFLOW_LOG2 = 160.0
SB_FINISHED_RUN = 1e30
NORM_ROWS = 16
NORM_UNROLL = 4


def _params(semantics):
    return pltpu.CompilerParams(dimension_semantics=semantics,
                                vmem_limit_bytes=V7X_VMEM_LIMIT_BYTES)


def _split_bf16(x):
    hi = x.astype(BF16)
    lo = (x - hi.astype(F32)).astype(BF16)
    return hi, lo


def _log_sigmoid(z):
    return jnp.minimum(z, 0.0) - jnp.log(1.0 + jnp.exp(-jnp.abs(z)))


def _softplus2(z):
    neg_abs = lax.bitcast_convert_type(lax.bitcast_convert_type(z, jnp.uint32) | jnp.uint32(1 << 31), F32)
    return jnp.maximum(z, 0.0) + jnp.log2(1.0 + jnp.exp2(neg_abs))


def _silu(z):
    return z / (1.0 + jnp.exp(-z))


def _ada_kernel(c_ref, w_ref, b_ref, o_ref):
    acc = jnp.dot(c_ref[...].astype(BF16), w_ref[0].astype(BF16), preferred_element_type=F32)
    o_ref[0] = acc + b_ref[0]


def _ada(c_all, w_ada, b_ada, tn=512):
    depth, d, n = w_ada.shape
    rows = c_all.shape[0]
    return pl.pallas_call(
        _ada_kernel,
        grid=(depth, n // tn),
        in_specs=[pl.BlockSpec((rows, d), lambda l, j: (0, 0)),
                  pl.BlockSpec((1, d, tn), lambda l, j: (l, 0, j)),
                  pl.BlockSpec((1, 1, tn), lambda l, j: (l, 0, j))],
        out_specs=pl.BlockSpec((1, rows, tn), lambda l, j: (l, 0, j)),
        out_shape=jax.ShapeDtypeStruct((depth, rows, n), F32),
        compiler_params=_params(("arbitrary", "arbitrary")),
        name="ada",
    )(c_all, w_ada, b_ada.reshape(depth, 1, n))


def _inproj_kernel(*refs, n_f32, bf16_mults, tm, steps_per_seg, with_decay):
    x_ref, sc_ref, sh_ref, g_ref, w_ref = refs[:5]
    pos = 5
    if with_decay:
        wa_ref, wa2_ref, ba_ref = refs[pos:pos + 3]
        pos += 3
    of_refs = refs[pos:pos + n_f32]
    ob_ref = refs[pos + n_f32]
    pos += n_f32 + 1
    if with_decay:
        dec_ref = refs[pos]
        pos += 1
    h_ref = refs[pos]
    n = pl.program_id(2)
    seg = n // steps_per_seg

    @pl.when(n == 0)
    def _():
        def norm_rows(r, carry):
            rows = pl.ds(pl.multiple_of(r * NORM_ROWS, NORM_ROWS), NORM_ROWS)
            x = x_ref[0, rows, :]
            y = x * lax.rsqrt(jnp.mean(x * x, axis=-1, keepdims=True) + EPS) * g_ref[...]
            if sc_ref.shape[1] == 1:
                sc, sh = sc_ref[0], sh_ref[0]
            else:
                sc, sh = sc_ref[0, rows, :], sh_ref[0, rows, :]
            h_ref[rows, :] = (y * (1.0 + sc) + sh).astype(BF16)
            return carry
        steps = tm // NORM_ROWS
        lax.fori_loop(0, steps, norm_rows, 0, unroll=NORM_UNROLL if steps % NORM_UNROLL == 0 else 1)

    for idx, of_ref in enumerate(of_refs):
        @pl.when(seg == idx)
        def _(of_ref=of_ref):
            of_ref[0] = jnp.dot(h_ref[...], w_ref[...], preferred_element_type=F32)

    @pl.when(seg >= n_f32)
    def _():
        acc = jnp.dot(h_ref[...], w_ref[...], preferred_element_type=F32)
        if any(m is not None for m in bf16_mults):
            mult = jnp.float32(1.0)
            for idx, m in enumerate(bf16_mults):
                if m is not None:
                    mult = jnp.where(seg == n_f32 + idx, jnp.float32(m), mult)
            acc = acc * mult
        ob_ref[0, 0] = acc.astype(BF16)

    if with_decay:
        @pl.when(n == pl.num_programs(2) - 1)
        def _():
            a_lr = jnp.dot(h_ref[...], wa_ref[...], preferred_element_type=F32)
            pre = jnp.dot(a_lr.astype(BF16), wa2_ref[...], preferred_element_type=F32) + ba_ref[...]
            dec_ref[0] = _log_sigmoid(pre) * (1.0 / GLA_TAU)


def _inproj(x, scale, shift, g, w, seg_order, n_f32, bf16_mults, *, tm, tn, decay=None):
    b, t, d = x.shape
    n_seg = len(seg_order)
    n_bf16 = len(bf16_mults)
    sps = d // tn
    assert t % tm == 0 and w.shape[1] >= n_seg * d and n_seg == n_f32 + n_bf16 and d == sps * tn
    assert tm % NORM_ROWS == 0

    def w_block(j):
        seg = jnp.int32(seg_order[-1])
        for k, s in enumerate(seg_order[:-1]):
            seg = jnp.where(j // sps == k, s, seg)
        return seg * sps + j % sps

    mod_rows = scale.shape[1]
    mod_spec = (pl.BlockSpec((1, 1, d), lambda bi, i, j: (bi, 0, 0)) if mod_rows == 1
                else pl.BlockSpec((1, tm, d), lambda bi, i, j: (bi, i, 0)))
    in_specs = [pl.BlockSpec((1, tm, d), lambda bi, i, j: (bi, i, 0)), mod_spec, mod_spec,
                pl.BlockSpec((1, d), lambda bi, i, j: (0, 0)),
                pl.BlockSpec((d, tn), lambda bi, i, j: (0, w_block(j)))]
    args = [x, scale, shift, g.reshape(1, d), w]
    if decay is not None:
        wa, wa2, ba = decay
        in_specs += [pl.BlockSpec(wa.shape, lambda bi, i, j: (0, 0)),
                     pl.BlockSpec(wa2.shape, lambda bi, i, j: (0, 0)),
                     pl.BlockSpec((1, ba.shape[-1]), lambda bi, i, j: (0, 0))]
        args += [wa, wa2, ba.reshape(1, -1)]

    def f32_spec(k):
        return pl.BlockSpec((1, tm, tn), lambda bi, i, j: (bi, i, jnp.clip(j - k * sps, 0, sps - 1)))

    out_specs = [f32_spec(k) for k in range(n_f32)]
    out_specs.append(pl.BlockSpec(
        (1, 1, tm, tn),
        lambda bi, i, j: (jnp.maximum(j // sps - n_f32, 0), bi, i, jnp.where(j // sps >= n_f32, j % sps, 0))))
    out_shape = [jax.ShapeDtypeStruct((b, t, d), F32) for _ in range(n_f32)]
    out_shape.append(jax.ShapeDtypeStruct((n_bf16, b, t, d), BF16))
    if decay is not None:
        nk = decay[1].shape[1]
        out_specs.append(pl.BlockSpec((1, tm, nk), lambda bi, i, j: (bi, i, 0)))
        out_shape.append(jax.ShapeDtypeStruct((b, t, nk), F32))
    return pl.pallas_call(
        functools.partial(_inproj_kernel, n_f32=n_f32, bf16_mults=tuple(bf16_mults), tm=tm,
                          steps_per_seg=sps, with_decay=decay is not None),
        grid=(b, t // tm, n_seg * sps),
        in_specs=in_specs, out_specs=out_specs, out_shape=out_shape,
        scratch_shapes=[pltpu.VMEM((tm, d), BF16)],
        compiler_params=_params(("arbitrary", "arbitrary", "arbitrary")),
        name="inproj",
    )(*args)


def _sb_blocks(qs, k_blks, v_blks, w, runs, mask=None):
    half = V7X_LANES
    zs = [lax.dot_general(q, k_blk, (((1,), (1,)), ((), ())), preferred_element_type=F32)
          for q, k_blk in zip(qs, k_blks)]
    sums = []
    for z in zs:
        sp = _softplus2(z)
        if mask is not None:
            sp = jnp.where(mask, sp, 0.0)
        hi, lo = _split_bf16(sp)
        sums.append((jnp.dot(jnp.concatenate([hi[:, half:], lo[:, half:]], axis=1), w,
                             preferred_element_type=F32),
                     jnp.dot(jnp.concatenate([hi[:, :half], lo[:, :half]], axis=1), w,
                             preferred_element_type=F32)))
    outs = []
    for z, (cs_r, cs_l), v_blk, run in zip(zs, sums, v_blks, runs):
        a_r = jnp.exp2(z[:, half:] - cs_r[:, :half] - run)
        run = run + cs_r[:, half:]
        a_l = jnp.exp2(z[:, :half] - cs_l[:, :half] - run)
        run = run + cs_l[:, half:]
        a = jnp.concatenate([a_l, a_r], axis=1)
        if mask is not None:
            a = jnp.where(mask, a, 0.0)
        outs.append((jnp.dot(a.astype(BF16), v_blk, preferred_element_type=F32), run))
    return outs


def _sb_consts():
    j = lax.broadcasted_iota(jnp.int32, (V7X_LANES, V7X_LANES), 0)
    s = lax.broadcasted_iota(jnp.int32, (V7X_LANES, V7X_LANES), 1)
    half = jnp.concatenate([(j >= s).astype(BF16), jnp.ones((V7X_LANES, V7X_LANES), BF16)], axis=1)
    return jnp.concatenate([half, half], axis=0)


def _causal_mask(rows):
    t_idx = lax.broadcasted_iota(jnp.int32, (rows, SB_BLOCK), 0)
    s_idx = lax.broadcasted_iota(jnp.int32, (rows, SB_BLOCK), 1)
    return s_idx < t_idx


def _sb_prompt_kernel(q_ref, k_ref, v_ref, zg_ref, w_ref, o_ref, acc_ref, run_ref, *, tq):
    i = pl.program_id(2)
    groups = tq // SB_BLOCK
    w = w_ref[...]

    def keys(first):
        rows = pl.ds(pl.multiple_of(first, SB_BLOCK), SB_BLOCK)
        return k_ref[0, rows, :].astype(BF16), v_ref[0, rows, :].astype(BF16)

    group_rows = [slice(g * SB_BLOCK, (g + 1) * SB_BLOCK) for g in range(groups)]
    queries = lambda: [q_ref[0, rows, :] for rows in group_rows]
    kvs = [keys((i * groups + g) * SB_BLOCK) for g in range(groups)]
    outs = _sb_blocks(queries(), [kv[0] for kv in kvs], [kv[1] for kv in kvs], w,
                      [jnp.zeros((SB_BLOCK, V7X_LANES), F32)] * groups, mask=_causal_mask(SB_BLOCK))
    for rows, (pv, run) in zip(group_rows, outs):
        acc_ref[rows, :] = pv
        run_ref[rows, :] = run

    def pending(d):
        low = jnp.float32(jnp.inf)
        for g in range(groups):
            rows = slice(g * SB_BLOCK, (g + 1) * SB_BLOCK)
            low = jnp.minimum(low, jnp.where(i * groups + g - d >= 0, jnp.min(run_ref[rows, :]), jnp.inf))
        return low

    def more(carry):
        return carry[1] < SB_UNDERFLOW_LOG2

    def diagonal(carry):
        d = carry[0]
        blks = [i * groups + g - d for g in range(groups)]
        kvs = [keys(jnp.maximum(blk, 0) * SB_BLOCK) for blk in blks]
        runs = [jnp.where(blk >= 0, run_ref[rows, :], SB_FINISHED_RUN) for blk, rows in zip(blks, group_rows)]
        outs = _sb_blocks(queries(), [kv[0] for kv in kvs], [kv[1] for kv in kvs], w, runs)
        for rows, (pv, run) in zip(group_rows, outs):
            acc_ref[rows, :] += pv
            run_ref[rows, :] = run
        return d + 1, pending(d + 1)

    lax.while_loop(more, diagonal, (jnp.int32(1), pending(1)))
    o_ref[0] = (acc_ref[...] * _silu(zg_ref[0].astype(F32))).astype(o_ref.dtype)


def _sb_prompt(qz, k, v, tq):
    _, b, t, _ = qz.shape
    assert t % tq == 0 and tq % SB_BLOCK == 0
    tile = pl.BlockSpec((1, tq, SB_HEAD_DIM), lambda bi, h, i: (bi, i, h))
    stacked = lambda which: pl.BlockSpec((None, 1, tq, SB_HEAD_DIM), lambda bi, h, i: (which, bi, i, h))
    whole = pl.BlockSpec((1, t, SB_HEAD_DIM), lambda bi, h, i: (bi, 0, h))
    return pl.pallas_call(
        functools.partial(_sb_prompt_kernel, tq=tq),
        grid=(b, SB_HEADS, t // tq),
        in_specs=[stacked(0), whole, whole, stacked(1),
                  pl.BlockSpec((SB_BLOCK, SB_BLOCK), lambda bi, h, i: (0, 0))],
        out_specs=tile,
        out_shape=jax.ShapeDtypeStruct(qz.shape[1:], BF16),
        scratch_shapes=[pltpu.VMEM((tq, SB_HEAD_DIM), F32), pltpu.VMEM((tq, V7X_LANES), F32)],
        compiler_params=_params(("arbitrary", "arbitrary", "arbitrary")),
        name="sb_prompt",
    )(qz, k, v, qz, _sb_consts())


def _sb_sample_kernel(q_ref, zg_ref, kn_ref, vn_ref, kc_hbm, vc_hbm, w_ref, o_ref,
                      kbuf, vbuf, sem, acc_ref, run_ref, *, n_blocks):
    bi = pl.program_id(0)
    t_new = q_ref.shape[1]
    block_rows = SB_BLOCK * SB_HEADS
    w = w_ref[...]

    def fetch(j, slot):
        src = pl.ds(j * block_rows, block_rows)
        return (pltpu.make_async_copy(kc_hbm.at[bi, src, :], kbuf.at[slot], sem.at[0, slot]),
                pltpu.make_async_copy(vc_hbm.at[bi, src, :], vbuf.at[slot], sem.at[1, slot]))

    for copy in fetch(n_blocks - 1, 0):
        copy.start()

    def head(h):
        return slice(h * t_new, (h + 1) * t_new), slice(h * SB_HEAD_DIM, (h + 1) * SB_HEAD_DIM)

    mask = _causal_mask(t_new)
    padding = jnp.zeros((SB_BLOCK - t_new, SB_HEAD_DIM), BF16)
    heads = [head(h) for h in range(SB_HEADS)]
    padded = lambda ref, cols: jnp.concatenate([ref[0, :, cols].astype(BF16), padding], axis=0)
    outs = _sb_blocks([q_ref[0, :, cols] for _, cols in heads], [padded(kn_ref, cols) for _, cols in heads],
                      [padded(vn_ref, cols) for _, cols in heads], w,
                      [jnp.zeros((t_new, V7X_LANES), F32)] * SB_HEADS, mask=mask)
    for (rows, _), (pv, run) in zip(heads, outs):
        acc_ref[rows, :] = pv
        run_ref[rows, :] = run

    def more(carry):
        return (carry[0] >= 0) & (carry[1] < SB_UNDERFLOW_LOG2)

    def past_block(carry):
        j = carry[0]
        slot = (n_blocks - 1 - j) % 2
        for copy in fetch(j, slot):
            copy.wait()

        @pl.when(j > 0)
        def _():
            for copy in fetch(j - 1, 1 - slot):
                copy.start()

        keys = [pl.ds(h, SB_BLOCK, stride=SB_HEADS) for h in range(SB_HEADS)]
        outs = _sb_blocks([q_ref[0, :, cols] for _, cols in heads],
                          [kbuf[slot, rows, :].astype(BF16) for rows in keys],
                          [vbuf[slot, rows, :].astype(BF16) for rows in keys], w,
                          [run_ref[rows, :] for rows, _ in heads])
        for (rows, _), (pv, run) in zip(heads, outs):
            acc_ref[rows, :] += pv
            run_ref[rows, :] = run
        return j - 1, jnp.min(run_ref[...])

    j_end, _ = lax.while_loop(more, past_block, (jnp.int32(n_blocks - 1), jnp.min(run_ref[...])))

    @pl.when(j_end >= 0)
    def _():
        for copy in fetch(j_end, (n_blocks - 1 - j_end) % 2):
            copy.wait()

    for h in range(SB_HEADS):
        rows, cols = head(h)
        o_ref[0, :, cols] = (acc_ref[rows, :] * _silu(zg_ref[0, :, cols].astype(F32))).astype(o_ref.dtype)


def _sb_sample(qz, k_new, v_new, k_past, v_past):
    _, b, t, d = qz.shape
    n_blocks = k_past.shape[1] // (SB_BLOCK * SB_HEADS)
    assert k_past.shape[1] == n_blocks * SB_BLOCK * SB_HEADS and n_blocks >= 1 and t <= V7X_LANES
    new = pl.BlockSpec((1, t, d), lambda bi: (bi, 0, 0))
    stacked = lambda which: pl.BlockSpec((None, 1, t, d), lambda bi: (which, bi, 0, 0))
    return pl.pallas_call(
        functools.partial(_sb_sample_kernel, n_blocks=n_blocks),
        grid=(b,),
        in_specs=[stacked(0), stacked(1), new, new,
                  pl.BlockSpec(memory_space=pl.ANY), pl.BlockSpec(memory_space=pl.ANY),
                  pl.BlockSpec((SB_BLOCK, SB_BLOCK), lambda bi: (0, 0))],
        out_specs=new,
        out_shape=jax.ShapeDtypeStruct(qz.shape[1:], BF16),
        scratch_shapes=[pltpu.VMEM((2, SB_BLOCK * SB_HEADS, SB_HEAD_DIM), F32),
                        pltpu.VMEM((2, SB_BLOCK * SB_HEADS, SB_HEAD_DIM), F32),
                        pltpu.SemaphoreType.DMA((2, 2)),
                        pltpu.VMEM((SB_HEADS * t, SB_HEAD_DIM), F32),
                        pltpu.VMEM((SB_HEADS * t, V7X_LANES), F32)],
        compiler_params=_params(("arbitrary",)),
        name="sb_sample",
    )(qz, qz, k_new, v_new, k_past, v_past, _sb_consts())


def _gla_kernel(*refs, chunk, n_chunks, has_state):
    if has_state:
        q_ref, k_ref, v_ref, dec_ref, r_ref, gn_ref, tril_ref, s0_ref, o_ref, sf_ref, st_ref, upd_ref = refs
    else:
        q_ref, k_ref, v_ref, dec_ref, r_ref, gn_ref, tril_ref, o_ref, sf_ref, st_ref, upd_ref = refs
    step = pl.program_id(2)

    @pl.when(step == 0)
    def _():
        st_ref[...] = s0_ref[0, 0].T if has_state else jnp.zeros_like(st_ref)

    tril2 = tril_ref[...]
    causal = (lax.broadcasted_iota(jnp.int32, (chunk, chunk), 1)
              <= lax.broadcasted_iota(jnp.int32, (chunk, chunk), 0))
    q_scale = GLA_DK ** -0.5
    nt_dot = lambda a, b: lax.dot_general(a, b, (((1,), (1,)), ((), ())), preferred_element_type=F32)
    chunks = [pl.ds(c * chunk, chunk) for c in range(n_chunks)]

    q_in, k_out, k_end, decay = [], [], [], []
    for rows in chunks:
        g_hi, g_lo = _split_bf16(dec_ref[0, rows, :])
        b = jnp.dot(tril2, jnp.concatenate([g_hi, g_lo], axis=0), preferred_element_type=F32)
        b_last = b[chunk - 1:chunk, :]
        k = k_ref[0, rows, :]
        q_in.append((q_ref[0, rows, :] * q_scale * jnp.exp(b)).astype(BF16))
        k_out.append((k * jnp.exp(-b)).astype(BF16))
        k_end.append((k * jnp.exp(b_last - b)).astype(BF16))
        decay.append(jnp.exp(b_last))
    intra = []
    for c, rows in enumerate(chunks):
        v = v_ref[0, rows, :]
        a = jnp.where(causal, nt_dot(q_in[c], k_out[c]), 0.0).astype(BF16)
        intra.append(jnp.dot(a, v, preferred_element_type=F32))
        upd_ref[c] = lax.dot_general(v, k_end[c], (((0,), (0,)), ((), ())), preferred_element_type=F32)
    for c, rows in enumerate(chunks):
        state_t = st_ref[...]
        o = intra[c] + nt_dot(q_in[c], state_t.astype(BF16))
        st_ref[...] = state_t * decay[c] + upd_ref[c]
        o = o * lax.rsqrt(jnp.mean(o * o, axis=-1, keepdims=True) + EPS) * gn_ref[...]
        o_ref[0, rows, :] = (o * _silu(r_ref[0, rows, :].astype(F32))).astype(o_ref.dtype)

    @pl.when(step == pl.num_programs(2) - 1)
    def _():
        sf_ref[0, 0] = st_ref[...].T


def _gla(qk, vr, dec, gn, s0, *, chunk, tt):
    _, b, t, _ = vr.shape
    assert t % tt == 0 and tt % chunk == 0
    tril = (lax.broadcasted_iota(jnp.int32, (chunk, chunk), 1)
            <= lax.broadcasted_iota(jnp.int32, (chunk, chunk), 0)).astype(BF16)
    tril2 = jnp.concatenate([tril, tril], axis=1)
    n_chunks = tt // chunk
    kspec = lambda off: pl.BlockSpec((1, tt, GLA_DK), lambda bi, h, s: (bi, s, h + off))
    vspec = pl.BlockSpec((1, tt, GLA_DV), lambda bi, h, s: (bi, s, h))
    sspec = pl.BlockSpec((1, 1, GLA_DK, GLA_DV), lambda bi, h, s: (bi, h, 0, 0))
    stacked = lambda which: pl.BlockSpec((None, 1, tt, GLA_DV), lambda bi, h, s: (which, bi, s, h))
    in_specs = [kspec(0), kspec(GLA_HEADS), stacked(0), kspec(0), stacked(1),
                pl.BlockSpec((1, GLA_DV), lambda bi, h, s: (0, h)),
                pl.BlockSpec((chunk, 2 * chunk), lambda bi, h, s: (0, 0))]
    args = [qk, qk, vr, dec, vr, gn.reshape(1, -1), tril2]
    if s0 is not None:
        in_specs.append(sspec)
        args.append(s0)
    return pl.pallas_call(
        functools.partial(_gla_kernel, chunk=chunk, n_chunks=n_chunks, has_state=s0 is not None),
        grid=(b, GLA_HEADS, t // tt),
        in_specs=in_specs,
        out_specs=[vspec, sspec],
        out_shape=[jax.ShapeDtypeStruct(vr.shape[1:], BF16),
                   jax.ShapeDtypeStruct((b, GLA_HEADS, GLA_DK, GLA_DV), F32)],
        scratch_shapes=[pltpu.VMEM((GLA_DV, GLA_DK), F32), pltpu.VMEM((n_chunks, GLA_DV, GLA_DK), F32)],
        compiler_params=_params(("arbitrary", "arbitrary", "arbitrary")),
        name="gla",
    )(*args)


def _outproj_kernel(y_ref, w_ref, x_ref, gate_ref, *rest, final_norm):
    if final_norm:
        gf_ref, o_ref = rest
    else:
        (o_ref,) = rest
    y = jnp.dot(y_ref[0], w_ref[...], preferred_element_type=F32)
    x = x_ref[0] + gate_ref[0] * y
    if final_norm:
        x = x * lax.rsqrt(jnp.mean(x * x, axis=-1, keepdims=True) + EPS) * gf_ref[...]
    o_ref[0] = x


def _outproj(y, w, x, gate, gf, *, tm):
    b, t, d = x.shape
    kdim = y.shape[-1]
    assert t % tm == 0
    gate_spec = (pl.BlockSpec((1, 1, d), lambda bi, i: (bi, 0, 0)) if gate.shape[1] == 1
                 else pl.BlockSpec((1, tm, d), lambda bi, i: (bi, i, 0)))
    in_specs = [pl.BlockSpec((1, tm, kdim), lambda bi, i: (bi, i, 0)),
                pl.BlockSpec((kdim, d), lambda bi, i: (0, 0)),
                pl.BlockSpec((1, tm, d), lambda bi, i: (bi, i, 0)),
                gate_spec]
    args = [y, w, x, gate]
    if gf is not None:
        in_specs.append(pl.BlockSpec((1, d), lambda bi, i: (0, 0)))
        args.append(gf.reshape(1, d))
    return pl.pallas_call(
        functools.partial(_outproj_kernel, final_norm=gf is not None),
        grid=(b, t // tm),
        in_specs=in_specs,
        out_specs=pl.BlockSpec((1, tm, d), lambda bi, i: (bi, i, 0)),
        out_shape=jax.ShapeDtypeStruct(x.shape, F32),
        compiler_params=_params(("arbitrary", "arbitrary")),
        name="outproj",
    )(*args)


def _trunk(x, mods, weights, *, per_row, tm_in, tn_in, tm_out, gla_tt, cache=None, state=None):
    (norm_g, sb_w_in, sb_w_out, gla_w_in, gla_wa, gla_wa2, gla_b_a, gla_norm_g, gla_w_out,
     final_norm_g) = weights
    b, t, d = x.shape
    if per_row:
        fold = lambda a: a.reshape(1, b * t, a.shape[-1])
        rows = lambda m: jnp.broadcast_to(m[:, None, :], (b, t, d)).reshape(1, b * t, d)
    else:
        fold = lambda a: a
        rows = lambda m: m[:, None, :]
    unfold = lambda a: a.reshape(b, t, a.shape[-1])
    width = SB_HEADS * SB_HEAD_DIM

    shift, scale, gate = mods[0]
    assert width == d
    k, v, qz = _inproj(fold(x), rows(scale), rows(shift), norm_g[0], sb_w_in, (1, 2, 0, 3), 2,
                       (SB_HEAD_DIM ** -0.5 * math.log2(math.e), None), tm=tm_in, tn=tn_in)
    k, v, qz = unfold(k), unfold(v), qz.reshape(2, b, t, d)
    if cache is None:
        branch = _sb_prompt(qz, k, v, tq=min(SB_QUERY_TILE, t))
    else:
        branch = _sb_sample(qz, k, v, cache[0], cache[1])
    x1 = _outproj(fold(branch), sb_w_out, fold(x), rows(gate), None, tm=tm_out)

    shift, scale, gate = mods[1]
    nk, nv = GLA_HEADS * GLA_DK, GLA_HEADS * GLA_DV
    assert 2 * nk == d and nv == d
    qk, vr, dec = _inproj(x1, rows(scale), rows(shift), norm_g[1], gla_w_in, (0, 1, 2), 1, (None, None),
                          tm=tm_in, tn=tn_in, decay=(gla_wa, gla_wa2, gla_b_a))
    chunk = min(GLA_CHUNK, t)
    branch, s_new = _gla(unfold(qk), vr.reshape(2, b, t, d), unfold(dec), gla_norm_g, state,
                         chunk=chunk, tt=min(gla_tt, t))
    y = _outproj(fold(branch), gla_w_out, x1, rows(gate), final_norm_g, tm=tm_out)
    return unfold(y), k, v, s_new


def kernel(x_prompt, x_sample, cache_sb_k, cache_sb_v, state_gla, c_prompt, c_sample, w_ada, b_ada, norm_g,
           sb_w_in, sb_w_out, gla_w_in, gla_w_a2, gla_b_a, gla_norm_g, gla_w_out, final_norm_g):
    depth, d, _ = w_ada.shape
    assert depth == 2 and sb_w_in.shape[0] == 1 and gla_w_in.shape[0] == 1
    bp, tp, _ = x_prompt.shape
    bs, ts, _ = x_sample.shape
    past = cache_sb_k.shape[2]
    nk, nv = GLA_HEADS * GLA_DK, GLA_HEADS * GLA_DV
    main = 2 * nk + 2 * nv

    mod = _ada(jnp.concatenate([c_prompt, c_sample], axis=0), w_ada, b_ada)
    split = lambda m: (m[:, :d], m[:, d:2 * d], m[:, 2 * d:])
    mods_p = [split(mod[l, :bp]) for l in range(depth)]
    mods_s = [split(mod[l, bp:]) for l in range(depth)]

    gla_w = gla_w_in[0]
    gla_wa = jnp.pad(gla_w[:, main:], ((0, 0), (0, V7X_LANES - GLA_GATE_RANK))).astype(BF16)
    gla_wa2 = jnp.pad(gla_w_a2[0], ((0, V7X_LANES - GLA_GATE_RANK), (0, 0))).astype(BF16)
    weights = (norm_g, sb_w_in[0].astype(BF16), sb_w_out[0].astype(BF16), gla_w.astype(BF16),
               gla_wa, gla_wa2, gla_b_a[0], gla_norm_g[0], gla_w_out[0].astype(BF16), final_norm_g)

    y_p, k_p, v_p, s_p = _trunk(x_prompt, mods_p, weights, per_row=False,
                                tm_in=min(1024, tp), tn_in=1024, tm_out=min(512, tp), gla_tt=1024)
    cache = (cache_sb_k[0].reshape(bs, past * SB_HEADS, SB_HEAD_DIM),
             cache_sb_v[0].reshape(bs, past * SB_HEADS, SB_HEAD_DIM))
    y_s, k_s, v_s, s_s = _trunk(x_sample, mods_s, weights, per_row=True,
                                tm_in=bs * ts, tn_in=d, tm_out=bs * ts, gla_tt=ts, cache=cache,
                                state=state_gla[0])

    heads = lambda a: a.reshape(1, a.shape[0], a.shape[1], SB_HEADS, SB_HEAD_DIM)
    return (y_p, y_s, heads(k_p), heads(v_p), heads(k_s), heads(v_s), s_p[None], s_s[None])
```

```python
import functools
import math

import jax
import jax.numpy as jnp
from jax import lax
from jax.experimental import pallas as pl
from jax.experimental.pallas import tpu as pltpu

F32 = jnp.float32
BF16 = jnp.bfloat16

SB_HEADS = 16
SB_HEAD_DIM = 128
GLA_HEADS = 4
GLA_DK = 256
GLA_DV = 512
GLA_GATE_RANK = 16
GLA_TAU = 16.0
GLA_CHUNK = 64
EPS = 1e-6

V7X_LANES = 128
V7X_VMEM_LIMIT_BYTES = 56 * 1024 * 1024
SB_BLOCK = 256
SB_QUERY_TILE = 4096
SB_UNDERFLOW_LOG2 = 160.0
SB_FINISHED_RUN = 1e30
NORM_ROWS = 16
NORM_UNROLL = 4


def _params(semantics):
    return pltpu.CompilerParams(dimension_semantics=semantics,
                                vmem_limit_bytes=V7X_VMEM_LIMIT_BYTES)


def _split_bf16(x):
    hi = x.astype(BF16)
    lo = (x - hi.astype(F32)).astype(BF16)
    return hi, lo


def _log_sigmoid(z):
    return jnp.minimum(z, 0.0) - jnp.log(1.0 + jnp.exp(-jnp.abs(z)))


def _softplus2(z):
    neg_abs = lax.bitcast_convert_type(lax.bitcast_convert_type(z, jnp.uint32) | jnp.uint32(1 << 31), F32)
    return jnp.maximum(z, 0.0) + jnp.log2(1.0 + jnp.exp2(neg_abs))


def _silu(z):
    return z / (1.0 + jnp.exp(-z))


def _ada_kernel(c_ref, w_ref, b_ref, o_ref):
    acc = jnp.dot(c_ref[...].astype(BF16), w_ref[0].astype(BF16), preferred_element_type=F32)
    o_ref[0] = acc + b_ref[0]


def _ada(c_all, w_ada, b_ada, tn=512):
    depth, d, n = w_ada.shape
    rows = c_all.shape[0]
    return pl.pallas_call(
        _ada_kernel,
        grid=(depth, n // tn),
        in_specs=[pl.BlockSpec((rows, d), lambda l, j: (0, 0)),
                  pl.BlockSpec((1, d, tn), lambda l, j: (l, 0, j)),
                  pl.BlockSpec((1, 1, tn), lambda l, j: (l, 0, j))],
        out_specs=pl.BlockSpec((1, rows, tn), lambda l, j: (l, 0, j)),
        out_shape=jax.ShapeDtypeStruct((depth, rows, n), F32),
        compiler_params=_params(("arbitrary", "arbitrary")),
        name="ada",
    )(c_all, w_ada, b_ada.reshape(depth, 1, n))


def _to_bf16_kernel(x_ref, o_ref):
    o_ref[...] = x_ref[...].astype(BF16)


def _to_bf16(w, cols, tn=512):
    rows = w.shape[0]
    assert cols % tn == 0 and cols <= w.shape[1]
    return pl.pallas_call(
        _to_bf16_kernel,
        grid=(cols // tn,),
        in_specs=[pl.BlockSpec((rows, tn), lambda j: (0, j))],
        out_specs=pl.BlockSpec((rows, tn), lambda j: (0, j)),
        out_shape=jax.ShapeDtypeStruct((rows, cols), BF16),
        compiler_params=_params(("arbitrary",)),
        name="to_bf16",
    )(w)


def _inproj_kernel(*refs, n_f32, bf16_mults, tm, steps_per_seg, with_decay):
    x_ref, sc_ref, sh_ref, g_ref, w_ref = refs[:5]
    pos = 5
    if with_decay:
        wa_ref, wa2_ref, ba_ref = refs[pos:pos + 3]
        pos += 3
    of_refs = refs[pos:pos + n_f32]
    ob_ref = refs[pos + n_f32]
    pos += n_f32 + 1
    if with_decay:
        dec_ref = refs[pos]
        pos += 1
    h_ref = refs[pos]
    n = pl.program_id(2)
    seg = n // steps_per_seg

    @pl.when(n == 0)
    def _():
        def norm_rows(r, carry):
            rows = pl.ds(pl.multiple_of(r * NORM_ROWS, NORM_ROWS), NORM_ROWS)
            x = x_ref[0, rows, :]
            y = x * lax.rsqrt(jnp.mean(x * x, axis=-1, keepdims=True) + EPS) * g_ref[...]
            if sc_ref.shape[1] == 1:
                sc, sh = sc_ref[0], sh_ref[0]
            else:
                sc, sh = sc_ref[0, rows, :], sh_ref[0, rows, :]
            h_ref[rows, :] = (y * (1.0 + sc) + sh).astype(BF16)
            return carry
        steps = tm // NORM_ROWS
        lax.fori_loop(0, steps, norm_rows, 0, unroll=NORM_UNROLL if steps % NORM_UNROLL == 0 else 1)

    for idx, of_ref in enumerate(of_refs):
        @pl.when(seg == idx)
        def _(of_ref=of_ref):
            of_ref[0] = jnp.dot(h_ref[...], w_ref[...], preferred_element_type=F32)

    @pl.when(seg >= n_f32)
    def _():
        acc = jnp.dot(h_ref[...], w_ref[...], preferred_element_type=F32)
        if any(m is not None for m in bf16_mults):
            mult = jnp.float32(1.0)
            for idx, m in enumerate(bf16_mults):
                if m is not None:
                    mult = jnp.where(seg == n_f32 + idx, jnp.float32(m), mult)
            acc = acc * mult
        ob_ref[0, 0] = acc.astype(BF16)

    if with_decay:
        @pl.when(n == pl.num_programs(2) - 1)
        def _():
            a_lr = jnp.dot(h_ref[...], wa_ref[...], preferred_element_type=F32)
            pre = jnp.dot(a_lr.astype(BF16), wa2_ref[...], preferred_element_type=F32) + ba_ref[...]
            dec_ref[0] = _log_sigmoid(pre) * (1.0 / GLA_TAU)


def _inproj(x, scale, shift, g, w, seg_order, n_f32, bf16_mults, *, tm, tn, decay=None):
    b, t, d = x.shape
    n_seg = len(seg_order)
    n_bf16 = len(bf16_mults)
    sps = d // tn
    assert t % tm == 0 and w.shape[1] >= n_seg * d and n_seg == n_f32 + n_bf16 and d == sps * tn
    assert tm % NORM_ROWS == 0

    def w_block(j):
        seg = jnp.int32(seg_order[-1])
        for k, s in enumerate(seg_order[:-1]):
            seg = jnp.where(j // sps == k, s, seg)
        return seg * sps + j % sps

    mod_rows = scale.shape[1]
    mod_spec = (pl.BlockSpec((1, 1, d), lambda bi, i, j: (bi, 0, 0)) if mod_rows == 1
                else pl.BlockSpec((1, tm, d), lambda bi, i, j: (bi, i, 0)))
    in_specs = [pl.BlockSpec((1, tm, d), lambda bi, i, j: (bi, i, 0)), mod_spec, mod_spec,
                pl.BlockSpec((1, d), lambda bi, i, j: (0, 0)),
                pl.BlockSpec((d, tn), lambda bi, i, j: (0, w_block(j)))]
    args = [x, scale, shift, g.reshape(1, d), w]
    if decay is not None:
        wa, wa2, ba = decay
        in_specs += [pl.BlockSpec(wa.shape, lambda bi, i, j: (0, 0)),
                     pl.BlockSpec(wa2.shape, lambda bi, i, j: (0, 0)),
                     pl.BlockSpec((1, ba.shape[-1]), lambda bi, i, j: (0, 0))]
        args += [wa, wa2, ba.reshape(1, -1)]

    def f32_spec(k):
        return pl.BlockSpec((1, tm, tn), lambda bi, i, j: (bi, i, jnp.clip(j - k * sps, 0, sps - 1)))

    out_specs = [f32_spec(k) for k in range(n_f32)]
    out_specs.append(pl.BlockSpec(
        (1, 1, tm, tn),
        lambda bi, i, j: (jnp.maximum(j // sps - n_f32, 0), bi, i, jnp.where(j // sps >= n_f32, j % sps, 0))))
    out_shape = [jax.ShapeDtypeStruct((b, t, d), F32) for _ in range(n_f32)]
    out_shape.append(jax.ShapeDtypeStruct((n_bf16, b, t, d), BF16))
    if decay is not None:
        nk = decay[1].shape[1]
        out_specs.append(pl.BlockSpec((1, tm, nk), lambda bi, i, j: (bi, i, 0)))
        out_shape.append(jax.ShapeDtypeStruct((b, t, nk), F32))
    return pl.pallas_call(
        functools.partial(_inproj_kernel, n_f32=n_f32, bf16_mults=tuple(bf16_mults), tm=tm,
                          steps_per_seg=sps, with_decay=decay is not None),
        grid=(b, t // tm, n_seg * sps),
        in_specs=in_specs, out_specs=out_specs, out_shape=out_shape,
        scratch_shapes=[pltpu.VMEM((tm, d), BF16)],
        compiler_params=_params(("arbitrary", "arbitrary", "arbitrary")),
        name="inproj",
    )(*args)


def _sb_blocks(qs, k_chains, v_chains, w, runs, mask=None, valid_chains=None):
    half = V7X_LANES
    stacked = lambda hi, lo, cols: jnp.concatenate([hi[:, cols], lo[:, cols]], axis=1)
    z_chains = [[lax.dot_general(q, k_blk, (((1,), (1,)), ((), ())), preferred_element_type=F32)
                 for k_blk in chain] for q, chain in zip(qs, k_chains)]
    sum_chains = []
    for z_chain in z_chains:
        sums = []
        for pos, z in enumerate(z_chain):
            sp = _softplus2(z)
            if mask is not None and pos == 0:
                sp = jnp.where(mask, sp, 0.0)
            hi, lo = _split_bf16(sp)
            sums.append((jnp.dot(stacked(hi, lo, slice(half, None)), w, preferred_element_type=F32),
                         jnp.dot(stacked(hi, lo, slice(None, half)), w, preferred_element_type=F32)))
        sum_chains.append(sums)
    outs = []
    for g, (z_chain, sums, v_chain, run) in enumerate(zip(z_chains, sum_chains, v_chains, runs)):
        weights = []
        for pos, (z, (cs_r, cs_l)) in enumerate(zip(z_chain, sums)):
            if valid_chains is not None and valid_chains[g][pos] is not None:
                run = jnp.where(valid_chains[g][pos], run, SB_FINISHED_RUN)
            a_r = jnp.exp2(z[:, half:] - cs_r[:, :half] - run)
            run = run + cs_r[:, half:]
            a_l = jnp.exp2(z[:, :half] - cs_l[:, :half] - run)
            run = run + cs_l[:, half:]
            a = jnp.concatenate([a_l, a_r], axis=1)
            if mask is not None and pos == 0:
                a = jnp.where(mask, a, 0.0)
            weights.append(a.astype(BF16))
        a_all = weights[0] if len(weights) == 1 else jnp.concatenate(weights, axis=1)
        v_all = v_chain[0] if len(v_chain) == 1 else jnp.concatenate(v_chain, axis=0)
        outs.append((jnp.dot(a_all, v_all, preferred_element_type=F32), run))
    return outs


def _sb_consts():
    j = lax.broadcasted_iota(jnp.int32, (V7X_LANES, V7X_LANES), 0)
    s = lax.broadcasted_iota(jnp.int32, (V7X_LANES, V7X_LANES), 1)
    half = jnp.concatenate([(j >= s).astype(BF16), jnp.ones((V7X_LANES, V7X_LANES), BF16)], axis=1)
    return jnp.concatenate([half, half], axis=0)


def _causal_mask(rows):
    t_idx = lax.broadcasted_iota(jnp.int32, (rows, SB_BLOCK), 0)
    s_idx = lax.broadcasted_iota(jnp.int32, (rows, SB_BLOCK), 1)
    return s_idx < t_idx


def _sb_prompt_kernel(q_ref, k_ref, v_ref, zg_ref, w_ref, o_ref, acc_ref, run_ref, *, tq):
    i = pl.program_id(2)
    groups = tq // SB_BLOCK
    w = w_ref[...]

    def keys(first):
        rows = pl.ds(pl.multiple_of(first, SB_BLOCK), SB_BLOCK)
        return k_ref[0, rows, :].astype(BF16), v_ref[0, rows, :].astype(BF16)

    group_rows = [slice(g * SB_BLOCK, (g + 1) * SB_BLOCK) for g in range(groups)]
    queries = lambda: [q_ref[0, rows, :] for rows in group_rows]
    blks = [i * groups + g for g in range(groups)]
    kvs = [(keys(blk * SB_BLOCK), keys(jnp.maximum(blk - 1, 0) * SB_BLOCK)) for blk in blks]
    outs = _sb_blocks(queries(), [[kv[0][0], kv[1][0]] for kv in kvs], [[kv[0][1], kv[1][1]] for kv in kvs], w,
                      [jnp.zeros((SB_BLOCK, V7X_LANES), F32)] * groups, mask=_causal_mask(SB_BLOCK),
                      valid_chains=[[None, blk >= 1] for blk in blks])
    for rows, (pv, run) in zip(group_rows, outs):
        acc_ref[rows, :] = pv
        run_ref[rows, :] = run

    def pending(d):
        low = jnp.float32(jnp.inf)
        for g in range(groups):
            rows = slice(g * SB_BLOCK, (g + 1) * SB_BLOCK)
            low = jnp.minimum(low, jnp.where(i * groups + g - d >= 0, jnp.min(run_ref[rows, :]), jnp.inf))
        return low

    def more(carry):
        return carry[1] < SB_UNDERFLOW_LOG2

    def diagonal(carry):
        d = carry[0]
        blks = [i * groups + g - d for g in range(groups)]
        kvs = [keys(jnp.maximum(blk, 0) * SB_BLOCK) for blk in blks]
        outs = _sb_blocks(queries(), [[kv[0]] for kv in kvs], [[kv[1]] for kv in kvs], w,
                          [run_ref[rows, :] for rows in group_rows],
                          valid_chains=[[blk >= 0] for blk in blks])
        for rows, (pv, run) in zip(group_rows, outs):
            acc_ref[rows, :] += pv
            run_ref[rows, :] = run
        return d + 1, pending(d + 1)

    lax.while_loop(more, diagonal, (jnp.int32(2), pending(2)))
    o_ref[0] = (acc_ref[...] * _silu(zg_ref[0].astype(F32))).astype(o_ref.dtype)


def _sb_prompt(qz, k, v, tq):
    _, b, t, _ = qz.shape
    assert t % tq == 0 and tq % SB_BLOCK == 0
    tile = pl.BlockSpec((1, tq, SB_HEAD_DIM), lambda bi, h, i: (bi, i, h))
    stacked = lambda which: pl.BlockSpec((None, 1, tq, SB_HEAD_DIM), lambda bi, h, i: (which, bi, i, h))
    whole = pl.BlockSpec((1, t, SB_HEAD_DIM), lambda bi, h, i: (bi, 0, h))
    return pl.pallas_call(
        functools.partial(_sb_prompt_kernel, tq=tq),
        grid=(b, SB_HEADS, t // tq),
        in_specs=[stacked(0), whole, whole, stacked(1),
                  pl.BlockSpec((SB_BLOCK, SB_BLOCK), lambda bi, h, i: (0, 0))],
        out_specs=tile,
        out_shape=jax.ShapeDtypeStruct(qz.shape[1:], BF16),
        scratch_shapes=[pltpu.VMEM((tq, SB_HEAD_DIM), F32), pltpu.VMEM((tq, V7X_LANES), F32)],
        compiler_params=_params(("arbitrary", "arbitrary", "arbitrary")),
        name="sb_prompt",
    )(qz, k, v, qz, _sb_consts())


def _sb_sample_kernel(q_ref, zg_ref, kn_ref, vn_ref, kc_hbm, vc_hbm, w_ref, o_ref,
                      kbuf, vbuf, sem, acc_ref, run_ref, *, n_blocks):
    bi = pl.program_id(0)
    t_new = q_ref.shape[1]
    block_rows = SB_BLOCK * SB_HEADS
    w = w_ref[...]

    def fetch(j, slot):
        src = pl.ds(j * block_rows, block_rows)
        return (pltpu.make_async_copy(kc_hbm.at[bi, src, :], kbuf.at[slot], sem.at[0, slot]),
                pltpu.make_async_copy(vc_hbm.at[bi, src, :], vbuf.at[slot], sem.at[1, slot]))

    for copy in fetch(n_blocks - 1, 0):
        copy.start()

    def head(h):
        return slice(h * t_new, (h + 1) * t_new), slice(h * SB_HEAD_DIM, (h + 1) * SB_HEAD_DIM)

    mask = _causal_mask(t_new)
    padding = jnp.zeros((SB_BLOCK - t_new, SB_HEAD_DIM), BF16)
    heads = [head(h) for h in range(SB_HEADS)]
    padded = lambda ref, cols: jnp.concatenate([ref[0, :, cols].astype(BF16), padding], axis=0)
    outs = _sb_blocks([q_ref[0, :, cols] for _, cols in heads], [[padded(kn_ref, cols)] for _, cols in heads],
                      [[padded(vn_ref, cols)] for _, cols in heads], w,
                      [jnp.zeros((t_new, V7X_LANES), F32)] * SB_HEADS, mask=mask)
    for (rows, _), (pv, run) in zip(heads, outs):
        acc_ref[rows, :] = pv
        run_ref[rows, :] = run

    def more(carry):
        return (carry[0] >= 0) & (carry[1] < SB_UNDERFLOW_LOG2)

    def past_block(carry):
        j = carry[0]
        slot = (n_blocks - 1 - j) % 2
        for copy in fetch(j, slot):
            copy.wait()

        @pl.when(j > 0)
        def _():
            for copy in fetch(j - 1, 1 - slot):
                copy.start()

        keys = [pl.ds(h, SB_BLOCK, stride=SB_HEADS) for h in range(SB_HEADS)]
        outs = _sb_blocks([q_ref[0, :, cols] for _, cols in heads],
                          [[kbuf[slot, rows, :].astype(BF16)] for rows in keys],
                          [[vbuf[slot, rows, :].astype(BF16)] for rows in keys], w,
                          [run_ref[rows, :] for rows, _ in heads])
        for (rows, _), (pv, run) in zip(heads, outs):
            acc_ref[rows, :] += pv
            run_ref[rows, :] = run
        return j - 1, jnp.min(run_ref[...])

    j_end, _ = lax.while_loop(more, past_block, (jnp.int32(n_blocks - 1), jnp.min(run_ref[...])))

    @pl.when(j_end >= 0)
    def _():
        for copy in fetch(j_end, (n_blocks - 1 - j_end) % 2):
            copy.wait()

    for h in range(SB_HEADS):
        rows, cols = head(h)
        o_ref[0, :, cols] = (acc_ref[rows, :] * _silu(zg_ref[0, :, cols].astype(F32))).astype(o_ref.dtype)


def _sb_sample(qz, k_new, v_new, k_past, v_past):
    _, b, t, d = qz.shape
    n_blocks = k_past.shape[1] // (SB_BLOCK * SB_HEADS)
    assert k_past.shape[1] == n_blocks * SB_BLOCK * SB_HEADS and n_blocks >= 1 and t <= V7X_LANES
    new = pl.BlockSpec((1, t, d), lambda bi: (bi, 0, 0))
    stacked = lambda which: pl.BlockSpec((None, 1, t, d), lambda bi: (which, bi, 0, 0))
    return pl.pallas_call(
        functools.partial(_sb_sample_kernel, n_blocks=n_blocks),
        grid=(b,),
        in_specs=[stacked(0), stacked(1), new, new,
                  pl.BlockSpec(memory_space=pl.ANY), pl.BlockSpec(memory_space=pl.ANY),
                  pl.BlockSpec((SB_BLOCK, SB_BLOCK), lambda bi: (0, 0))],
        out_specs=new,
        out_shape=jax.ShapeDtypeStruct(qz.shape[1:], BF16),
        scratch_shapes=[pltpu.VMEM((2, SB_BLOCK * SB_HEADS, SB_HEAD_DIM), F32),
                        pltpu.VMEM((2, SB_BLOCK * SB_HEADS, SB_HEAD_DIM), F32),
                        pltpu.SemaphoreType.DMA((2, 2)),
                        pltpu.VMEM((SB_HEADS * t, SB_HEAD_DIM), F32),
                        pltpu.VMEM((SB_HEADS * t, V7X_LANES), F32)],
        compiler_params=_params(("arbitrary",)),
        name="sb_sample",
    )(qz, qz, k_new, v_new, k_past, v_past, _sb_consts())


def _gla_kernel(*refs, chunk, n_chunks, has_state):
    if has_state:
        q_ref, k_ref, v_ref, dec_ref, r_ref, gn_ref, tril_ref, s0_ref, o_ref, sf_ref, st_ref, upd_ref = refs
    else:
        q_ref, k_ref, v_ref, dec_ref, r_ref, gn_ref, tril_ref, o_ref, sf_ref, st_ref, upd_ref = refs
    step = pl.program_id(2)

    @pl.when(step == 0)
    def _():
        st_ref[...] = s0_ref[0, 0].T if has_state else jnp.zeros_like(st_ref)

    tril2 = tril_ref[...]
    causal = (lax.broadcasted_iota(jnp.int32, (chunk, chunk), 1)
              <= lax.broadcasted_iota(jnp.int32, (chunk, chunk), 0))
    q_scale = GLA_DK ** -0.5
    nt_dot = lambda a, b: lax.dot_general(a, b, (((1,), (1,)), ((), ())), preferred_element_type=F32)
    chunks = [pl.ds(c * chunk, chunk) for c in range(n_chunks)]

    q_in, k_out, k_end, decay = [], [], [], []
    for rows in chunks:
        g_hi, g_lo = _split_bf16(dec_ref[0, rows, :])
        b = jnp.dot(tril2, jnp.concatenate([g_hi, g_lo], axis=0), preferred_element_type=F32)
        b_last = b[chunk - 1:chunk, :]
        k = k_ref[0, rows, :]
        q_in.append((q_ref[0, rows, :] * q_scale * jnp.exp(b)).astype(BF16))
        k_out.append((k * jnp.exp(-b)).astype(BF16))
        k_end.append((k * jnp.exp(b_last - b)).astype(BF16))
        decay.append(jnp.exp(b_last))
    intra = []
    for c, rows in enumerate(chunks):
        v = v_ref[0, rows, :]
        a = jnp.where(causal, nt_dot(q_in[c], k_out[c]), 0.0).astype(BF16)
        intra.append(jnp.dot(a, v, preferred_element_type=F32))
        upd_ref[c] = lax.dot_general(v, k_end[c], (((0,), (0,)), ((), ())), preferred_element_type=F32)
    for c, rows in enumerate(chunks):
        state_t = st_ref[...]
        o = intra[c] + nt_dot(q_in[c], state_t.astype(BF16))
        st_ref[...] = state_t * decay[c] + upd_ref[c]
        o = o * lax.rsqrt(jnp.mean(o * o, axis=-1, keepdims=True) + EPS) * gn_ref[...]
        o_ref[0, rows, :] = (o * _silu(r_ref[0, rows, :].astype(F32))).astype(o_ref.dtype)

    @pl.when(step == pl.num_programs(2) - 1)
    def _():
        sf_ref[0, 0] = st_ref[...].T


def _gla(qk, vr, dec, gn, s0, *, chunk, tt):
    _, b, t, _ = vr.shape
    assert t % tt == 0 and tt % chunk == 0
    tril = (lax.broadcasted_iota(jnp.int32, (chunk, chunk), 1)
            <= lax.broadcasted_iota(jnp.int32, (chunk, chunk), 0)).astype(BF16)
    tril2 = jnp.concatenate([tril, tril], axis=1)
    n_chunks = tt // chunk
    kspec = lambda off: pl.BlockSpec((1, tt, GLA_DK), lambda bi, h, s: (bi, s, h + off))
    vspec = pl.BlockSpec((1, tt, GLA_DV), lambda bi, h, s: (bi, s, h))
    sspec = pl.BlockSpec((1, 1, GLA_DK, GLA_DV), lambda bi, h, s: (bi, h, 0, 0))
    stacked = lambda which: pl.BlockSpec((None, 1, tt, GLA_DV), lambda bi, h, s: (which, bi, s, h))
    in_specs = [kspec(0), kspec(GLA_HEADS), stacked(0), kspec(0), stacked(1),
                pl.BlockSpec((1, GLA_DV), lambda bi, h, s: (0, h)),
                pl.BlockSpec((chunk, 2 * chunk), lambda bi, h, s: (0, 0))]
    args = [qk, qk, vr, dec, vr, gn.reshape(1, -1), tril2]
    if s0 is not None:
        in_specs.append(sspec)
        args.append(s0)
    return pl.pallas_call(
        functools.partial(_gla_kernel, chunk=chunk, n_chunks=n_chunks, has_state=s0 is not None),
        grid=(b, GLA_HEADS, t // tt),
        in_specs=in_specs,
        out_specs=[vspec, sspec],
        out_shape=[jax.ShapeDtypeStruct(vr.shape[1:], BF16),
                   jax.ShapeDtypeStruct((b, GLA_HEADS, GLA_DK, GLA_DV), F32)],
        scratch_shapes=[pltpu.VMEM((GLA_DV, GLA_DK), F32), pltpu.VMEM((n_chunks, GLA_DV, GLA_DK), F32)],
        compiler_params=_params(("arbitrary", "arbitrary", "arbitrary")),
        name="gla",
    )(*args)


def _outproj_kernel(y_ref, w_ref, x_ref, gate_ref, *rest, final_norm):
    if final_norm:
        gf_ref, o_ref = rest
    else:
        (o_ref,) = rest
    y = jnp.dot(y_ref[0], w_ref[...], preferred_element_type=F32)
    x = x_ref[0] + gate_ref[0] * y
    if final_norm:
        x = x * lax.rsqrt(jnp.mean(x * x, axis=-1, keepdims=True) + EPS) * gf_ref[...]
    o_ref[0] = x


def _outproj(y, w, x, gate, gf, *, tm):
    b, t, d = x.shape
    kdim = y.shape[-1]
    assert t % tm == 0
    gate_spec = (pl.BlockSpec((1, 1, d), lambda bi, i: (bi, 0, 0)) if gate.shape[1] == 1
                 else pl.BlockSpec((1, tm, d), lambda bi, i: (bi, i, 0)))
    in_specs = [pl.BlockSpec((1, tm, kdim), lambda bi, i: (bi, i, 0)),
                pl.BlockSpec((kdim, d), lambda bi, i: (0, 0)),
                pl.BlockSpec((1, tm, d), lambda bi, i: (bi, i, 0)),
                gate_spec]
    args = [y, w, x, gate]
    if gf is not None:
        in_specs.append(pl.BlockSpec((1, d), lambda bi, i: (0, 0)))
        args.append(gf.reshape(1, d))
    return pl.pallas_call(
        functools.partial(_outproj_kernel, final_norm=gf is not None),
        grid=(b, t // tm),
        in_specs=in_specs,
        out_specs=pl.BlockSpec((1, tm, d), lambda bi, i: (bi, i, 0)),
        out_shape=jax.ShapeDtypeStruct(x.shape, F32),
        compiler_params=_params(("arbitrary", "arbitrary")),
        name="outproj",
    )(*args)


def _trunk(x, mods, weights, *, per_row, tm_in, tn_in, tm_out, gla_tt, cache=None, state=None):
    (norm_g, sb_w_in, sb_w_out, gla_w_in, gla_wa, gla_wa2, gla_b_a, gla_norm_g, gla_w_out,
     final_norm_g) = weights
    b, t, d = x.shape
    if per_row:
        fold = lambda a: a.reshape(1, b * t, a.shape[-1])
        rows = lambda m: jnp.broadcast_to(m[:, None, :], (b, t, d)).reshape(1, b * t, d)
    else:
        fold = lambda a: a
        rows = lambda m: m[:, None, :]
    unfold = lambda a: a.reshape(b, t, a.shape[-1])
    width = SB_HEADS * SB_HEAD_DIM

    shift, scale, gate = mods[0]
    assert width == d
    k, v, qz = _inproj(fold(x), rows(scale), rows(shift), norm_g[0], sb_w_in, (1, 2, 0, 3), 2,
                       (SB_HEAD_DIM ** -0.5 * math.log2(math.e), None), tm=tm_in, tn=tn_in)
    k, v, qz = unfold(k), unfold(v), qz.reshape(2, b, t, d)
    if cache is None:
        branch = _sb_prompt(qz, k, v, tq=min(SB_QUERY_TILE, t))
    else:
        branch = _sb_sample(qz, k, v, cache[0], cache[1])
    x1 = _outproj(fold(branch), sb_w_out, fold(x), rows(gate), None, tm=tm_out)

    shift, scale, gate = mods[1]
    nk, nv = GLA_HEADS * GLA_DK, GLA_HEADS * GLA_DV
    assert 2 * nk == d and nv == d
    qk, vr, dec = _inproj(x1, rows(scale), rows(shift), norm_g[1], gla_w_in, (0, 1, 2), 1, (None, None),
                          tm=tm_in, tn=tn_in, decay=(gla_wa, gla_wa2, gla_b_a))
    chunk = min(GLA_CHUNK, t)
    branch, s_new = _gla(unfold(qk), vr.reshape(2, b, t, d), unfold(dec), gla_norm_g, state,
                         chunk=chunk, tt=min(gla_tt, t))
    y = _outproj(fold(branch), gla_w_out, x1, rows(gate), final_norm_g, tm=tm_out)
    return unfold(y), k, v, s_new


def kernel(x_prompt, x_sample, cache_sb_k, cache_sb_v, state_gla, c_prompt, c_sample, w_ada, b_ada, norm_g,
           sb_w_in, sb_w_out, gla_w_in, gla_w_a2, gla_b_a, gla_norm_g, gla_w_out, final_norm_g):
    depth, d, _ = w_ada.shape
    assert depth == 2 and sb_w_in.shape[0] == 1 and gla_w_in.shape[0] == 1
    bp, tp, _ = x_prompt.shape
    bs, ts, _ = x_sample.shape
    past = cache_sb_k.shape[2]
    nk, nv = GLA_HEADS * GLA_DK, GLA_HEADS * GLA_DV
    main = 2 * nk + 2 * nv

    mod = _ada(jnp.concatenate([c_prompt, c_sample], axis=0), w_ada, b_ada)
    split = lambda m: (m[:, :d], m[:, d:2 * d], m[:, 2 * d:])
    mods_p = [split(mod[l, :bp]) for l in range(depth)]
    mods_s = [split(mod[l, bp:]) for l in range(depth)]

    gla_w = gla_w_in[0]
    gla_wa = jnp.pad(gla_w[:, main:], ((0, 0), (0, V7X_LANES - GLA_GATE_RANK))).astype(BF16)
    gla_wa2 = jnp.pad(gla_w_a2[0], ((0, V7X_LANES - GLA_GATE_RANK), (0, 0))).astype(BF16)
    weights = (norm_g, sb_w_in[0].astype(BF16), sb_w_out[0].astype(BF16), _to_bf16(gla_w, main),
               gla_wa, gla_wa2, gla_b_a[0], gla_norm_g[0], gla_w_out[0].astype(BF16), final_norm_g)

    y_p, k_p, v_p, s_p = _trunk(x_prompt, mods_p, weights, per_row=False,
                                tm_in=min(1024, tp), tn_in=1024, tm_out=min(512, tp), gla_tt=1024)
    cache = (cache_sb_k[0].reshape(bs, past * SB_HEADS, SB_HEAD_DIM),
             cache_sb_v[0].reshape(bs, past * SB_HEADS, SB_HEAD_DIM))
    y_s, k_s, v_s, s_s = _trunk(x_sample, mods_s, weights, per_row=True,
                                tm_in=bs * ts, tn_in=d, tm_out=bs * ts, gla_tt=ts, cache=cache,
                                state=state_gla[0])

    heads = lambda a: a.reshape(1, a.shape[0], a.shape[1], SB_HEADS, SB_HEAD_DIM)
    return (y_p, y_s, heads(k_p), heads(v_p), heads(k_s), heads(v_s), s_p[None], s_s[None])
```

```python
import functools
import math

import jax
import jax.numpy as jnp
from jax import lax
from jax.experimental import pallas as pl
from jax.experimental.pallas import tpu as pltpu

F32 = jnp.float32
BF16 = jnp.bfloat16

SB_HEADS = 16
SB_HEAD_DIM = 128
GLA_HEADS = 4
GLA_DK = 256
GLA_DV = 512
GLA_GATE_RANK = 16
GLA_TAU = 16.0
GLA_CHUNK = 64
EPS = 1e-6

V7X_LANES = 128
V7X_VMEM_LIMIT_BYTES = 56 * 1024 * 1024
SB_BLOCK = 256
SB_QUERY_TILE = 4096
SB_UNDERFLOW_LOG2 = 160.0
SB_FINISHED_RUN = 1e30
NORM_ROWS = 16
NORM_UNROLL = 4


def _params(semantics):
    return pltpu.CompilerParams(dimension_semantics=semantics,
                                vmem_limit_bytes=V7X_VMEM_LIMIT_BYTES)


def _split_bf16(x):
    hi = x.astype(BF16)
    lo = (x - hi.astype(F32)).astype(BF16)
    return hi, lo


def _log_sigmoid(z):
    return jnp.minimum(z, 0.0) - jnp.log(1.0 + jnp.exp(-jnp.abs(z)))


def _softplus2(z):
    neg_abs = lax.bitcast_convert_type(lax.bitcast_convert_type(z, jnp.uint32) | jnp.uint32(1 << 31), F32)
    return jnp.maximum(z, 0.0) + jnp.log2(1.0 + jnp.exp2(neg_abs))


def _silu(z):
    return z / (1.0 + jnp.exp(-z))


def _ada_kernel(c_ref, w_ref, b_ref, o_ref):
    acc = jnp.dot(c_ref[...].astype(BF16), w_ref[0].astype(BF16), preferred_element_type=F32)
    o_ref[0] = acc + b_ref[0]


def _ada(c_all, w_ada, b_ada, tn=512):
    depth, d, n = w_ada.shape
    rows = c_all.shape[0]
    return pl.pallas_call(
        _ada_kernel,
        grid=(depth, n // tn),
        in_specs=[pl.BlockSpec((rows, d), lambda l, j: (0, 0)),
                  pl.BlockSpec((1, d, tn), lambda l, j: (l, 0, j)),
                  pl.BlockSpec((1, 1, tn), lambda l, j: (l, 0, j))],
        out_specs=pl.BlockSpec((1, rows, tn), lambda l, j: (l, 0, j)),
        out_shape=jax.ShapeDtypeStruct((depth, rows, n), F32),
        compiler_params=_params(("arbitrary", "arbitrary")),
        name="ada",
    )(c_all, w_ada, b_ada.reshape(depth, 1, n))


def _inproj_kernel(*refs, n_f32, bf16_mults, tm, steps_per_seg, with_decay):
    x_ref, sc_ref, sh_ref, g_ref, w_ref = refs[:5]
    pos = 5
    if with_decay:
        wa_ref, wa2_ref, ba_ref = refs[pos:pos + 3]
        pos += 3
    of_refs = refs[pos:pos + n_f32]
    ob_ref = refs[pos + n_f32]
    pos += n_f32 + 1
    if with_decay:
        dec_ref = refs[pos]
        pos += 1
    h_ref = refs[pos]
    n = pl.program_id(2)
    seg = n // steps_per_seg

    @pl.when(n == 0)
    def _():
        def norm_rows(r, carry):
            rows = pl.ds(pl.multiple_of(r * NORM_ROWS, NORM_ROWS), NORM_ROWS)
            x = x_ref[0, rows, :]
            y = x * lax.rsqrt(jnp.mean(x * x, axis=-1, keepdims=True) + EPS) * g_ref[...]
            if sc_ref.shape[1] == 1:
                sc, sh = sc_ref[0], sh_ref[0]
            else:
                sc, sh = sc_ref[0, rows, :], sh_ref[0, rows, :]
            h_ref[rows, :] = (y * (1.0 + sc) + sh).astype(BF16)
            return carry
        steps = tm // NORM_ROWS
        lax.fori_loop(0, steps, norm_rows, 0, unroll=NORM_UNROLL if steps % NORM_UNROLL == 0 else 1)

    for idx, of_ref in enumerate(of_refs):
        @pl.when(seg == idx)
        def _(of_ref=of_ref):
            of_ref[0] = jnp.dot(h_ref[...], w_ref[...], preferred_element_type=F32)

    @pl.when(seg >= n_f32)
    def _():
        acc = jnp.dot(h_ref[...], w_ref[...], preferred_element_type=F32)
        if any(m is not None for m in bf16_mults):
            mult = jnp.float32(1.0)
            for idx, m in enumerate(bf16_mults):
                if m is not None:
                    mult = jnp.where(seg == n_f32 + idx, jnp.float32(m), mult)
            acc = acc * mult
        ob_ref[0, 0] = acc.astype(BF16)

    if with_decay:
        @pl.when(n == pl.num_programs(2) - 1)
        def _():
            a_lr = jnp.dot(h_ref[...], wa_ref[...], preferred_element_type=F32)
            pre = jnp.dot(a_lr.astype(BF16), wa2_ref[...], preferred_element_type=F32) + ba_ref[...]
            dec_ref[0] = _log_sigmoid(pre) * (1.0 / GLA_TAU)


def _inproj(x, scale, shift, g, w, seg_order, n_f32, bf16_mults, *, tm, tn, decay=None):
    b, t, d = x.shape
    n_seg = len(seg_order)
    n_bf16 = len(bf16_mults)
    sps = d // tn
    assert t % tm == 0 and w.shape[1] >= n_seg * d and n_seg == n_f32 + n_bf16 and d == sps * tn
    assert tm % NORM_ROWS == 0

    def w_block(j):
        seg = jnp.int32(seg_order[-1])
        for k, s in enumerate(seg_order[:-1]):
            seg = jnp.where(j // sps == k, s, seg)
        return seg * sps + j % sps

    mod_rows = scale.shape[1]
    mod_spec = (pl.BlockSpec((1, 1, d), lambda bi, i, j: (bi, 0, 0)) if mod_rows == 1
                else pl.BlockSpec((1, tm, d), lambda bi, i, j: (bi, i, 0)))
    in_specs = [pl.BlockSpec((1, tm, d), lambda bi, i, j: (bi, i, 0)), mod_spec, mod_spec,
                pl.BlockSpec((1, d), lambda bi, i, j: (0, 0)),
                pl.BlockSpec((d, tn), lambda bi, i, j: (0, w_block(j)))]
    args = [x, scale, shift, g.reshape(1, d), w]
    if decay is not None:
        wa, wa2, ba = decay
        in_specs += [pl.BlockSpec(wa.shape, lambda bi, i, j: (0, 0)),
                     pl.BlockSpec(wa2.shape, lambda bi, i, j: (0, 0)),
                     pl.BlockSpec((1, ba.shape[-1]), lambda bi, i, j: (0, 0))]
        args += [wa, wa2, ba.reshape(1, -1)]

    def f32_spec(k):
        return pl.BlockSpec((1, tm, tn), lambda bi, i, j: (bi, i, jnp.clip(j - k * sps, 0, sps - 1)))

    out_specs = [f32_spec(k) for k in range(n_f32)]
    out_specs.append(pl.BlockSpec(
        (1, 1, tm, tn),
        lambda bi, i, j: (jnp.maximum(j // sps - n_f32, 0), bi, i, jnp.where(j // sps >= n_f32, j % sps, 0))))
    out_shape = [jax.ShapeDtypeStruct((b, t, d), F32) for _ in range(n_f32)]
    out_shape.append(jax.ShapeDtypeStruct((n_bf16, b, t, d), BF16))
    if decay is not None:
        nk = decay[1].shape[1]
        out_specs.append(pl.BlockSpec((1, tm, nk), lambda bi, i, j: (bi, i, 0)))
        out_shape.append(jax.ShapeDtypeStruct((b, t, nk), F32))
    return pl.pallas_call(
        functools.partial(_inproj_kernel, n_f32=n_f32, bf16_mults=tuple(bf16_mults), tm=tm,
                          steps_per_seg=sps, with_decay=decay is not None),
        grid=(b, t // tm, n_seg * sps),
        in_specs=in_specs, out_specs=out_specs, out_shape=out_shape,
        scratch_shapes=[pltpu.VMEM((tm, d), BF16)],
        compiler_params=_params(("arbitrary", "arbitrary", "arbitrary")),
        name="inproj",
    )(*args)


def _sb_blocks(qs, k_chains, v_chains, w, runs, mask=None, valid_chains=None):
    half = V7X_LANES
    stacked = lambda hi, lo, cols: jnp.concatenate([hi[:, cols], lo[:, cols]], axis=1)
    z_chains = [[lax.dot_general(q, k_blk, (((1,), (1,)), ((), ())), preferred_element_type=F32)
                 for k_blk in chain] for q, chain in zip(qs, k_chains)]
    sum_chains = []
    for z_chain in z_chains:
        sums = []
        for pos, z in enumerate(z_chain):
            sp = _softplus2(z)
            if mask is not None and pos == 0:
                sp = jnp.where(mask, sp, 0.0)
            hi, lo = _split_bf16(sp)
            sums.append((jnp.dot(stacked(hi, lo, slice(half, None)), w, preferred_element_type=F32),
                         jnp.dot(stacked(hi, lo, slice(None, half)), w, preferred_element_type=F32)))
        sum_chains.append(sums)
    outs = []
    for g, (z_chain, sums, v_chain, run) in enumerate(zip(z_chains, sum_chains, v_chains, runs)):
        weights = []
        for pos, (z, (cs_r, cs_l)) in enumerate(zip(z_chain, sums)):
            if valid_chains is not None and valid_chains[g][pos] is not None:
                run = jnp.where(valid_chains[g][pos], run, SB_FINISHED_RUN)
            a_r = jnp.exp2(z[:, half:] - cs_r[:, :half] - run)
            run = run + cs_r[:, half:]
            a_l = jnp.exp2(z[:, :half] - cs_l[:, :half] - run)
            run = run + cs_l[:, half:]
            a = jnp.concatenate([a_l, a_r], axis=1)
            if mask is not None and pos == 0:
                a = jnp.where(mask, a, 0.0)
            weights.append(a.astype(BF16))
        a_all = weights[0] if len(weights) == 1 else jnp.concatenate(weights, axis=1)
        v_all = v_chain[0] if len(v_chain) == 1 else jnp.concatenate(v_chain, axis=0)
        outs.append((jnp.dot(a_all, v_all, preferred_element_type=F32), run))
    return outs


def _sb_consts():
    j = lax.broadcasted_iota(jnp.int32, (V7X_LANES, V7X_LANES), 0)
    s = lax.broadcasted_iota(jnp.int32, (V7X_LANES, V7X_LANES), 1)
    half = jnp.concatenate([(j >= s).astype(BF16), jnp.ones((V7X_LANES, V7X_LANES), BF16)], axis=1)
    return jnp.concatenate([half, half], axis=0)


def _causal_mask(rows):
    t_idx = lax.broadcasted_iota(jnp.int32, (rows, SB_BLOCK), 0)
    s_idx = lax.broadcasted_iota(jnp.int32, (rows, SB_BLOCK), 1)
    return s_idx < t_idx


def _sb_prompt_kernel(q_ref, k_ref, v_ref, zg_ref, w_ref, o_ref, acc_ref, run_ref, *, tq):
    i = pl.program_id(2)
    groups = tq // SB_BLOCK
    w = w_ref[...]

    def keys(first):
        rows = pl.ds(pl.multiple_of(first, SB_BLOCK), SB_BLOCK)
        return k_ref[0, rows, :].astype(BF16), v_ref[0, rows, :].astype(BF16)

    group_rows = [slice(g * SB_BLOCK, (g + 1) * SB_BLOCK) for g in range(groups)]
    queries = lambda: [q_ref[0, rows, :] for rows in group_rows]
    blks = [i * groups + g for g in range(groups)]
    kvs = [(keys(blk * SB_BLOCK), keys(jnp.maximum(blk - 1, 0) * SB_BLOCK)) for blk in blks]
    outs = _sb_blocks(queries(), [[kv[0][0], kv[1][0]] for kv in kvs], [[kv[0][1], kv[1][1]] for kv in kvs], w,
                      [jnp.zeros((SB_BLOCK, V7X_LANES), F32)] * groups, mask=_causal_mask(SB_BLOCK),
                      valid_chains=[[None, blk >= 1] for blk in blks])
    for rows, (pv, run) in zip(group_rows, outs):
        acc_ref[rows, :] = pv
        run_ref[rows, :] = run

    def pending(d):
        low = jnp.full((SB_BLOCK, V7X_LANES), jnp.inf, F32)
        for g, rows in enumerate(group_rows):
            low = jnp.minimum(low, jnp.where(i * groups + g - d >= 0, run_ref[rows, :], jnp.inf))
        return jnp.min(low)

    def more(carry):
        return carry[1] < SB_UNDERFLOW_LOG2

    def diagonal(carry):
        d = carry[0]
        blks = [i * groups + g - d for g in range(groups)]
        kvs = [keys(jnp.maximum(blk, 0) * SB_BLOCK) for blk in blks]
        outs = _sb_blocks(queries(), [[kv[0]] for kv in kvs], [[kv[1]] for kv in kvs], w,
                          [run_ref[rows, :] for rows in group_rows],
                          valid_chains=[[blk >= 0] for blk in blks])
        for rows, (pv, run) in zip(group_rows, outs):
            acc_ref[rows, :] += pv
            run_ref[rows, :] = run
        return d + 1, pending(d + 1)

    lax.while_loop(more, diagonal, (jnp.int32(2), pending(2)))
    o_ref[0] = (acc_ref[...] * _silu(zg_ref[0].astype(F32))).astype(o_ref.dtype)


def _sb_prompt(qz, k, v, tq):
    _, b, t, _ = qz.shape
    assert t % tq == 0 and tq % SB_BLOCK == 0
    tile = pl.BlockSpec((1, tq, SB_HEAD_DIM), lambda bi, h, i: (bi, i, h))
    stacked = lambda which: pl.BlockSpec((None, 1, tq, SB_HEAD_DIM), lambda bi, h, i: (which, bi, i, h))
    whole = pl.BlockSpec((1, t, SB_HEAD_DIM), lambda bi, h, i: (bi, 0, h))
    return pl.pallas_call(
        functools.partial(_sb_prompt_kernel, tq=tq),
        grid=(b, SB_HEADS, t // tq),
        in_specs=[stacked(0), whole, whole, stacked(1),
                  pl.BlockSpec((SB_BLOCK, SB_BLOCK), lambda bi, h, i: (0, 0))],
        out_specs=tile,
        out_shape=jax.ShapeDtypeStruct(qz.shape[1:], BF16),
        scratch_shapes=[pltpu.VMEM((tq, SB_HEAD_DIM), F32), pltpu.VMEM((tq, V7X_LANES), F32)],
        compiler_params=_params(("arbitrary", "arbitrary", "arbitrary")),
        name="sb_prompt",
    )(qz, k, v, qz, _sb_consts())


def _sb_sample_kernel(q_ref, zg_ref, kn_ref, vn_ref, kc_hbm, vc_hbm, w_ref, o_ref,
                      kbuf, vbuf, sem, acc_ref, run_ref, *, n_blocks):
    bi = pl.program_id(0)
    t_new = q_ref.shape[1]
    block_rows = SB_BLOCK * SB_HEADS
    w = w_ref[...]

    def fetch(j, slot):
        src = pl.ds(j * block_rows, block_rows)
        return (pltpu.make_async_copy(kc_hbm.at[bi, src, :], kbuf.at[slot], sem.at[0, slot]),
                pltpu.make_async_copy(vc_hbm.at[bi, src, :], vbuf.at[slot], sem.at[1, slot]))

    for copy in fetch(n_blocks - 1, 0):
        copy.start()

    def head(h):
        return slice(h * t_new, (h + 1) * t_new), slice(h * SB_HEAD_DIM, (h + 1) * SB_HEAD_DIM)

    mask = _causal_mask(t_new)
    padding = jnp.zeros((SB_BLOCK - t_new, SB_HEAD_DIM), BF16)
    heads = [head(h) for h in range(SB_HEADS)]
    padded = lambda ref, cols: jnp.concatenate([ref[0, :, cols].astype(BF16), padding], axis=0)
    outs = _sb_blocks([q_ref[0, :, cols] for _, cols in heads], [[padded(kn_ref, cols)] for _, cols in heads],
                      [[padded(vn_ref, cols)] for _, cols in heads], w,
                      [jnp.zeros((t_new, V7X_LANES), F32)] * SB_HEADS, mask=mask)
    for (rows, _), (pv, run) in zip(heads, outs):
        acc_ref[rows, :] = pv
        run_ref[rows, :] = run

    def more(carry):
        return (carry[0] >= 0) & (carry[1] < SB_UNDERFLOW_LOG2)

    def past_block(carry):
        j = carry[0]
        slot = (n_blocks - 1 - j) % 2
        for copy in fetch(j, slot):
            copy.wait()

        @pl.when(j > 0)
        def _():
            for copy in fetch(j - 1, 1 - slot):
                copy.start()

        keys = [pl.ds(h, SB_BLOCK, stride=SB_HEADS) for h in range(SB_HEADS)]
        outs = _sb_blocks([q_ref[0, :, cols] for _, cols in heads],
                          [[kbuf[slot, rows, :].astype(BF16)] for rows in keys],
                          [[vbuf[slot, rows, :].astype(BF16)] for rows in keys], w,
                          [run_ref[rows, :] for rows, _ in heads])
        for (rows, _), (pv, run) in zip(heads, outs):
            acc_ref[rows, :] += pv
            run_ref[rows, :] = run
        return j - 1, jnp.min(run_ref[...])

    j_end, _ = lax.while_loop(more, past_block, (jnp.int32(n_blocks - 1), jnp.min(run_ref[...])))

    @pl.when(j_end >= 0)
    def _():
        for copy in fetch(j_end, (n_blocks - 1 - j_end) % 2):
            copy.wait()

    for h in range(SB_HEADS):
        rows, cols = head(h)
        o_ref[0, :, cols] = (acc_ref[rows, :] * _silu(zg_ref[0, :, cols].astype(F32))).astype(o_ref.dtype)


def _sb_sample(qz, k_new, v_new, k_past, v_past):
    _, b, t, d = qz.shape
    n_blocks = k_past.shape[1] // (SB_BLOCK * SB_HEADS)
    assert k_past.shape[1] == n_blocks * SB_BLOCK * SB_HEADS and n_blocks >= 1 and t <= V7X_LANES
    new = pl.BlockSpec((1, t, d), lambda bi: (bi, 0, 0))
    stacked = lambda which: pl.BlockSpec((None, 1, t, d), lambda bi: (which, bi, 0, 0))
    return pl.pallas_call(
        functools.partial(_sb_sample_kernel, n_blocks=n_blocks),
        grid=(b,),
        in_specs=[stacked(0), stacked(1), new, new,
                  pl.BlockSpec(memory_space=pl.ANY), pl.BlockSpec(memory_space=pl.ANY),
                  pl.BlockSpec((SB_BLOCK, SB_BLOCK), lambda bi: (0, 0))],
        out_specs=new,
        out_shape=jax.ShapeDtypeStruct(qz.shape[1:], BF16),
        scratch_shapes=[pltpu.VMEM((2, SB_BLOCK * SB_HEADS, SB_HEAD_DIM), F32),
                        pltpu.VMEM((2, SB_BLOCK * SB_HEADS, SB_HEAD_DIM), F32),
                        pltpu.SemaphoreType.DMA((2, 2)),
                        pltpu.VMEM((SB_HEADS * t, SB_HEAD_DIM), F32),
                        pltpu.VMEM((SB_HEADS * t, V7X_LANES), F32)],
        compiler_params=_params(("arbitrary",)),
        name="sb_sample",
    )(qz, qz, k_new, v_new, k_past, v_past, _sb_consts())


def _gla_kernel(*refs, chunk, n_chunks, has_state):
    if has_state:
        q_ref, k_ref, v_ref, dec_ref, r_ref, gn_ref, tril_ref, s0_ref, o_ref, sf_ref, st_ref, upd_ref = refs
    else:
        q_ref, k_ref, v_ref, dec_ref, r_ref, gn_ref, tril_ref, o_ref, sf_ref, st_ref, upd_ref = refs
    step = pl.program_id(2)

    @pl.when(step == 0)
    def _():
        st_ref[...] = s0_ref[0, 0].T if has_state else jnp.zeros_like(st_ref)

    tril2 = tril_ref[...]
    causal = (lax.broadcasted_iota(jnp.int32, (chunk, chunk), 1)
              <= lax.broadcasted_iota(jnp.int32, (chunk, chunk), 0))
    q_scale = GLA_DK ** -0.5
    nt_dot = lambda a, b: lax.dot_general(a, b, (((1,), (1,)), ((), ())), preferred_element_type=F32)
    chunks = [pl.ds(c * chunk, chunk) for c in range(n_chunks)]

    q_in, k_out, k_end, decay = [], [], [], []
    for rows in chunks:
        g_hi, g_lo = _split_bf16(dec_ref[0, rows, :])
        b = jnp.dot(tril2, jnp.concatenate([g_hi, g_lo], axis=0), preferred_element_type=F32)
        b_last = b[chunk - 1:chunk, :]
        k = k_ref[0, rows, :]
        q_in.append((q_ref[0, rows, :] * q_scale * jnp.exp(b)).astype(BF16))
        k_out.append((k * jnp.exp(-b)).astype(BF16))
        k_end.append((k * jnp.exp(b_last - b)).astype(BF16))
        decay.append(jnp.exp(b_last))
    intra = []
    for c, rows in enumerate(chunks):
        v = v_ref[0, rows, :]
        a = jnp.where(causal, nt_dot(q_in[c], k_out[c]), 0.0).astype(BF16)
        intra.append(jnp.dot(a, v, preferred_element_type=F32))
        upd_ref[c] = lax.dot_general(v, k_end[c], (((0,), (0,)), ((), ())), preferred_element_type=F32)
    for c, rows in enumerate(chunks):
        state_t = st_ref[...]
        o = intra[c] + nt_dot(q_in[c], state_t.astype(BF16))
        st_ref[...] = state_t * decay[c] + upd_ref[c]
        o = o * lax.rsqrt(jnp.mean(o * o, axis=-1, keepdims=True) + EPS) * gn_ref[...]
        o_ref[0, rows, :] = (o * _silu(r_ref[0, rows, :].astype(F32))).astype(o_ref.dtype)

    @pl.when(step == pl.num_programs(2) - 1)
    def _():
        sf_ref[0, 0] = st_ref[...].T


def _gla(qk, vr, dec, gn, s0, *, chunk, tt):
    _, b, t, _ = vr.shape
    assert t % tt == 0 and tt % chunk == 0
    tril = (lax.broadcasted_iota(jnp.int32, (chunk, chunk), 1)
            <= lax.broadcasted_iota(jnp.int32, (chunk, chunk), 0)).astype(BF16)
    tril2 = jnp.concatenate([tril, tril], axis=1)
    n_chunks = tt // chunk
    kspec = lambda off: pl.BlockSpec((1, tt, GLA_DK), lambda bi, h, s: (bi, s, h + off))
    vspec = pl.BlockSpec((1, tt, GLA_DV), lambda bi, h, s: (bi, s, h))
    sspec = pl.BlockSpec((1, 1, GLA_DK, GLA_DV), lambda bi, h, s: (bi, h, 0, 0))
    stacked = lambda which: pl.BlockSpec((None, 1, tt, GLA_DV), lambda bi, h, s: (which, bi, s, h))
    in_specs = [kspec(0), kspec(GLA_HEADS), stacked(0), kspec(0), stacked(1),
                pl.BlockSpec((1, GLA_DV), lambda bi, h, s: (0, h)),
                pl.BlockSpec((chunk, 2 * chunk), lambda bi, h, s: (0, 0))]
    args = [qk, qk, vr, dec, vr, gn.reshape(1, -1), tril2]
    if s0 is not None:
        in_specs.append(sspec)
        args.append(s0)
    return pl.pallas_call(
        functools.partial(_gla_kernel, chunk=chunk, n_chunks=n_chunks, has_state=s0 is not None),
        grid=(b, GLA_HEADS, t // tt),
        in_specs=in_specs,
        out_specs=[vspec, sspec],
        out_shape=[jax.ShapeDtypeStruct(vr.shape[1:], BF16),
                   jax.ShapeDtypeStruct((b, GLA_HEADS, GLA_DK, GLA_DV), F32)],
        scratch_shapes=[pltpu.VMEM((GLA_DV, GLA_DK), F32), pltpu.VMEM((n_chunks, GLA_DV, GLA_DK), F32)],
        compiler_params=_params(("arbitrary", "arbitrary", "arbitrary")),
        name="gla",
    )(*args)


def _outproj_kernel(y_ref, w_ref, x_ref, gate_ref, *rest, final_norm):
    if final_norm:
        gf_ref, o_ref = rest
    else:
        (o_ref,) = rest
    y = jnp.dot(y_ref[0], w_ref[...], preferred_element_type=F32)
    x = x_ref[0] + gate_ref[0] * y
    if final_norm:
        x = x * lax.rsqrt(jnp.mean(x * x, axis=-1, keepdims=True) + EPS) * gf_ref[...]
    o_ref[0] = x


def _outproj(y, w, x, gate, gf, *, tm):
    b, t, d = x.shape
    kdim = y.shape[-1]
    assert t % tm == 0
    gate_spec = (pl.BlockSpec((1, 1, d), lambda bi, i: (bi, 0, 0)) if gate.shape[1] == 1
                 else pl.BlockSpec((1, tm, d), lambda bi, i: (bi, i, 0)))
    in_specs = [pl.BlockSpec((1, tm, kdim), lambda bi, i: (bi, i, 0)),
                pl.BlockSpec((kdim, d), lambda bi, i: (0, 0)),
                pl.BlockSpec((1, tm, d), lambda bi, i: (bi, i, 0)),
                gate_spec]
    args = [y, w, x, gate]
    if gf is not None:
        in_specs.append(pl.BlockSpec((1, d), lambda bi, i: (0, 0)))
        args.append(gf.reshape(1, d))
    return pl.pallas_call(
        functools.partial(_outproj_kernel, final_norm=gf is not None),
        grid=(b, t // tm),
        in_specs=in_specs,
        out_specs=pl.BlockSpec((1, tm, d), lambda bi, i: (bi, i, 0)),
        out_shape=jax.ShapeDtypeStruct(x.shape, F32),
        compiler_params=_params(("arbitrary", "arbitrary")),
        name="outproj",
    )(*args)


def _trunk(x, mods, weights, *, per_row, tm_in, tn_in, tm_out, gla_tt, cache=None, state=None):
    (norm_g, sb_w_in, sb_w_out, gla_w_in, gla_wa, gla_wa2, gla_b_a, gla_norm_g, gla_w_out,
     final_norm_g) = weights
    b, t, d = x.shape
    if per_row:
        fold = lambda a: a.reshape(1, b * t, a.shape[-1])
        rows = lambda m: jnp.broadcast_to(m[:, None, :], (b, t, d)).reshape(1, b * t, d)
    else:
        fold = lambda a: a
        rows = lambda m: m[:, None, :]
    unfold = lambda a: a.reshape(b, t, a.shape[-1])
    width = SB_HEADS * SB_HEAD_DIM

    shift, scale, gate = mods[0]
    assert width == d
    k, v, qz = _inproj(fold(x), rows(scale), rows(shift), norm_g[0], sb_w_in, (1, 2, 0, 3), 2,
                       (SB_HEAD_DIM ** -0.5 * math.log2(math.e), None), tm=tm_in, tn=tn_in)
    k, v, qz = unfold(k), unfold(v), qz.reshape(2, b, t, d)
    if cache is None:
        branch = _sb_prompt(qz, k, v, tq=min(SB_QUERY_TILE, t))
    else:
        branch = _sb_sample(qz, k, v, cache[0], cache[1])
    x1 = _outproj(fold(branch), sb_w_out, fold(x), rows(gate), None, tm=tm_out)

    shift, scale, gate = mods[1]
    nk, nv = GLA_HEADS * GLA_DK, GLA_HEADS * GLA_DV
    assert 2 * nk == d and nv == d
    qk, vr, dec = _inproj(x1, rows(scale), rows(shift), norm_g[1], gla_w_in, (0, 1, 2), 1, (None, None),
                          tm=tm_in, tn=tn_in, decay=(gla_wa, gla_wa2, gla_b_a))
    chunk = min(GLA_CHUNK, t)
    branch, s_new = _gla(unfold(qk), vr.reshape(2, b, t, d), unfold(dec), gla_norm_g, state,
                         chunk=chunk, tt=min(gla_tt, t))
    y = _outproj(fold(branch), gla_w_out, x1, rows(gate), final_norm_g, tm=tm_out)
    return unfold(y), k, v, s_new


def kernel(x_prompt, x_sample, cache_sb_k, cache_sb_v, state_gla, c_prompt, c_sample, w_ada, b_ada, norm_g,
           sb_w_in, sb_w_out, gla_w_in, gla_w_a2, gla_b_a, gla_norm_g, gla_w_out, final_norm_g):
    depth, d, _ = w_ada.shape
    assert depth == 2 and sb_w_in.shape[0] == 1 and gla_w_in.shape[0] == 1
    bp, tp, _ = x_prompt.shape
    bs, ts, _ = x_sample.shape
    past = cache_sb_k.shape[2]
    nk, nv = GLA_HEADS * GLA_DK, GLA_HEADS * GLA_DV
    main = 2 * nk + 2 * nv

    mod = _ada(jnp.concatenate([c_prompt, c_sample], axis=0), w_ada, b_ada)
    split = lambda m: (m[:, :d], m[:, d:2 * d], m[:, 2 * d:])
    mods_p = [split(mod[l, :bp]) for l in range(depth)]
    mods_s = [split(mod[l, bp:]) for l in range(depth)]

    gla_w = gla_w_in[0]
    gla_wa = jnp.pad(gla_w[:, main:], ((0, 0), (0, V7X_LANES - GLA_GATE_RANK))).astype(BF16)
    gla_wa2 = jnp.pad(gla_w_a2[0], ((0, V7X_LANES - GLA_GATE_RANK), (0, 0))).astype(BF16)
    weights = (norm_g, sb_w_in[0].astype(BF16), sb_w_out[0].astype(BF16), gla_w.astype(BF16),
               gla_wa, gla_wa2, gla_b_a[0], gla_norm_g[0], gla_w_out[0].astype(BF16), final_norm_g)

    y_p, k_p, v_p, s_p = _trunk(x_prompt, mods_p, weights, per_row=False,
                                tm_in=min(1024, tp), tn_in=1024, tm_out=min(512, tp), gla_tt=1024)
    cache = (cache_sb_k[0].reshape(bs, past * SB_HEADS, SB_HEAD_DIM),
             cache_sb_v[0].reshape(bs, past * SB_HEADS, SB_HEAD_DIM))
    y_s, k_s, v_s, s_s = _trunk(x_sample, mods_s, weights, per_row=True,
                                tm_in=bs * ts, tn_in=d, tm_out=bs * ts, gla_tt=ts, cache=cache,
                                state=state_gla[0])

    heads = lambda a: a.reshape(1, a.shape[0], a.shape[1], SB_HEADS, SB_HEAD_DIM)
    return (y_p, y_s, heads(k_p), heads(v_p), heads(k_s), heads(v_s), s_p[None], s_s[None])
```

```python
import functools
import math

import jax
import jax.numpy as jnp
from jax import lax
from jax.experimental import pallas as pl
from jax.experimental.pallas import tpu as pltpu

F32 = jnp.float32
BF16 = jnp.bfloat16

SB_HEADS = 16
SB_HEAD_DIM = 128
GLA_HEADS = 4
GLA_DK = 256
GLA_DV = 512
GLA_GATE_RANK = 16
GLA_TAU = 16.0
GLA_CHUNK = 64
EPS = 1e-6

V7X_LANES = 128
V7X_VMEM_LIMIT_BYTES = 56 * 1024 * 1024
SB_BLOCK = 256
SB_QUERY_TILE = 4096
SB_UNDERFLOW_LOG2 = 160.0
SB_FINISHED_RUN = 1e30
NORM_ROWS = 16
NORM_UNROLL = 4


def _params(semantics):
    return pltpu.CompilerParams(dimension_semantics=semantics,
                                vmem_limit_bytes=V7X_VMEM_LIMIT_BYTES)


def _split_bf16(x):
    hi = x.astype(BF16)
    lo = (x - hi.astype(F32)).astype(BF16)
    return hi, lo


def _log_sigmoid(z):
    return jnp.minimum(z, 0.0) - jnp.log(1.0 + jnp.exp(-jnp.abs(z)))


def _softplus2(z):
    neg_abs = lax.bitcast_convert_type(lax.bitcast_convert_type(z, jnp.uint32) | jnp.uint32(1 << 31), F32)
    return jnp.maximum(z, 0.0) + jnp.log2(1.0 + jnp.exp2(neg_abs))


def _silu(z):
    return z / (1.0 + jnp.exp(-z))


def _ada_kernel(c_ref, w_ref, b_ref, o_ref):
    acc = jnp.dot(c_ref[...].astype(BF16), w_ref[0].astype(BF16), preferred_element_type=F32)
    o_ref[0] = acc + b_ref[0]


def _ada(c_all, w_ada, b_ada, tn=512):
    depth, d, n = w_ada.shape
    rows = c_all.shape[0]
    return pl.pallas_call(
        _ada_kernel,
        grid=(depth, n // tn),
        in_specs=[pl.BlockSpec((rows, d), lambda l, j: (0, 0)),
                  pl.BlockSpec((1, d, tn), lambda l, j: (l, 0, j)),
                  pl.BlockSpec((1, 1, tn), lambda l, j: (l, 0, j))],
        out_specs=pl.BlockSpec((1, rows, tn), lambda l, j: (l, 0, j)),
        out_shape=jax.ShapeDtypeStruct((depth, rows, n), F32),
        compiler_params=_params(("arbitrary", "arbitrary")),
        name="ada",
    )(c_all, w_ada, b_ada.reshape(depth, 1, n))


def _inproj_kernel(*refs, n_f32, bf16_mults, tm, steps_per_seg, with_decay):
    x_ref, sc_ref, sh_ref, g_ref, w_ref = refs[:5]
    pos = 5
    if with_decay:
        wa_ref, wa2_ref, ba_ref = refs[pos:pos + 3]
        pos += 3
    of_refs = refs[pos:pos + n_f32]
    ob_ref = refs[pos + n_f32]
    pos += n_f32 + 1
    if with_decay:
        dec_ref = refs[pos]
        pos += 1
    h_ref = refs[pos]
    n = pl.program_id(2)
    seg = n // steps_per_seg

    @pl.when(n == 0)
    def _():
        def norm_rows(r, carry):
            rows = pl.ds(pl.multiple_of(r * NORM_ROWS, NORM_ROWS), NORM_ROWS)
            x = x_ref[0, rows, :]
            y = x * lax.rsqrt(jnp.mean(x * x, axis=-1, keepdims=True) + EPS) * g_ref[...]
            if sc_ref.shape[1] == 1:
                sc, sh = sc_ref[0], sh_ref[0]
            else:
                sc, sh = sc_ref[0, rows, :], sh_ref[0, rows, :]
            h_ref[rows, :] = (y * (1.0 + sc) + sh).astype(BF16)
            return carry
        steps = tm // NORM_ROWS
        lax.fori_loop(0, steps, norm_rows, 0, unroll=NORM_UNROLL if steps % NORM_UNROLL == 0 else 1)

    for idx, of_ref in enumerate(of_refs):
        @pl.when(seg == idx)
        def _(of_ref=of_ref):
            of_ref[0] = jnp.dot(h_ref[...], w_ref[...], preferred_element_type=F32)

    @pl.when(seg >= n_f32)
    def _():
        acc = jnp.dot(h_ref[...], w_ref[...], preferred_element_type=F32)
        if any(m is not None for m in bf16_mults):
            mult = jnp.float32(1.0)
            for idx, m in enumerate(bf16_mults):
                if m is not None:
                    mult = jnp.where(seg == n_f32 + idx, jnp.float32(m), mult)
            acc = acc * mult
        ob_ref[0, 0] = acc.astype(BF16)

    if with_decay:
        @pl.when(n == pl.num_programs(2) - 1)
        def _():
            a_lr = jnp.dot(h_ref[...], wa_ref[...], preferred_element_type=F32)
            pre = jnp.dot(a_lr.astype(BF16), wa2_ref[...], preferred_element_type=F32) + ba_ref[...]
            dec_ref[0] = _log_sigmoid(pre) * (1.0 / GLA_TAU)


def _inproj(x, scale, shift, g, w, seg_order, n_f32, bf16_mults, *, tm, tn, decay=None):
    b, t, d = x.shape
    n_seg = len(seg_order)
    n_bf16 = len(bf16_mults)
    sps = d // tn
    assert t % tm == 0 and w.shape[1] >= n_seg * d and n_seg == n_f32 + n_bf16 and d == sps * tn
    assert tm % NORM_ROWS == 0

    def w_block(j):
        seg = jnp.int32(seg_order[-1])
        for k, s in enumerate(seg_order[:-1]):
            seg = jnp.where(j // sps == k, s, seg)
        return seg * sps + j % sps

    mod_rows = scale.shape[1]
    mod_spec = (pl.BlockSpec((1, 1, d), lambda bi, i, j: (bi, 0, 0)) if mod_rows == 1
                else pl.BlockSpec((1, tm, d), lambda bi, i, j: (bi, i, 0)))
    in_specs = [pl.BlockSpec((1, tm, d), lambda bi, i, j: (bi, i, 0)), mod_spec, mod_spec,
                pl.BlockSpec((1, d), lambda bi, i, j: (0, 0)),
                pl.BlockSpec((d, tn), lambda bi, i, j: (0, w_block(j)))]
    args = [x, scale, shift, g.reshape(1, d), w]
    if decay is not None:
        wa, wa2, ba = decay
        in_specs += [pl.BlockSpec(wa.shape, lambda bi, i, j: (0, 0)),
                     pl.BlockSpec(wa2.shape, lambda bi, i, j: (0, 0)),
                     pl.BlockSpec((1, ba.shape[-1]), lambda bi, i, j: (0, 0))]
        args += [wa, wa2, ba.reshape(1, -1)]

    def f32_spec(k):
        return pl.BlockSpec((1, tm, tn), lambda bi, i, j: (bi, i, jnp.clip(j - k * sps, 0, sps - 1)))

    out_specs = [f32_spec(k) for k in range(n_f32)]
    out_specs.append(pl.BlockSpec(
        (1, 1, tm, tn),
        lambda bi, i, j: (jnp.maximum(j // sps - n_f32, 0), bi, i, jnp.where(j // sps >= n_f32, j % sps, 0))))
    out_shape = [jax.ShapeDtypeStruct((b, t, d), F32) for _ in range(n_f32)]
    out_shape.append(jax.ShapeDtypeStruct((n_bf16, b, t, d), BF16))
    if decay is not None:
        nk = decay[1].shape[1]
        out_specs.append(pl.BlockSpec((1, tm, nk), lambda bi, i, j: (bi, i, 0)))
        out_shape.append(jax.ShapeDtypeStruct((b, t, nk), F32))
    return pl.pallas_call(
        functools.partial(_inproj_kernel, n_f32=n_f32, bf16_mults=tuple(bf16_mults), tm=tm,
                          steps_per_seg=sps, with_decay=decay is not None),
        grid=(b, t // tm, n_seg * sps),
        in_specs=in_specs, out_specs=out_specs, out_shape=out_shape,
        scratch_shapes=[pltpu.VMEM((tm, d), BF16)],
        compiler_params=_params(("arbitrary", "arbitrary", "arbitrary")),
        name="inproj",
    )(*args)


def _sb_blocks(qs, k_chains, v_chains, w, runs, mask=None, valid_chains=None):
    half = V7X_LANES
    stacked = lambda hi, lo, cols: jnp.concatenate([hi[:, cols], lo[:, cols]], axis=1)
    z_chains = [[lax.dot_general(q, k_blk, (((1,), (1,)), ((), ())), preferred_element_type=F32)
                 for k_blk in chain] for q, chain in zip(qs, k_chains)]
    sum_chains = []
    for z_chain in z_chains:
        sums = []
        for pos, z in enumerate(z_chain):
            sp = _softplus2(z)
            if mask is not None and pos == 0:
                sp = jnp.where(mask, sp, 0.0)
            hi, lo = _split_bf16(sp)
            sums.append((jnp.dot(stacked(hi, lo, slice(half, None)), w, preferred_element_type=F32),
                         jnp.dot(stacked(hi, lo, slice(None, half)), w, preferred_element_type=F32)))
        sum_chains.append(sums)
    outs = []
    for g, (z_chain, sums, v_chain, run) in enumerate(zip(z_chains, sum_chains, v_chains, runs)):
        weights = []
        for pos, (z, (cs_r, cs_l)) in enumerate(zip(z_chain, sums)):
            if valid_chains is not None and valid_chains[g][pos] is not None:
                run = jnp.where(valid_chains[g][pos], run, SB_FINISHED_RUN)
            a_r = jnp.exp2(z[:, half:] - cs_r[:, :half] - run)
            run = run + cs_r[:, half:]
            a_l = jnp.exp2(z[:, :half] - cs_l[:, :half] - run)
            run = run + cs_l[:, half:]
            a = jnp.concatenate([a_l, a_r], axis=1)
            if mask is not None and pos == 0:
                a = jnp.where(mask, a, 0.0)
            weights.append(a.astype(BF16))
        a_all = weights[0] if len(weights) == 1 else jnp.concatenate(weights, axis=1)
        v_all = v_chain[0] if len(v_chain) == 1 else jnp.concatenate(v_chain, axis=0)
        outs.append((jnp.dot(a_all, v_all, preferred_element_type=F32), run))
    return outs


def _sb_consts():
    j = lax.broadcasted_iota(jnp.int32, (V7X_LANES, V7X_LANES), 0)
    s = lax.broadcasted_iota(jnp.int32, (V7X_LANES, V7X_LANES), 1)
    half = jnp.concatenate([(j >= s).astype(BF16), jnp.ones((V7X_LANES, V7X_LANES), BF16)], axis=1)
    return jnp.concatenate([half, half], axis=0)


def _causal_mask(rows):
    t_idx = lax.broadcasted_iota(jnp.int32, (rows, SB_BLOCK), 0)
    s_idx = lax.broadcasted_iota(jnp.int32, (rows, SB_BLOCK), 1)
    return s_idx < t_idx


def _sb_prompt_kernel(q_ref, k_ref, v_ref, zg_ref, w_ref, o_ref, acc_ref, run_ref, *, tq):
    i = pl.program_id(2)
    groups = tq // SB_BLOCK
    w = w_ref[...]

    def keys(first):
        rows = pl.ds(pl.multiple_of(first, SB_BLOCK), SB_BLOCK)
        return k_ref[0, rows, :].astype(BF16), v_ref[0, rows, :].astype(BF16)

    group_rows = [slice(g * SB_BLOCK, (g + 1) * SB_BLOCK) for g in range(groups)]
    queries = lambda: [q_ref[0, rows, :] for rows in group_rows]
    blks = [i * groups + g for g in range(groups)]
    kvs = [(keys(blk * SB_BLOCK), keys(jnp.maximum(blk - 1, 0) * SB_BLOCK)) for blk in blks]
    outs = _sb_blocks(queries(), [[kv[0][0], kv[1][0]] for kv in kvs], [[kv[0][1], kv[1][1]] for kv in kvs], w,
                      [jnp.zeros((SB_BLOCK, V7X_LANES), F32)] * groups, mask=_causal_mask(SB_BLOCK),
                      valid_chains=[[None, blk >= 1] for blk in blks])
    for rows, (pv, run) in zip(group_rows, outs):
        acc_ref[rows, :] = pv
        run_ref[rows, :] = run

    def pending(d):
        low = jnp.float32(jnp.inf)
        for g, rows in enumerate(group_rows):
            low = jnp.minimum(low, jnp.where(i * groups + g - d >= 0, jnp.min(run_ref[rows, :]), jnp.inf))
        return low

    def more(carry):
        return carry[1] < SB_UNDERFLOW_LOG2

    def diagonal(carry):
        d = carry[0]
        blks = [i * groups + g - d for g in range(groups)]
        kvs = [keys(jnp.maximum(blk, 0) * SB_BLOCK) for blk in blks]
        outs = _sb_blocks(queries(), [[kv[0]] for kv in kvs], [[kv[1]] for kv in kvs], w,
                          [run_ref[rows, :] for rows in group_rows],
                          valid_chains=[[blk >= 0] for blk in blks])
        for rows, (pv, run) in zip(group_rows, outs):
            acc_ref[rows, :] += pv
            run_ref[rows, :] = run
        return d + 1, pending(d + 1)

    lax.while_loop(more, diagonal, (jnp.int32(2), pending(2)))
    o_ref[0] = (acc_ref[...] * _silu(zg_ref[0].astype(F32))).astype(o_ref.dtype)


def _sb_prompt(qz, k, v, tq):
    _, b, t, _ = qz.shape
    assert t % tq == 0 and tq % SB_BLOCK == 0
    tile = pl.BlockSpec((1, tq, SB_HEAD_DIM), lambda bi, h, i: (bi, i, h))
    stacked = lambda which: pl.BlockSpec((None, 1, tq, SB_HEAD_DIM), lambda bi, h, i: (which, bi, i, h))
    whole = pl.BlockSpec((1, t, SB_HEAD_DIM), lambda bi, h, i: (bi, 0, h))
    return pl.pallas_call(
        functools.partial(_sb_prompt_kernel, tq=tq),
        grid=(b, SB_HEADS, t // tq),
        in_specs=[stacked(0), whole, whole, stacked(1),
                  pl.BlockSpec((SB_BLOCK, SB_BLOCK), lambda bi, h, i: (0, 0))],
        out_specs=tile,
        out_shape=jax.ShapeDtypeStruct(qz.shape[1:], BF16),
        scratch_shapes=[pltpu.VMEM((tq, SB_HEAD_DIM), F32), pltpu.VMEM((tq, V7X_LANES), F32)],
        compiler_params=_params(("arbitrary", "arbitrary", "arbitrary")),
        name="sb_prompt",
    )(qz, k, v, qz, _sb_consts())


def _sb_sample_kernel(q_ref, zg_ref, kn_ref, vn_ref, kc_hbm, vc_hbm, w_ref, o_ref,
                      kbuf, vbuf, sem, acc_ref, run_ref, *, n_blocks):
    bi = pl.program_id(0)
    t_new = q_ref.shape[1]
    block_rows = SB_BLOCK * SB_HEADS
    w = w_ref[...]

    def fetch(j, slot):
        src = pl.ds(j * block_rows, block_rows)
        return (pltpu.make_async_copy(kc_hbm.at[bi, src, :], kbuf.at[slot], sem.at[0, slot]),
                pltpu.make_async_copy(vc_hbm.at[bi, src, :], vbuf.at[slot], sem.at[1, slot]))

    for copy in fetch(n_blocks - 1, 0):
        copy.start()

    def head(h):
        return slice(h * t_new, (h + 1) * t_new), slice(h * SB_HEAD_DIM, (h + 1) * SB_HEAD_DIM)

    mask = _causal_mask(t_new)
    padding = jnp.zeros((SB_BLOCK - t_new, SB_HEAD_DIM), BF16)
    heads = [head(h) for h in range(SB_HEADS)]
    padded = lambda ref, cols: jnp.concatenate([ref[0, :, cols].astype(BF16), padding], axis=0)
    outs = _sb_blocks([q_ref[0, :, cols] for _, cols in heads], [[padded(kn_ref, cols)] for _, cols in heads],
                      [[padded(vn_ref, cols)] for _, cols in heads], w,
                      [jnp.zeros((t_new, V7X_LANES), F32)] * SB_HEADS, mask=mask)
    for (rows, _), (pv, run) in zip(heads, outs):
        acc_ref[rows, :] = pv
        run_ref[rows, :] = run

    def more(carry):
        return (carry[0] >= 0) & (carry[1] < SB_UNDERFLOW_LOG2)

    def past_block(carry):
        j = carry[0]
        slot = (n_blocks - 1 - j) % 2
        for copy in fetch(j, slot):
            copy.wait()

        @pl.when(j > 0)
        def _():
            for copy in fetch(j - 1, 1 - slot):
                copy.start()

        keys = [pl.ds(h, SB_BLOCK, stride=SB_HEADS) for h in range(SB_HEADS)]
        outs = _sb_blocks([q_ref[0, :, cols] for _, cols in heads],
                          [[kbuf[slot, rows, :].astype(BF16)] for rows in keys],
                          [[vbuf[slot, rows, :].astype(BF16)] for rows in keys], w,
                          [run_ref[rows, :] for rows, _ in heads])
        for (rows, _), (pv, run) in zip(heads, outs):
            acc_ref[rows, :] += pv
            run_ref[rows, :] = run
        return j - 1, jnp.min(run_ref[...])

    j_end, _ = lax.while_loop(more, past_block, (jnp.int32(n_blocks - 1), jnp.min(run_ref[...])))

    @pl.when(j_end >= 0)
    def _():
        for copy in fetch(j_end, (n_blocks - 1 - j_end) % 2):
            copy.wait()

    for h in range(SB_HEADS):
        rows, cols = head(h)
        o_ref[0, :, cols] = (acc_ref[rows, :] * _silu(zg_ref[0, :, cols].astype(F32))).astype(o_ref.dtype)


def _sb_sample(qz, k_new, v_new, k_past, v_past):
    _, b, t, d = qz.shape
    n_blocks = k_past.shape[1] // (SB_BLOCK * SB_HEADS)
    assert k_past.shape[1] == n_blocks * SB_BLOCK * SB_HEADS and n_blocks >= 1 and t <= V7X_LANES
    new = pl.BlockSpec((1, t, d), lambda bi: (bi, 0, 0))
    stacked = lambda which: pl.BlockSpec((None, 1, t, d), lambda bi: (which, bi, 0, 0))
    return pl.pallas_call(
        functools.partial(_sb_sample_kernel, n_blocks=n_blocks),
        grid=(b,),
        in_specs=[stacked(0), stacked(1), new, new,
                  pl.BlockSpec(memory_space=pl.ANY), pl.BlockSpec(memory_space=pl.ANY),
                  pl.BlockSpec((SB_BLOCK, SB_BLOCK), lambda bi: (0, 0))],
        out_specs=new,
        out_shape=jax.ShapeDtypeStruct(qz.shape[1:], BF16),
        scratch_shapes=[pltpu.VMEM((2, SB_BLOCK * SB_HEADS, SB_HEAD_DIM), F32),
                        pltpu.VMEM((2, SB_BLOCK * SB_HEADS, SB_HEAD_DIM), F32),
                        pltpu.SemaphoreType.DMA((2, 2)),
                        pltpu.VMEM((SB_HEADS * t, SB_HEAD_DIM), F32),
                        pltpu.VMEM((SB_HEADS * t, V7X_LANES), F32)],
        compiler_params=_params(("arbitrary",)),
        name="sb_sample",
    )(qz, qz, k_new, v_new, k_past, v_past, _sb_consts())


def _gla_kernel(*refs, chunk, n_chunks, has_state):
    if has_state:
        q_ref, k_ref, v_ref, dec_ref, r_ref, gn_ref, tril_ref, s0_ref, o_ref, sf_ref, st_ref, upd_ref = refs
    else:
        q_ref, k_ref, v_ref, dec_ref, r_ref, gn_ref, tril_ref, o_ref, sf_ref, st_ref, upd_ref = refs
    step = pl.program_id(2)

    @pl.when(step == 0)
    def _():
        st_ref[...] = s0_ref[0, 0].T if has_state else jnp.zeros_like(st_ref)

    tril2 = tril_ref[...]
    causal = (lax.broadcasted_iota(jnp.int32, (chunk, chunk), 1)
              <= lax.broadcasted_iota(jnp.int32, (chunk, chunk), 0))
    q_scale = GLA_DK ** -0.5
    nt_dot = lambda a, b: lax.dot_general(a, b, (((1,), (1,)), ((), ())), preferred_element_type=F32)
    chunks = [pl.ds(c * chunk, chunk) for c in range(n_chunks)]

    q_in, k_out, k_end, decay = [], [], [], []
    for rows in chunks:
        g_hi, g_lo = _split_bf16(dec_ref[0, rows, :])
        b = jnp.dot(tril2, jnp.concatenate([g_hi, g_lo], axis=0), preferred_element_type=F32)
        b_last = b[chunk - 1:chunk, :]
        k = k_ref[0, rows, :]
        q_in.append((q_ref[0, rows, :] * q_scale * jnp.exp(b)).astype(BF16))
        k_out.append((k * jnp.exp(-b)).astype(BF16))
        k_end.append((k * jnp.exp(b_last - b)).astype(BF16))
        decay.append(jnp.exp(b_last))
    intra = []
    for c, rows in enumerate(chunks):
        v = v_ref[0, rows, :]
        a = jnp.where(causal, nt_dot(q_in[c], k_out[c]), 0.0).astype(BF16)
        intra.append(jnp.dot(a, v, preferred_element_type=F32))
        upd_ref[c] = lax.dot_general(v, k_end[c], (((0,), (0,)), ((), ())), preferred_element_type=F32)
    for c, rows in enumerate(chunks):
        state_t = st_ref[...]
        o = intra[c] + nt_dot(q_in[c], state_t.astype(BF16))
        st_ref[...] = state_t * decay[c] + upd_ref[c]
        o = o * lax.rsqrt(jnp.mean(o * o, axis=-1, keepdims=True) + EPS) * gn_ref[...]
        o_ref[0, rows, :] = (o * _silu(r_ref[0, rows, :].astype(F32))).astype(o_ref.dtype)

    @pl.when(step == pl.num_programs(2) - 1)
    def _():
        sf_ref[0, 0] = st_ref[...].T


def _gla(qk, vr, dec, gn, s0, *, chunk, tt):
    _, b, t, _ = vr.shape
    assert t % tt == 0 and tt % chunk == 0
    tril = (lax.broadcasted_iota(jnp.int32, (chunk, chunk), 1)
            <= lax.broadcasted_iota(jnp.int32, (chunk, chunk), 0)).astype(BF16)
    tril2 = jnp.concatenate([tril, tril], axis=1)
    n_chunks = tt // chunk
    kspec = lambda off: pl.BlockSpec((1, tt, GLA_DK), lambda bi, h, s: (bi, s, h + off))
    vspec = pl.BlockSpec((1, tt, GLA_DV), lambda bi, h, s: (bi, s, h))
    sspec = pl.BlockSpec((1, 1, GLA_DK, GLA_DV), lambda bi, h, s: (bi, h, 0, 0))
    stacked = lambda which: pl.BlockSpec((None, 1, tt, GLA_DV), lambda bi, h, s: (which, bi, s, h))
    in_specs = [kspec(0), kspec(GLA_HEADS), stacked(0), kspec(0), stacked(1),
                pl.BlockSpec((1, GLA_DV), lambda bi, h, s: (0, h)),
                pl.BlockSpec((chunk, 2 * chunk), lambda bi, h, s: (0, 0))]
    args = [qk, qk, vr, dec, vr, gn.reshape(1, -1), tril2]
    if s0 is not None:
        in_specs.append(sspec)
        args.append(s0)
    return pl.pallas_call(
        functools.partial(_gla_kernel, chunk=chunk, n_chunks=n_chunks, has_state=s0 is not None),
        grid=(b, GLA_HEADS, t // tt),
        in_specs=in_specs,
        out_specs=[vspec, sspec],
        out_shape=[jax.ShapeDtypeStruct(vr.shape[1:], BF16),
                   jax.ShapeDtypeStruct((b, GLA_HEADS, GLA_DK, GLA_DV), F32)],
        scratch_shapes=[pltpu.VMEM((GLA_DV, GLA_DK), F32), pltpu.VMEM((n_chunks, GLA_DV, GLA_DK), F32)],
        compiler_params=_params(("arbitrary", "arbitrary", "arbitrary")),
        name="gla",
    )(*args)


def _outproj_kernel(y_ref, w_ref, x_ref, gate_ref, *rest, final_norm):
    if final_norm:
        gf_ref, o_ref = rest
    else:
        (o_ref,) = rest
    y = jnp.dot(y_ref[0], w_ref[...], preferred_element_type=F32)
    x = x_ref[0] + gate_ref[0] * y
    if final_norm:
        x = x * lax.rsqrt(jnp.mean(x * x, axis=-1, keepdims=True) + EPS) * gf_ref[...]
    o_ref[0] = x


def _outproj(y, w, x, gate, gf, *, tm):
    b, t, d = x.shape
    kdim = y.shape[-1]
    assert t % tm == 0
    gate_spec = (pl.BlockSpec((1, 1, d), lambda bi, i: (bi, 0, 0)) if gate.shape[1] == 1
                 else pl.BlockSpec((1, tm, d), lambda bi, i: (bi, i, 0)))
    in_specs = [pl.BlockSpec((1, tm, kdim), lambda bi, i: (bi, i, 0)),
                pl.BlockSpec((kdim, d), lambda bi, i: (0, 0)),
                pl.BlockSpec((1, tm, d), lambda bi, i: (bi, i, 0)),
                gate_spec]
    args = [y, w, x, gate]
    if gf is not None:
        in_specs.append(pl.BlockSpec((1, d), lambda bi, i: (0, 0)))
        args.append(gf.reshape(1, d))
    return pl.pallas_call(
        functools.partial(_outproj_kernel, final_norm=gf is not None),
        grid=(b, t // tm),
        in_specs=in_specs,
        out_specs=pl.BlockSpec((1, tm, d), lambda bi, i: (bi, i, 0)),
        out_shape=jax.ShapeDtypeStruct(x.shape, F32),
        compiler_params=_params(("arbitrary", "arbitrary")),
        name="outproj",
    )(*args)


def _trunk(x, mods, weights, *, per_row, tm_in, tn_in, tm_out, gla_tt, cache=None, state=None):
    (norm_g, sb_w_in, sb_w_out, gla_w_in, gla_wa, gla_wa2, gla_b_a, gla_norm_g, gla_w_out,
     final_norm_g) = weights
    b, t, d = x.shape
    if per_row:
        fold = lambda a: a.reshape(1, b * t, a.shape[-1])
        rows = lambda m: jnp.broadcast_to(m[:, None, :], (b, t, d)).reshape(1, b * t, d)
    else:
        fold = lambda a: a
        rows = lambda m: m[:, None, :]
    unfold = lambda a: a.reshape(b, t, a.shape[-1])
    width = SB_HEADS * SB_HEAD_DIM

    shift, scale, gate = mods[0]
    assert width == d
    k, v, qz = _inproj(fold(x), rows(scale), rows(shift), norm_g[0], sb_w_in, (1, 2, 0, 3), 2,
                       (SB_HEAD_DIM ** -0.5 * math.log2(math.e), None), tm=tm_in, tn=tn_in)
    k, v, qz = unfold(k), unfold(v), qz.reshape(2, b, t, d)
    if cache is None:
        branch = _sb_prompt(qz, k, v, tq=min(SB_QUERY_TILE, t))
    else:
        branch = _sb_sample(qz, k, v, cache[0], cache[1])
    x1 = _outproj(fold(branch), sb_w_out, fold(x), rows(gate), None, tm=tm_out)

    shift, scale, gate = mods[1]
    nk, nv = GLA_HEADS * GLA_DK, GLA_HEADS * GLA_DV
    assert 2 * nk == d and nv == d
    qk, vr, dec = _inproj(x1, rows(scale), rows(shift), norm_g[1], gla_w_in, (0, 1, 2), 1, (None, None),
                          tm=tm_in, tn=tn_in, decay=(gla_wa, gla_wa2, gla_b_a))
    chunk = min(GLA_CHUNK, t)
    branch, s_new = _gla(unfold(qk), vr.reshape(2, b, t, d), unfold(dec), gla_norm_g, state,
                         chunk=chunk, tt=min(gla_tt, t))
    y = _outproj(fold(branch), gla_w_out, x1, rows(gate), final_norm_g, tm=tm_out)
    return unfold(y), k, v, s_new


def kernel(x_prompt, x_sample, cache_sb_k, cache_sb_v, state_gla, c_prompt, c_sample, w_ada, b_ada, norm_g,
           sb_w_in, sb_w_out, gla_w_in, gla_w_a2, gla_b_a, gla_norm_g, gla_w_out, final_norm_g):
    depth, d, _ = w_ada.shape
    assert depth == 2 and sb_w_in.shape[0] == 1 and gla_w_in.shape[0] == 1
    bp, tp, _ = x_prompt.shape
    bs, ts, _ = x_sample.shape
    past = cache_sb_k.shape[2]
    nk, nv = GLA_HEADS * GLA_DK, GLA_HEADS * GLA_DV
    main = 2 * nk + 2 * nv

    mod = _ada(jnp.concatenate([c_prompt, c_sample], axis=0), w_ada, b_ada)
    split = lambda m: (m[:, :d], m[:, d:2 * d], m[:, 2 * d:])
    mods_p = [split(mod[l, :bp]) for l in range(depth)]
    mods_s = [split(mod[l, bp:]) for l in range(depth)]

    gla_w = gla_w_in[0]
    gla_wa = jnp.pad(gla_w[:, main:], ((0, 0), (0, V7X_LANES - GLA_GATE_RANK))).astype(BF16)
    gla_wa2 = jnp.pad(gla_w_a2[0], ((0, V7X_LANES - GLA_GATE_RANK), (0, 0))).astype(BF16)
    weights = (norm_g, sb_w_in[0].astype(BF16), sb_w_out[0].astype(BF16), gla_w.astype(BF16),
               gla_wa, gla_wa2, gla_b_a[0], gla_norm_g[0], gla_w_out[0].astype(BF16), final_norm_g)

    y_p, k_p, v_p, s_p = _trunk(x_prompt, mods_p, weights, per_row=False,
                                tm_in=min(1024, tp), tn_in=1024, tm_out=min(512, tp), gla_tt=1024)
    cache = (cache_sb_k[0].reshape(bs, past * SB_HEADS, SB_HEAD_DIM),
             cache_sb_v[0].reshape(bs, past * SB_HEADS, SB_HEAD_DIM))
    y_s, k_s, v_s, s_s = _trunk(x_sample, mods_s, weights, per_row=True,
                                tm_in=bs * ts, tn_in=d, tm_out=bs * ts, gla_tt=ts, cache=cache,
                                state=state_gla[0])

    heads = lambda a: a.reshape(1, a.shape[0], a.shape[1], SB_HEADS, SB_HEAD_DIM)
    return (y_p, y_s, heads(k_p), heads(v_p), heads(k_s), heads(v_s), s_p[None], s_s[None])
```

```python
import functools
import math

import jax
import jax.numpy as jnp
from jax import lax
from jax.experimental import pallas as pl
from jax.experimental.pallas import tpu as pltpu

F32 = jnp.float32
BF16 = jnp.bfloat16

SB_HEADS = 16
SB_HEAD_DIM = 128
GLA_HEADS = 4
GLA_DK = 256
GLA_DV = 512
GLA_GATE_RANK = 16
GLA_TAU = 16.0
GLA_CHUNK = 64
EPS = 1e-6

V7X_LANES = 128
V7X_VMEM_LIMIT_BYTES = 56 * 1024 * 1024
ADA_COL_TILE = 512
INPROJ_ROW_TILE = 1024
INPROJ_COL_TILE = 1024
OUTPROJ_ROW_TILE = 512
GLA_TOKEN_TILE = 1024
SB_BLOCK = 256
SB_QUERY_TILE = 4096
SB_UNDERFLOW_LOG2 = 160.0
SB_FINISHED_RUN = 1e30
NORM_ROWS = 16
NORM_UNROLL = 4


def _params(semantics):
    return pltpu.CompilerParams(dimension_semantics=semantics,
                                vmem_limit_bytes=V7X_VMEM_LIMIT_BYTES)


def _split_bf16(x):
    hi = x.astype(BF16)
    lo = (x - hi.astype(F32)).astype(BF16)
    return hi, lo


def _log_sigmoid(z):
    return jnp.minimum(z, 0.0) - jnp.log(1.0 + jnp.exp(-jnp.abs(z)))


def _softplus2(z):
    neg_abs = lax.bitcast_convert_type(lax.bitcast_convert_type(z, jnp.uint32) | jnp.uint32(1 << 31), F32)
    return jnp.maximum(z, 0.0) + jnp.log2(1.0 + jnp.exp2(neg_abs))


def _silu(z):
    return z / (1.0 + jnp.exp(-z))


def _ada_kernel(c_ref, w_ref, b_ref, o_ref):
    acc = jnp.dot(c_ref[...].astype(BF16), w_ref[0].astype(BF16), preferred_element_type=F32)
    o_ref[0] = acc + b_ref[0]


def _ada(c_all, w_ada, b_ada, tn=ADA_COL_TILE):
    depth, d, n = w_ada.shape
    rows = c_all.shape[0]
    return pl.pallas_call(
        _ada_kernel,
        grid=(depth, n // tn),
        in_specs=[pl.BlockSpec((rows, d), lambda l, j: (0, 0)),
                  pl.BlockSpec((1, d, tn), lambda l, j: (l, 0, j)),
                  pl.BlockSpec((1, 1, tn), lambda l, j: (l, 0, j))],
        out_specs=pl.BlockSpec((1, rows, tn), lambda l, j: (l, 0, j)),
        out_shape=jax.ShapeDtypeStruct((depth, rows, n), F32),
        compiler_params=_params(("arbitrary", "arbitrary")),
        name="ada",
    )(c_all, w_ada, b_ada.reshape(depth, 1, n))


def _inproj_kernel(*refs, n_f32, bf16_mults, tm, steps_per_seg, with_decay):
    x_ref, sc_ref, sh_ref, g_ref, w_ref = refs[:5]
    pos = 5
    if with_decay:
        wa_ref, wa2_ref, ba_ref = refs[pos:pos + 3]
        pos += 3
    of_refs = refs[pos:pos + n_f32]
    ob_ref = refs[pos + n_f32]
    pos += n_f32 + 1
    if with_decay:
        dec_ref = refs[pos]
        pos += 1
    h_ref = refs[pos]
    n = pl.program_id(2)
    seg = n // steps_per_seg

    @pl.when(n == 0)
    def _():
        def norm_rows(r, carry):
            rows = pl.ds(pl.multiple_of(r * NORM_ROWS, NORM_ROWS), NORM_ROWS)
            x = x_ref[0, rows, :]
            y = x * lax.rsqrt(jnp.mean(x * x, axis=-1, keepdims=True) + EPS) * g_ref[...]
            if sc_ref.shape[1] == 1:
                sc, sh = sc_ref[0], sh_ref[0]
            else:
                sc, sh = sc_ref[0, rows, :], sh_ref[0, rows, :]
            h_ref[rows, :] = (y * (1.0 + sc) + sh).astype(BF16)
            return carry
        steps = tm // NORM_ROWS
        lax.fori_loop(0, steps, norm_rows, 0, unroll=NORM_UNROLL if steps % NORM_UNROLL == 0 else 1)

    for idx, of_ref in enumerate(of_refs):
        @pl.when(seg == idx)
        def _(of_ref=of_ref):
            of_ref[0] = jnp.dot(h_ref[...], w_ref[...], preferred_element_type=F32)

    @pl.when(seg >= n_f32)
    def _():
        acc = jnp.dot(h_ref[...], w_ref[...], preferred_element_type=F32)
        if any(m is not None for m in bf16_mults):
            mult = jnp.float32(1.0)
            for idx, m in enumerate(bf16_mults):
                if m is not None:
                    mult = jnp.where(seg == n_f32 + idx, jnp.float32(m), mult)
            acc = acc * mult
        ob_ref[0, 0] = acc.astype(BF16)

    if with_decay:
        @pl.when(n == pl.num_programs(2) - 1)
        def _():
            a_lr = jnp.dot(h_ref[...], wa_ref[...], preferred_element_type=F32)
            pre = jnp.dot(a_lr.astype(BF16), wa2_ref[...], preferred_element_type=F32) + ba_ref[...]
            dec_ref[0] = _log_sigmoid(pre) * (1.0 / GLA_TAU)


def _inproj(x, scale, shift, g, w, seg_order, n_f32, bf16_mults, *, tm, tn, decay=None):
    b, t, d = x.shape
    n_seg = len(seg_order)
    n_bf16 = len(bf16_mults)
    sps = d // tn
    assert t % tm == 0 and w.shape[1] >= n_seg * d and n_seg == n_f32 + n_bf16 and d == sps * tn
    assert tm % NORM_ROWS == 0

    def w_block(j):
        seg = jnp.int32(seg_order[-1])
        for k, s in enumerate(seg_order[:-1]):
            seg = jnp.where(j // sps == k, s, seg)
        return seg * sps + j % sps

    mod_rows = scale.shape[1]
    mod_spec = (pl.BlockSpec((1, 1, d), lambda bi, i, j: (bi, 0, 0)) if mod_rows == 1
                else pl.BlockSpec((1, tm, d), lambda bi, i, j: (bi, i, 0)))
    in_specs = [pl.BlockSpec((1, tm, d), lambda bi, i, j: (bi, i, 0)), mod_spec, mod_spec,
                pl.BlockSpec((1, d), lambda bi, i, j: (0, 0)),
                pl.BlockSpec((d, tn), lambda bi, i, j: (0, w_block(j)))]
    args = [x, scale, shift, g.reshape(1, d), w]
    if decay is not None:
        wa, wa2, ba = decay
        in_specs += [pl.BlockSpec(wa.shape, lambda bi, i, j: (0, 0)),
                     pl.BlockSpec(wa2.shape, lambda bi, i, j: (0, 0)),
                     pl.BlockSpec((1, ba.shape[-1]), lambda bi, i, j: (0, 0))]
        args += [wa, wa2, ba.reshape(1, -1)]

    def f32_spec(k):
        return pl.BlockSpec((1, tm, tn), lambda bi, i, j: (bi, i, jnp.clip(j - k * sps, 0, sps - 1)))

    out_specs = [f32_spec(k) for k in range(n_f32)]
    out_specs.append(pl.BlockSpec(
        (1, 1, tm, tn),
        lambda bi, i, j: (jnp.maximum(j // sps - n_f32, 0), bi, i, jnp.where(j // sps >= n_f32, j % sps, 0))))
    out_shape = [jax.ShapeDtypeStruct((b, t, d), F32) for _ in range(n_f32)]
    out_shape.append(jax.ShapeDtypeStruct((n_bf16, b, t, d), BF16))
    if decay is not None:
        nk = decay[1].shape[1]
        out_specs.append(pl.BlockSpec((1, tm, nk), lambda bi, i, j: (bi, i, 0)))
        out_shape.append(jax.ShapeDtypeStruct((b, t, nk), F32))
    return pl.pallas_call(
        functools.partial(_inproj_kernel, n_f32=n_f32, bf16_mults=tuple(bf16_mults), tm=tm,
                          steps_per_seg=sps, with_decay=decay is not None),
        grid=(b, t // tm, n_seg * sps),
        in_specs=in_specs, out_specs=out_specs, out_shape=out_shape,
        scratch_shapes=[pltpu.VMEM((tm, d), BF16)],
        compiler_params=_params(("arbitrary", "arbitrary", "arbitrary")),
        name="inproj",
    )(*args)


def _sb_blocks(qs, k_chains, v_chains, w, runs, mask=None, valid_chains=None):
    half = V7X_LANES
    stacked = lambda hi, lo, cols: jnp.concatenate([hi[:, cols], lo[:, cols]], axis=1)
    z_chains = [[lax.dot_general(q, k_blk, (((1,), (1,)), ((), ())), preferred_element_type=F32)
                 for k_blk in chain] for q, chain in zip(qs, k_chains)]
    sum_chains = []
    for z_chain in z_chains:
        sums = []
        for pos, z in enumerate(z_chain):
            sp = _softplus2(z)
            if mask is not None and pos == 0:
                sp = jnp.where(mask, sp, 0.0)
            hi, lo = _split_bf16(sp)
            sums.append((jnp.dot(stacked(hi, lo, slice(half, None)), w, preferred_element_type=F32),
                         jnp.dot(stacked(hi, lo, slice(None, half)), w, preferred_element_type=F32)))
        sum_chains.append(sums)
    outs = []
    for g, (z_chain, sums, v_chain, run) in enumerate(zip(z_chains, sum_chains, v_chains, runs)):
        weights = []
        for pos, (z, (cs_r, cs_l)) in enumerate(zip(z_chain, sums)):
            if valid_chains is not None and valid_chains[g][pos] is not None:
                run = jnp.where(valid_chains[g][pos], run, SB_FINISHED_RUN)
            a_r = jnp.exp2(z[:, half:] - cs_r[:, :half] - run)
            run = run + cs_r[:, half:]
            a_l = jnp.exp2(z[:, :half] - cs_l[:, :half] - run)
            run = run + cs_l[:, half:]
            a = jnp.concatenate([a_l, a_r], axis=1)
            if mask is not None and pos == 0:
                a = jnp.where(mask, a, 0.0)
            weights.append(a.astype(BF16))
        a_all = weights[0] if len(weights) == 1 else jnp.concatenate(weights, axis=1)
        v_all = v_chain[0] if len(v_chain) == 1 else jnp.concatenate(v_chain, axis=0)
        outs.append((jnp.dot(a_all, v_all, preferred_element_type=F32), run))
    return outs


def _sb_consts():
    j = lax.broadcasted_iota(jnp.int32, (V7X_LANES, V7X_LANES), 0)
    s = lax.broadcasted_iota(jnp.int32, (V7X_LANES, V7X_LANES), 1)
    half = jnp.concatenate([(j >= s).astype(BF16), jnp.ones((V7X_LANES, V7X_LANES), BF16)], axis=1)
    return jnp.concatenate([half, half], axis=0)


def _causal_mask(rows):
    t_idx = lax.broadcasted_iota(jnp.int32, (rows, SB_BLOCK), 0)
    s_idx = lax.broadcasted_iota(jnp.int32, (rows, SB_BLOCK), 1)
    return s_idx < t_idx


def _sb_prompt_kernel(q_ref, k_ref, v_ref, zg_ref, w_ref, o_ref, acc_ref, run_ref, *, tq):
    i = pl.program_id(2)
    groups = tq // SB_BLOCK
    w = w_ref[...]

    def keys(first):
        rows = pl.ds(pl.multiple_of(first, SB_BLOCK), SB_BLOCK)
        return k_ref[0, rows, :].astype(BF16), v_ref[0, rows, :].astype(BF16)

    group_rows = [slice(g * SB_BLOCK, (g + 1) * SB_BLOCK) for g in range(groups)]
    queries = lambda: [q_ref[0, rows, :] for rows in group_rows]
    blks = [i * groups + g for g in range(groups)]
    kvs = [(keys(blk * SB_BLOCK), keys(jnp.maximum(blk - 1, 0) * SB_BLOCK)) for blk in blks]
    outs = _sb_blocks(queries(), [[kv[0][0], kv[1][0]] for kv in kvs], [[kv[0][1], kv[1][1]] for kv in kvs], w,
                      [jnp.zeros((SB_BLOCK, V7X_LANES), F32)] * groups, mask=_causal_mask(SB_BLOCK),
                      valid_chains=[[None, blk >= 1] for blk in blks])
    for rows, (pv, run) in zip(group_rows, outs):
        acc_ref[rows, :] = pv
        run_ref[rows, :] = run

    def pending(d):
        low = jnp.float32(jnp.inf)
        for g, rows in enumerate(group_rows):
            low = jnp.minimum(low, jnp.where(i * groups + g - d >= 0, jnp.min(run_ref[rows, :]), jnp.inf))
        return low

    def more(carry):
        return carry[1] < SB_UNDERFLOW_LOG2

    def diagonal(carry):
        d = carry[0]
        blks = [i * groups + g - d for g in range(groups)]
        kvs = [keys(jnp.maximum(blk, 0) * SB_BLOCK) for blk in blks]
        outs = _sb_blocks(queries(), [[kv[0]] for kv in kvs], [[kv[1]] for kv in kvs], w,
                          [run_ref[rows, :] for rows in group_rows],
                          valid_chains=[[blk >= 0] for blk in blks])
        for rows, (pv, run) in zip(group_rows, outs):
            acc_ref[rows, :] += pv
            run_ref[rows, :] = run
        return d + 1, pending(d + 1)

    lax.while_loop(more, diagonal, (jnp.int32(2), pending(2)))
    o_ref[0] = (acc_ref[...] * _silu(zg_ref[0].astype(F32))).astype(o_ref.dtype)


def _sb_prompt(qz, k, v, tq):
    _, b, t, _ = qz.shape
    assert t % tq == 0 and tq % SB_BLOCK == 0
    tile = pl.BlockSpec((1, tq, SB_HEAD_DIM), lambda bi, h, i: (bi, i, h))
    stacked = lambda which: pl.BlockSpec((None, 1, tq, SB_HEAD_DIM), lambda bi, h, i: (which, bi, i, h))
    whole = pl.BlockSpec((1, t, SB_HEAD_DIM), lambda bi, h, i: (bi, 0, h))
    return pl.pallas_call(
        functools.partial(_sb_prompt_kernel, tq=tq),
        grid=(b, SB_HEADS, t // tq),
        in_specs=[stacked(0), whole, whole, stacked(1),
                  pl.BlockSpec((SB_BLOCK, SB_BLOCK), lambda bi, h, i: (0, 0))],
        out_specs=tile,
        out_shape=jax.ShapeDtypeStruct(qz.shape[1:], BF16),
        scratch_shapes=[pltpu.VMEM((tq, SB_HEAD_DIM), F32), pltpu.VMEM((tq, V7X_LANES), F32)],
        compiler_params=_params(("arbitrary", "arbitrary", "arbitrary")),
        name="sb_prompt",
    )(qz, k, v, qz, _sb_consts())


def _sb_sample_kernel(q_ref, zg_ref, kn_ref, vn_ref, kc_hbm, vc_hbm, w_ref, o_ref,
                      kbuf, vbuf, sem, acc_ref, run_ref, *, n_blocks):
    bi = pl.program_id(0)
    t_new = q_ref.shape[1]
    block_rows = SB_BLOCK * SB_HEADS
    w = w_ref[...]

    def fetch(j, slot):
        src = pl.ds(j * block_rows, block_rows)
        return (pltpu.make_async_copy(kc_hbm.at[bi, src, :], kbuf.at[slot], sem.at[0, slot]),
                pltpu.make_async_copy(vc_hbm.at[bi, src, :], vbuf.at[slot], sem.at[1, slot]))

    for copy in fetch(n_blocks - 1, 0):
        copy.start()

    def head(h):
        return slice(h * t_new, (h + 1) * t_new), slice(h * SB_HEAD_DIM, (h + 1) * SB_HEAD_DIM)

    mask = _causal_mask(t_new)
    padding = jnp.zeros((SB_BLOCK - t_new, SB_HEAD_DIM), BF16)
    heads = [head(h) for h in range(SB_HEADS)]
    padded = lambda ref, cols: jnp.concatenate([ref[0, :, cols].astype(BF16), padding], axis=0)
    outs = _sb_blocks([q_ref[0, :, cols] for _, cols in heads], [[padded(kn_ref, cols)] for _, cols in heads],
                      [[padded(vn_ref, cols)] for _, cols in heads], w,
                      [jnp.zeros((t_new, V7X_LANES), F32)] * SB_HEADS, mask=mask)
    for (rows, _), (pv, run) in zip(heads, outs):
        acc_ref[rows, :] = pv
        run_ref[rows, :] = run

    def more(carry):
        return (carry[0] >= 0) & (carry[1] < SB_UNDERFLOW_LOG2)

    def past_block(carry):
        j = carry[0]
        slot = (n_blocks - 1 - j) % 2
        for copy in fetch(j, slot):
            copy.wait()

        @pl.when(j > 0)
        def _():
            for copy in fetch(j - 1, 1 - slot):
                copy.start()

        keys = [pl.ds(h, SB_BLOCK, stride=SB_HEADS) for h in range(SB_HEADS)]
        outs = _sb_blocks([q_ref[0, :, cols] for _, cols in heads],
                          [[kbuf[slot, rows, :].astype(BF16)] for rows in keys],
                          [[vbuf[slot, rows, :].astype(BF16)] for rows in keys], w,
                          [run_ref[rows, :] for rows, _ in heads])
        for (rows, _), (pv, run) in zip(heads, outs):
            acc_ref[rows, :] += pv
            run_ref[rows, :] = run
        return j - 1, jnp.min(run_ref[...])

    j_end, _ = lax.while_loop(more, past_block, (jnp.int32(n_blocks - 1), jnp.min(run_ref[...])))

    @pl.when(j_end >= 0)
    def _():
        for copy in fetch(j_end, (n_blocks - 1 - j_end) % 2):
            copy.wait()

    for h in range(SB_HEADS):
        rows, cols = head(h)
        o_ref[0, :, cols] = (acc_ref[rows, :] * _silu(zg_ref[0, :, cols].astype(F32))).astype(o_ref.dtype)


def _sb_sample(qz, k_new, v_new, k_past, v_past):
    _, b, t, d = qz.shape
    n_blocks = k_past.shape[1] // (SB_BLOCK * SB_HEADS)
    assert k_past.shape[1] == n_blocks * SB_BLOCK * SB_HEADS and n_blocks >= 1 and t <= V7X_LANES
    new = pl.BlockSpec((1, t, d), lambda bi: (bi, 0, 0))
    stacked = lambda which: pl.BlockSpec((None, 1, t, d), lambda bi: (which, bi, 0, 0))
    return pl.pallas_call(
        functools.partial(_sb_sample_kernel, n_blocks=n_blocks),
        grid=(b,),
        in_specs=[stacked(0), stacked(1), new, new,
                  pl.BlockSpec(memory_space=pl.ANY), pl.BlockSpec(memory_space=pl.ANY),
                  pl.BlockSpec((SB_BLOCK, SB_BLOCK), lambda bi: (0, 0))],
        out_specs=new,
        out_shape=jax.ShapeDtypeStruct(qz.shape[1:], BF16),
        scratch_shapes=[pltpu.VMEM((2, SB_BLOCK * SB_HEADS, SB_HEAD_DIM), F32),
                        pltpu.VMEM((2, SB_BLOCK * SB_HEADS, SB_HEAD_DIM), F32),
                        pltpu.SemaphoreType.DMA((2, 2)),
                        pltpu.VMEM((SB_HEADS * t, SB_HEAD_DIM), F32),
                        pltpu.VMEM((SB_HEADS * t, V7X_LANES), F32)],
        compiler_params=_params(("arbitrary",)),
        name="sb_sample",
    )(qz, qz, k_new, v_new, k_past, v_past, _sb_consts())


def _gla_kernel(*refs, chunk, n_chunks, has_state):
    if has_state:
        q_ref, k_ref, v_ref, dec_ref, r_ref, gn_ref, tril_ref, s0_ref, o_ref, sf_ref, st_ref, upd_ref = refs
    else:
        q_ref, k_ref, v_ref, dec_ref, r_ref, gn_ref, tril_ref, o_ref, sf_ref, st_ref, upd_ref = refs
    step = pl.program_id(2)

    @pl.when(step == 0)
    def _():
        st_ref[...] = s0_ref[0, 0].T if has_state else jnp.zeros_like(st_ref)

    tril2 = tril_ref[...]
    causal = (lax.broadcasted_iota(jnp.int32, (chunk, chunk), 1)
              <= lax.broadcasted_iota(jnp.int32, (chunk, chunk), 0))
    q_scale = GLA_DK ** -0.5
    nt_dot = lambda a, b: lax.dot_general(a, b, (((1,), (1,)), ((), ())), preferred_element_type=F32)
    chunks = [pl.ds(c * chunk, chunk) for c in range(n_chunks)]

    q_in, k_out, k_end, decay = [], [], [], []
    for rows in chunks:
        g_hi, g_lo = _split_bf16(dec_ref[0, rows, :])
        b = jnp.dot(tril2, jnp.concatenate([g_hi, g_lo], axis=0), preferred_element_type=F32)
        b_last = b[chunk - 1:chunk, :]
        k = k_ref[0, rows, :]
        q_in.append((q_ref[0, rows, :] * q_scale * jnp.exp(b)).astype(BF16))
        k_out.append((k * jnp.exp(-b)).astype(BF16))
        k_end.append((k * jnp.exp(b_last - b)).astype(BF16))
        decay.append(jnp.exp(b_last))
    intra = []
    for c, rows in enumerate(chunks):
        v = v_ref[0, rows, :]
        a = jnp.where(causal, nt_dot(q_in[c], k_out[c]), 0.0).astype(BF16)
        intra.append(jnp.dot(a, v, preferred_element_type=F32))
        upd_ref[c] = lax.dot_general(v, k_end[c], (((0,), (0,)), ((), ())), preferred_element_type=F32)
    for c, rows in enumerate(chunks):
        state_t = st_ref[...]
        o = intra[c] + nt_dot(q_in[c], state_t.astype(BF16))
        st_ref[...] = state_t * decay[c] + upd_ref[c]
        o = o * lax.rsqrt(jnp.mean(o * o, axis=-1, keepdims=True) + EPS) * gn_ref[...]
        o_ref[0, rows, :] = (o * _silu(r_ref[0, rows, :].astype(F32))).astype(o_ref.dtype)

    @pl.when(step == pl.num_programs(2) - 1)
    def _():
        sf_ref[0, 0] = st_ref[...].T


def _gla(qk, vr, dec, gn, s0, *, chunk, tt):
    _, b, t, _ = vr.shape
    assert t % tt == 0 and tt % chunk == 0
    tril = (lax.broadcasted_iota(jnp.int32, (chunk, chunk), 1)
            <= lax.broadcasted_iota(jnp.int32, (chunk, chunk), 0)).astype(BF16)
    tril2 = jnp.concatenate([tril, tril], axis=1)
    n_chunks = tt // chunk
    kspec = lambda off: pl.BlockSpec((1, tt, GLA_DK), lambda bi, h, s: (bi, s, h + off))
    vspec = pl.BlockSpec((1, tt, GLA_DV), lambda bi, h, s: (bi, s, h))
    sspec = pl.BlockSpec((1, 1, GLA_DK, GLA_DV), lambda bi, h, s: (bi, h, 0, 0))
    stacked = lambda which: pl.BlockSpec((None, 1, tt, GLA_DV), lambda bi, h, s: (which, bi, s, h))
    in_specs = [kspec(0), kspec(GLA_HEADS), stacked(0), kspec(0), stacked(1),
                pl.BlockSpec((1, GLA_DV), lambda bi, h, s: (0, h)),
                pl.BlockSpec((chunk, 2 * chunk), lambda bi, h, s: (0, 0))]
    args = [qk, qk, vr, dec, vr, gn.reshape(1, -1), tril2]
    if s0 is not None:
        in_specs.append(sspec)
        args.append(s0)
    return pl.pallas_call(
        functools.partial(_gla_kernel, chunk=chunk, n_chunks=n_chunks, has_state=s0 is not None),
        grid=(b, GLA_HEADS, t // tt),
        in_specs=in_specs,
        out_specs=[vspec, sspec],
        out_shape=[jax.ShapeDtypeStruct(vr.shape[1:], BF16),
                   jax.ShapeDtypeStruct((b, GLA_HEADS, GLA_DK, GLA_DV), F32)],
        scratch_shapes=[pltpu.VMEM((GLA_DV, GLA_DK), F32), pltpu.VMEM((n_chunks, GLA_DV, GLA_DK), F32)],
        compiler_params=_params(("arbitrary", "arbitrary", "arbitrary")),
        name="gla",
    )(*args)


def _outproj_kernel(y_ref, w_ref, x_ref, gate_ref, *rest, final_norm):
    if final_norm:
        gf_ref, o_ref = rest
    else:
        (o_ref,) = rest
    y = jnp.dot(y_ref[0], w_ref[...], preferred_element_type=F32)
    x = x_ref[0] + gate_ref[0] * y
    if final_norm:
        x = x * lax.rsqrt(jnp.mean(x * x, axis=-1, keepdims=True) + EPS) * gf_ref[...]
    o_ref[0] = x


def _outproj(y, w, x, gate, gf, *, tm):
    b, t, d = x.shape
    kdim = y.shape[-1]
    assert t % tm == 0
    gate_spec = (pl.BlockSpec((1, 1, d), lambda bi, i: (bi, 0, 0)) if gate.shape[1] == 1
                 else pl.BlockSpec((1, tm, d), lambda bi, i: (bi, i, 0)))
    in_specs = [pl.BlockSpec((1, tm, kdim), lambda bi, i: (bi, i, 0)),
                pl.BlockSpec((kdim, d), lambda bi, i: (0, 0)),
                pl.BlockSpec((1, tm, d), lambda bi, i: (bi, i, 0)),
                gate_spec]
    args = [y, w, x, gate]
    if gf is not None:
        in_specs.append(pl.BlockSpec((1, d), lambda bi, i: (0, 0)))
        args.append(gf.reshape(1, d))
    return pl.pallas_call(
        functools.partial(_outproj_kernel, final_norm=gf is not None),
        grid=(b, t // tm),
        in_specs=in_specs,
        out_specs=pl.BlockSpec((1, tm, d), lambda bi, i: (bi, i, 0)),
        out_shape=jax.ShapeDtypeStruct(x.shape, F32),
        compiler_params=_params(("arbitrary", "arbitrary")),
        name="outproj",
    )(*args)


def _trunk(x, mods, weights, *, per_row, tm_in, tn_in, tm_out, gla_tt, cache=None, state=None):
    (norm_g, sb_w_in, sb_w_out, gla_w_in, gla_wa, gla_wa2, gla_b_a, gla_norm_g, gla_w_out,
     final_norm_g) = weights
    b, t, d = x.shape
    if per_row:
        fold = lambda a: a.reshape(1, b * t, a.shape[-1])
        rows = lambda m: jnp.broadcast_to(m[:, None, :], (b, t, d)).reshape(1, b * t, d)
    else:
        fold = lambda a: a
        rows = lambda m: m[:, None, :]
    unfold = lambda a: a.reshape(b, t, a.shape[-1])
    width = SB_HEADS * SB_HEAD_DIM

    shift, scale, gate = mods[0]
    assert width == d
    k, v, qz = _inproj(fold(x), rows(scale), rows(shift), norm_g[0], sb_w_in, (1, 2, 0, 3), 2,
                       (SB_HEAD_DIM ** -0.5 * math.log2(math.e), None), tm=tm_in, tn=tn_in)
    k, v, qz = unfold(k), unfold(v), qz.reshape(2, b, t, d)
    if cache is None:
        branch = _sb_prompt(qz, k, v, tq=min(SB_QUERY_TILE, t))
    else:
        branch = _sb_sample(qz, k, v, cache[0], cache[1])
    x1 = _outproj(fold(branch), sb_w_out, fold(x), rows(gate), None, tm=tm_out)

    shift, scale, gate = mods[1]
    nk, nv = GLA_HEADS * GLA_DK, GLA_HEADS * GLA_DV
    assert 2 * nk == d and nv == d
    qk, vr, dec = _inproj(x1, rows(scale), rows(shift), norm_g[1], gla_w_in, (0, 1, 2), 1, (None, None),
                          tm=tm_in, tn=tn_in, decay=(gla_wa, gla_wa2, gla_b_a))
    chunk = min(GLA_CHUNK, t)
    branch, s_new = _gla(unfold(qk), vr.reshape(2, b, t, d), unfold(dec), gla_norm_g, state,
                         chunk=chunk, tt=min(gla_tt, t))
    y = _outproj(fold(branch), gla_w_out, x1, rows(gate), final_norm_g, tm=tm_out)
    return unfold(y), k, v, s_new


def kernel(x_prompt, x_sample, cache_sb_k, cache_sb_v, state_gla, c_prompt, c_sample, w_ada, b_ada, norm_g,
           sb_w_in, sb_w_out, gla_w_in, gla_w_a2, gla_b_a, gla_norm_g, gla_w_out, final_norm_g):
    depth, d, _ = w_ada.shape
    assert depth == 2 and sb_w_in.shape[0] == 1 and gla_w_in.shape[0] == 1
    bp, tp, _ = x_prompt.shape
    bs, ts, _ = x_sample.shape
    past = cache_sb_k.shape[2]
    nk, nv = GLA_HEADS * GLA_DK, GLA_HEADS * GLA_DV
    main = 2 * nk + 2 * nv

    mod = _ada(jnp.concatenate([c_prompt, c_sample], axis=0), w_ada, b_ada)
    split = lambda m: (m[:, :d], m[:, d:2 * d], m[:, 2 * d:])
    mods_p = [split(mod[l, :bp]) for l in range(depth)]
    mods_s = [split(mod[l, bp:]) for l in range(depth)]

    gla_w = gla_w_in[0]
    gla_wa = jnp.pad(gla_w[:, main:], ((0, 0), (0, V7X_LANES - GLA_GATE_RANK))).astype(BF16)
    gla_wa2 = jnp.pad(gla_w_a2[0], ((0, V7X_LANES - GLA_GATE_RANK), (0, 0))).astype(BF16)
    weights = (norm_g, sb_w_in[0].astype(BF16), sb_w_out[0].astype(BF16), gla_w.astype(BF16),
               gla_wa, gla_wa2, gla_b_a[0], gla_norm_g[0], gla_w_out[0].astype(BF16), final_norm_g)

    y_p, k_p, v_p, s_p = _trunk(x_prompt, mods_p, weights, per_row=False,
                                tm_in=min(INPROJ_ROW_TILE, tp), tn_in=INPROJ_COL_TILE,
                                tm_out=min(OUTPROJ_ROW_TILE, tp), gla_tt=GLA_TOKEN_TILE)
    cache = (cache_sb_k[0].reshape(bs, past * SB_HEADS, SB_HEAD_DIM),
             cache_sb_v[0].reshape(bs, past * SB_HEADS, SB_HEAD_DIM))
    y_s, k_s, v_s, s_s = _trunk(x_sample, mods_s, weights, per_row=True,
                                tm_in=bs * ts, tn_in=d, tm_out=bs * ts, gla_tt=ts, cache=cache,
                                state=state_gla[0])

    heads = lambda a: a.reshape(1, a.shape[0], a.shape[1], SB_HEADS, SB_HEAD_DIM)
    return (y_p, y_s, heads(k_p), heads(v_p), heads(k_s), heads(v_s), s_p[None], s_s[None])
```

```python
import functools
import math

import jax
import jax.numpy as jnp
from jax import lax
from jax.experimental import pallas as pl
from jax.experimental.pallas import tpu as pltpu

F32 = jnp.float32
BF16 = jnp.bfloat16

SB_HEADS = 16
SB_HEAD_DIM = 128
GLA_HEADS = 4
GLA_DK = 256
GLA_DV = 512
GLA_GATE_RANK = 16
GLA_TAU = 16.0
GLA_CHUNK = 64
EPS = 1e-6

V7X_LANES = 128
V7X_VMEM_LIMIT_BYTES = 56 * 1024 * 1024
ADA_COL_TILE = 512
INPROJ_ROW_TILE = 1024
INPROJ_COL_TILE = 1024
OUTPROJ_ROW_TILE = 512
GLA_TOKEN_TILE = 1024
SB_BLOCK = 256
SB_QUERY_TILE = 4096
SB_UNDERFLOW_LOG2 = 160.0
SB_FINISHED_RUN = 1e30
NORM_ROWS = 16
NORM_UNROLL = 4


def _params(semantics):
    return pltpu.CompilerParams(dimension_semantics=semantics,
                                vmem_limit_bytes=V7X_VMEM_LIMIT_BYTES)


def _split_bf16(x):
    hi = x.astype(BF16)
    lo = (x - hi.astype(F32)).astype(BF16)
    return hi, lo


def _log_sigmoid(z):
    return jnp.minimum(z, 0.0) - jnp.log(1.0 + jnp.exp(-jnp.abs(z)))


def _softplus2(z):
    neg_abs = lax.bitcast_convert_type(lax.bitcast_convert_type(z, jnp.uint32) | jnp.uint32(1 << 31), F32)
    return jnp.maximum(z, 0.0) + jnp.log2(1.0 + jnp.exp2(neg_abs))


def _silu(z):
    return z / (1.0 + jnp.exp(-z))


def _ada_kernel(c_ref, w_ref, b_ref, o_ref):
    acc = jnp.dot(c_ref[...].astype(BF16), w_ref[0].astype(BF16), preferred_element_type=F32)
    o_ref[0] = acc + b_ref[0]


def _ada(c_all, w_ada, b_ada, tn=ADA_COL_TILE):
    depth, d, n = w_ada.shape
    rows = c_all.shape[0]
    return pl.pallas_call(
        _ada_kernel,
        grid=(depth, n // tn),
        in_specs=[pl.BlockSpec((rows, d), lambda l, j: (0, 0)),
                  pl.BlockSpec((1, d, tn), lambda l, j: (l, 0, j)),
                  pl.BlockSpec((1, 1, tn), lambda l, j: (l, 0, j))],
        out_specs=pl.BlockSpec((1, rows, tn), lambda l, j: (l, 0, j)),
        out_shape=jax.ShapeDtypeStruct((depth, rows, n), F32),
        compiler_params=_params(("arbitrary", "arbitrary")),
        name="ada",
    )(c_all, w_ada, b_ada.reshape(depth, 1, n))


def _inproj_kernel(*refs, n_f32, bf16_mults, tm, steps_per_seg, with_decay):
    x_ref, sc_ref, sh_ref, g_ref, w_ref = refs[:5]
    pos = 5
    if with_decay:
        wa_ref, wa2_ref, ba_ref = refs[pos:pos + 3]
        pos += 3
    of_refs = refs[pos:pos + n_f32]
    ob_ref = refs[pos + n_f32]
    pos += n_f32 + 1
    if with_decay:
        dec_ref = refs[pos]
        pos += 1
    h_ref = refs[pos]
    n = pl.program_id(2)
    seg = n // steps_per_seg

    @pl.when(n == 0)
    def _():
        def norm_rows(r, carry):
            rows = pl.ds(pl.multiple_of(r * NORM_ROWS, NORM_ROWS), NORM_ROWS)
            x = x_ref[0, rows, :]
            y = x * lax.rsqrt(jnp.mean(x * x, axis=-1, keepdims=True) + EPS) * g_ref[...]
            if sc_ref.shape[1] == 1:
                sc, sh = sc_ref[0], sh_ref[0]
            else:
                sc, sh = sc_ref[0, rows, :], sh_ref[0, rows, :]
            h_ref[rows, :] = (y * (1.0 + sc) + sh).astype(BF16)
            return carry
        steps = tm // NORM_ROWS
        lax.fori_loop(0, steps, norm_rows, 0, unroll=NORM_UNROLL if steps % NORM_UNROLL == 0 else 1)

    for idx, of_ref in enumerate(of_refs):
        @pl.when(seg == idx)
        def _(of_ref=of_ref):
            of_ref[0] = jnp.dot(h_ref[...], w_ref[...], preferred_element_type=F32)

    @pl.when(seg >= n_f32)
    def _():
        acc = jnp.dot(h_ref[...], w_ref[...], preferred_element_type=F32)
        if any(m is not None for m in bf16_mults):
            mult = jnp.float32(1.0)
            for idx, m in enumerate(bf16_mults):
                if m is not None:
                    mult = jnp.where(seg == n_f32 + idx, jnp.float32(m), mult)
            acc = acc * mult
        ob_ref[0, 0] = acc.astype(BF16)

    if with_decay:
        @pl.when(n == pl.num_programs(2) - 1)
        def _():
            a_lr = jnp.dot(h_ref[...], wa_ref[...], preferred_element_type=F32)
            pre = jnp.dot(a_lr.astype(BF16), wa2_ref[...], preferred_element_type=F32) + ba_ref[...]
            dec_ref[0] = _log_sigmoid(pre) * (1.0 / GLA_TAU)


def _inproj(x, scale, shift, g, w, seg_order, n_f32, bf16_mults, *, tm, tn, decay=None):
    b, t, d = x.shape
    n_seg = len(seg_order)
    n_bf16 = len(bf16_mults)
    sps = d // tn
    assert t % tm == 0 and w.shape[1] >= n_seg * d and n_seg == n_f32 + n_bf16 and d == sps * tn
    assert tm % NORM_ROWS == 0

    def w_block(j):
        seg = jnp.int32(seg_order[-1])
        for k, s in enumerate(seg_order[:-1]):
            seg = jnp.where(j // sps == k, s, seg)
        return seg * sps + j % sps

    mod_rows = scale.shape[1]
    mod_spec = (pl.BlockSpec((1, 1, d), lambda bi, i, j: (bi, 0, 0)) if mod_rows == 1
                else pl.BlockSpec((1, tm, d), lambda bi, i, j: (bi, i, 0)))
    in_specs = [pl.BlockSpec((1, tm, d), lambda bi, i, j: (bi, i, 0)), mod_spec, mod_spec,
                pl.BlockSpec((1, d), lambda bi, i, j: (0, 0)),
                pl.BlockSpec((d, tn), lambda bi, i, j: (0, w_block(j)))]
    args = [x, scale, shift, g.reshape(1, d), w]
    if decay is not None:
        wa, wa2, ba = decay
        in_specs += [pl.BlockSpec(wa.shape, lambda bi, i, j: (0, 0)),
                     pl.BlockSpec(wa2.shape, lambda bi, i, j: (0, 0)),
                     pl.BlockSpec((1, ba.shape[-1]), lambda bi, i, j: (0, 0))]
        args += [wa, wa2, ba.reshape(1, -1)]

    def f32_spec(k):
        return pl.BlockSpec((1, tm, tn), lambda bi, i, j: (bi, i, jnp.clip(j - k * sps, 0, sps - 1)))

    out_specs = [f32_spec(k) for k in range(n_f32)]
    out_specs.append(pl.BlockSpec(
        (1, 1, tm, tn),
        lambda bi, i, j: (jnp.maximum(j // sps - n_f32, 0), bi, i, jnp.where(j // sps >= n_f32, j % sps, 0))))
    out_shape = [jax.ShapeDtypeStruct((b, t, d), F32) for _ in range(n_f32)]
    out_shape.append(jax.ShapeDtypeStruct((n_bf16, b, t, d), BF16))
    if decay is not None:
        nk = decay[1].shape[1]
        out_specs.append(pl.BlockSpec((1, tm, nk), lambda bi, i, j: (bi, i, 0)))
        out_shape.append(jax.ShapeDtypeStruct((b, t, nk), F32))
    return pl.pallas_call(
        functools.partial(_inproj_kernel, n_f32=n_f32, bf16_mults=tuple(bf16_mults), tm=tm,
                          steps_per_seg=sps, with_decay=decay is not None),
        grid=(b, t // tm, n_seg * sps),
        in_specs=in_specs, out_specs=out_specs, out_shape=out_shape,
        scratch_shapes=[pltpu.VMEM((tm, d), BF16)],
        compiler_params=_params(("arbitrary", "arbitrary", "arbitrary")),
        name="inproj",
    )(*args)


def _sb_blocks(qs, k_chains, v_chains, w, runs, mask=None, valid_chains=None):
    half = V7X_LANES
    stacked = lambda hi, lo, cols: jnp.concatenate([hi[:, cols], lo[:, cols]], axis=1)
    z_chains = [[lax.dot_general(q, k_blk, (((1,), (1,)), ((), ())), preferred_element_type=F32)
                 for k_blk in chain] for q, chain in zip(qs, k_chains)]
    sum_chains = []
    for z_chain in z_chains:
        sums = []
        for pos, z in enumerate(z_chain):
            sp = _softplus2(z)
            if mask is not None and pos == 0:
                sp = jnp.where(mask, sp, 0.0)
            hi, lo = _split_bf16(sp)
            sums.append((jnp.dot(stacked(hi, lo, slice(half, None)), w, preferred_element_type=F32),
                         jnp.dot(stacked(hi, lo, slice(None, half)), w, preferred_element_type=F32)))
        sum_chains.append(sums)
    outs = []
    for g, (z_chain, sums, v_chain, run) in enumerate(zip(z_chains, sum_chains, v_chains, runs)):
        weights = []
        for pos, (z, (cs_r, cs_l)) in enumerate(zip(z_chain, sums)):
            if valid_chains is not None and valid_chains[g][pos] is not None:
                run = jnp.where(valid_chains[g][pos], run, SB_FINISHED_RUN)
            a_r = jnp.exp2(z[:, half:] - cs_r[:, :half] - run)
            run = run + cs_r[:, half:]
            a_l = jnp.exp2(z[:, :half] - cs_l[:, :half] - run)
            run = run + cs_l[:, half:]
            a = jnp.concatenate([a_l, a_r], axis=1)
            if mask is not None and pos == 0:
                a = jnp.where(mask, a, 0.0)
            weights.append(a.astype(BF16))
        a_all = weights[0] if len(weights) == 1 else jnp.concatenate(weights, axis=1)
        v_all = v_chain[0] if len(v_chain) == 1 else jnp.concatenate(v_chain, axis=0)
        outs.append((jnp.dot(a_all, v_all, preferred_element_type=F32), run))
    return outs


def _sb_consts():
    j = lax.broadcasted_iota(jnp.int32, (V7X_LANES, V7X_LANES), 0)
    s = lax.broadcasted_iota(jnp.int32, (V7X_LANES, V7X_LANES), 1)
    half = jnp.concatenate([(j >= s).astype(BF16), jnp.ones((V7X_LANES, V7X_LANES), BF16)], axis=1)
    return jnp.concatenate([half, half], axis=0)


def _causal_mask(rows):
    t_idx = lax.broadcasted_iota(jnp.int32, (rows, SB_BLOCK), 0)
    s_idx = lax.broadcasted_iota(jnp.int32, (rows, SB_BLOCK), 1)
    return s_idx < t_idx


def _sb_prompt_kernel(q_ref, k_ref, v_ref, zg_ref, w_ref, o_ref, acc_ref, run_ref, *, tq):
    i = pl.program_id(2)
    groups = tq // SB_BLOCK
    w = w_ref[...]

    def keys(first):
        rows = pl.ds(pl.multiple_of(first, SB_BLOCK), SB_BLOCK)
        return k_ref[0, rows, :].astype(BF16), v_ref[0, rows, :].astype(BF16)

    group_rows = [slice(g * SB_BLOCK, (g + 1) * SB_BLOCK) for g in range(groups)]
    queries = lambda: [q_ref[0, rows, :] for rows in group_rows]
    blks = [i * groups + g for g in range(groups)]
    kvs = [(keys(blk * SB_BLOCK), keys(jnp.maximum(blk - 1, 0) * SB_BLOCK)) for blk in blks]
    outs = _sb_blocks(queries(), [[kv[0][0], kv[1][0]] for kv in kvs], [[kv[0][1], kv[1][1]] for kv in kvs], w,
                      [jnp.zeros((SB_BLOCK, V7X_LANES), F32)] * groups, mask=_causal_mask(SB_BLOCK),
                      valid_chains=[[None, blk >= 1] for blk in blks])
    for rows, (pv, run) in zip(group_rows, outs):
        acc_ref[rows, :] = pv
        run_ref[rows, :] = run

    half = SB_BLOCK // 2
    upper_rows = [slice(g * SB_BLOCK, g * SB_BLOCK + half) for g in range(groups)]
    lower_rows = [slice(g * SB_BLOCK + half, (g + 1) * SB_BLOCK) for g in range(groups)]

    def pending(d, row_sets):
        low = jnp.float32(jnp.inf)
        for g, rows in enumerate(row_sets):
            low = jnp.minimum(low, jnp.where(i * groups + g - d >= 0, jnp.min(run_ref[rows, :]), jnp.inf))
        return low

    def more(carry):
        return jnp.minimum(carry[1], carry[2]) < SB_UNDERFLOW_LOG2

    def diagonal(carry):
        d = carry[0]
        blks = [i * groups + g - d for g in range(groups)]

        def sweep(row_sets):
            kvs = [keys(jnp.maximum(blk, 0) * SB_BLOCK) for blk in blks]
            outs = _sb_blocks([q_ref[0, rows, :] for rows in row_sets], [[kv[0]] for kv in kvs],
                              [[kv[1]] for kv in kvs], w, [run_ref[rows, :] for rows in row_sets],
                              valid_chains=[[blk >= 0] for blk in blks])
            for rows, (pv, run) in zip(row_sets, outs):
                acc_ref[rows, :] += pv
                run_ref[rows, :] = run

        lax.cond(carry[2] < SB_UNDERFLOW_LOG2, lambda: sweep(group_rows), lambda: sweep(upper_rows))
        return d + 1, pending(d + 1, upper_rows), pending(d + 1, lower_rows)

    lax.while_loop(more, diagonal, (jnp.int32(2), pending(2, upper_rows), pending(2, lower_rows)))
    o_ref[0] = (acc_ref[...] * _silu(zg_ref[0].astype(F32))).astype(o_ref.dtype)


def _sb_prompt(qz, k, v, tq):
    _, b, t, _ = qz.shape
    assert t % tq == 0 and tq % SB_BLOCK == 0
    tile = pl.BlockSpec((1, tq, SB_HEAD_DIM), lambda bi, h, i: (bi, i, h))
    stacked = lambda which: pl.BlockSpec((None, 1, tq, SB_HEAD_DIM), lambda bi, h, i: (which, bi, i, h))
    whole = pl.BlockSpec((1, t, SB_HEAD_DIM), lambda bi, h, i: (bi, 0, h))
    return pl.pallas_call(
        functools.partial(_sb_prompt_kernel, tq=tq),
        grid=(b, SB_HEADS, t // tq),
        in_specs=[stacked(0), whole, whole, stacked(1),
                  pl.BlockSpec((SB_BLOCK, SB_BLOCK), lambda bi, h, i: (0, 0))],
        out_specs=tile,
        out_shape=jax.ShapeDtypeStruct(qz.shape[1:], BF16),
        scratch_shapes=[pltpu.VMEM((tq, SB_HEAD_DIM), F32), pltpu.VMEM((tq, V7X_LANES), F32)],
        compiler_params=_params(("arbitrary", "arbitrary", "arbitrary")),
        name="sb_prompt",
    )(qz, k, v, qz, _sb_consts())


def _sb_sample_kernel(q_ref, zg_ref, kn_ref, vn_ref, kc_hbm, vc_hbm, w_ref, o_ref,
                      kbuf, vbuf, sem, acc_ref, run_ref, *, n_blocks):
    bi = pl.program_id(0)
    t_new = q_ref.shape[1]
    block_rows = SB_BLOCK * SB_HEADS
    w = w_ref[...]

    def fetch(j, slot):
        src = pl.ds(j * block_rows, block_rows)
        return (pltpu.make_async_copy(kc_hbm.at[bi, src, :], kbuf.at[slot], sem.at[0, slot]),
                pltpu.make_async_copy(vc_hbm.at[bi, src, :], vbuf.at[slot], sem.at[1, slot]))

    for copy in fetch(n_blocks - 1, 0):
        copy.start()

    def head(h):
        return slice(h * t_new, (h + 1) * t_new), slice(h * SB_HEAD_DIM, (h + 1) * SB_HEAD_DIM)

    mask = _causal_mask(t_new)
    padding = jnp.zeros((SB_BLOCK - t_new, SB_HEAD_DIM), BF16)
    heads = [head(h) for h in range(SB_HEADS)]
    padded = lambda ref, cols: jnp.concatenate([ref[0, :, cols].astype(BF16), padding], axis=0)
    outs = _sb_blocks([q_ref[0, :, cols] for _, cols in heads], [[padded(kn_ref, cols)] for _, cols in heads],
                      [[padded(vn_ref, cols)] for _, cols in heads], w,
                      [jnp.zeros((t_new, V7X_LANES), F32)] * SB_HEADS, mask=mask)
    for (rows, _), (pv, run) in zip(heads, outs):
        acc_ref[rows, :] = pv
        run_ref[rows, :] = run

    def more(carry):
        return (carry[0] >= 0) & (carry[1] < SB_UNDERFLOW_LOG2)

    def past_block(carry):
        j = carry[0]
        slot = (n_blocks - 1 - j) % 2
        for copy in fetch(j, slot):
            copy.wait()

        @pl.when(j > 0)
        def _():
            for copy in fetch(j - 1, 1 - slot):
                copy.start()

        keys = [pl.ds(h, SB_BLOCK, stride=SB_HEADS) for h in range(SB_HEADS)]
        outs = _sb_blocks([q_ref[0, :, cols] for _, cols in heads],
                          [[kbuf[slot, rows, :].astype(BF16)] for rows in keys],
                          [[vbuf[slot, rows, :].astype(BF16)] for rows in keys], w,
                          [run_ref[rows, :] for rows, _ in heads])
        for (rows, _), (pv, run) in zip(heads, outs):
            acc_ref[rows, :] += pv
            run_ref[rows, :] = run
        return j - 1, jnp.min(run_ref[...])

    j_end, _ = lax.while_loop(more, past_block, (jnp.int32(n_blocks - 1), jnp.min(run_ref[...])))

    @pl.when(j_end >= 0)
    def _():
        for copy in fetch(j_end, (n_blocks - 1 - j_end) % 2):
            copy.wait()

    for h in range(SB_HEADS):
        rows, cols = head(h)
        o_ref[0, :, cols] = (acc_ref[rows, :] * _silu(zg_ref[0, :, cols].astype(F32))).astype(o_ref.dtype)


def _sb_sample(qz, k_new, v_new, k_past, v_past):
    _, b, t, d = qz.shape
    n_blocks = k_past.shape[1] // (SB_BLOCK * SB_HEADS)
    assert k_past.shape[1] == n_blocks * SB_BLOCK * SB_HEADS and n_blocks >= 1 and t <= V7X_LANES
    new = pl.BlockSpec((1, t, d), lambda bi: (bi, 0, 0))
    stacked = lambda which: pl.BlockSpec((None, 1, t, d), lambda bi: (which, bi, 0, 0))
    return pl.pallas_call(
        functools.partial(_sb_sample_kernel, n_blocks=n_blocks),
        grid=(b,),
        in_specs=[stacked(0), stacked(1), new, new,
                  pl.BlockSpec(memory_space=pl.ANY), pl.BlockSpec(memory_space=pl.ANY),
                  pl.BlockSpec((SB_BLOCK, SB_BLOCK), lambda bi: (0, 0))],
        out_specs=new,
        out_shape=jax.ShapeDtypeStruct(qz.shape[1:], BF16),
        scratch_shapes=[pltpu.VMEM((2, SB_BLOCK * SB_HEADS, SB_HEAD_DIM), F32),
                        pltpu.VMEM((2, SB_BLOCK * SB_HEADS, SB_HEAD_DIM), F32),
                        pltpu.SemaphoreType.DMA((2, 2)),
                        pltpu.VMEM((SB_HEADS * t, SB_HEAD_DIM), F32),
                        pltpu.VMEM((SB_HEADS * t, V7X_LANES), F32)],
        compiler_params=_params(("arbitrary",)),
        name="sb_sample",
    )(qz, qz, k_new, v_new, k_past, v_past, _sb_consts())


def _gla_kernel(*refs, chunk, n_chunks, has_state):
    if has_state:
        q_ref, k_ref, v_ref, dec_ref, r_ref, gn_ref, tril_ref, s0_ref, o_ref, sf_ref, st_ref, upd_ref = refs
    else:
        q_ref, k_ref, v_ref, dec_ref, r_ref, gn_ref, tril_ref, o_ref, sf_ref, st_ref, upd_ref = refs
    step = pl.program_id(2)

    @pl.when(step == 0)
    def _():
        st_ref[...] = s0_ref[0, 0].T if has_state else jnp.zeros_like(st_ref)

    tril2 = tril_ref[...]
    causal = (lax.broadcasted_iota(jnp.int32, (chunk, chunk), 1)
              <= lax.broadcasted_iota(jnp.int32, (chunk, chunk), 0))
    q_scale = GLA_DK ** -0.5
    nt_dot = lambda a, b: lax.dot_general(a, b, (((1,), (1,)), ((), ())), preferred_element_type=F32)
    chunks = [pl.ds(c * chunk, chunk) for c in range(n_chunks)]

    q_in, k_out, k_end, decay = [], [], [], []
    for rows in chunks:
        g_hi, g_lo = _split_bf16(dec_ref[0, rows, :])
        b = jnp.dot(tril2, jnp.concatenate([g_hi, g_lo], axis=0), preferred_element_type=F32)
        b_last = b[chunk - 1:chunk, :]
        k = k_ref[0, rows, :]
        q_in.append((q_ref[0, rows, :] * q_scale * jnp.exp(b)).astype(BF16))
        k_out.append((k * jnp.exp(-b)).astype(BF16))
        k_end.append((k * jnp.exp(b_last - b)).astype(BF16))
        decay.append(jnp.exp(b_last))
    intra = []
    for c, rows in enumerate(chunks):
        v = v_ref[0, rows, :]
        a = jnp.where(causal, nt_dot(q_in[c], k_out[c]), 0.0).astype(BF16)
        intra.append(jnp.dot(a, v, preferred_element_type=F32))
        upd_ref[c] = lax.dot_general(v, k_end[c], (((0,), (0,)), ((), ())), preferred_element_type=F32)
    for c, rows in enumerate(chunks):
        state_t = st_ref[...]
        o = intra[c] + nt_dot(q_in[c], state_t.astype(BF16))
        st_ref[...] = state_t * decay[c] + upd_ref[c]
        o = o * lax.rsqrt(jnp.mean(o * o, axis=-1, keepdims=True) + EPS) * gn_ref[...]
        o_ref[0, rows, :] = (o * _silu(r_ref[0, rows, :].astype(F32))).astype(o_ref.dtype)

    @pl.when(step == pl.num_programs(2) - 1)
    def _():
        sf_ref[0, 0] = st_ref[...].T


def _gla(qk, vr, dec, gn, s0, *, chunk, tt):
    _, b, t, _ = vr.shape
    assert t % tt == 0 and tt % chunk == 0
    tril = (lax.broadcasted_iota(jnp.int32, (chunk, chunk), 1)
            <= lax.broadcasted_iota(jnp.int32, (chunk, chunk), 0)).astype(BF16)
    tril2 = jnp.concatenate([tril, tril], axis=1)
    n_chunks = tt // chunk
    kspec = lambda off: pl.BlockSpec((1, tt, GLA_DK), lambda bi, h, s: (bi, s, h + off))
    vspec = pl.BlockSpec((1, tt, GLA_DV), lambda bi, h, s: (bi, s, h))
    sspec = pl.BlockSpec((1, 1, GLA_DK, GLA_DV), lambda bi, h, s: (bi, h, 0, 0))
    stacked = lambda which: pl.BlockSpec((None, 1, tt, GLA_DV), lambda bi, h, s: (which, bi, s, h))
    in_specs = [kspec(0), kspec(GLA_HEADS), stacked(0), kspec(0), stacked(1),
                pl.BlockSpec((1, GLA_DV), lambda bi, h, s: (0, h)),
                pl.BlockSpec((chunk, 2 * chunk), lambda bi, h, s: (0, 0))]
    args = [qk, qk, vr, dec, vr, gn.reshape(1, -1), tril2]
    if s0 is not None:
        in_specs.append(sspec)
        args.append(s0)
    return pl.pallas_call(
        functools.partial(_gla_kernel, chunk=chunk, n_chunks=n_chunks, has_state=s0 is not None),
        grid=(b, GLA_HEADS, t // tt),
        in_specs=in_specs,
        out_specs=[vspec, sspec],
        out_shape=[jax.ShapeDtypeStruct(vr.shape[1:], BF16),
                   jax.ShapeDtypeStruct((b, GLA_HEADS, GLA_DK, GLA_DV), F32)],
        scratch_shapes=[pltpu.VMEM((GLA_DV, GLA_DK), F32), pltpu.VMEM((n_chunks, GLA_DV, GLA_DK), F32)],
        compiler_params=_params(("arbitrary", "arbitrary", "arbitrary")),
        name="gla",
    )(*args)


def _outproj_kernel(y_ref, w_ref, x_ref, gate_ref, *rest, final_norm):
    if final_norm:
        gf_ref, o_ref = rest
    else:
        (o_ref,) = rest
    y = jnp.dot(y_ref[0], w_ref[...], preferred_element_type=F32)
    x = x_ref[0] + gate_ref[0] * y
    if final_norm:
        x = x * lax.rsqrt(jnp.mean(x * x, axis=-1, keepdims=True) + EPS) * gf_ref[...]
    o_ref[0] = x


def _outproj(y, w, x, gate, gf, *, tm):
    b, t, d = x.shape
    kdim = y.shape[-1]
    assert t % tm == 0
    gate_spec = (pl.BlockSpec((1, 1, d), lambda bi, i: (bi, 0, 0)) if gate.shape[1] == 1
                 else pl.BlockSpec((1, tm, d), lambda bi, i: (bi, i, 0)))
    in_specs = [pl.BlockSpec((1, tm, kdim), lambda bi, i: (bi, i, 0)),
                pl.BlockSpec((kdim, d), lambda bi, i: (0, 0)),
                pl.BlockSpec((1, tm, d), lambda bi, i: (bi, i, 0)),
                gate_spec]
    args = [y, w, x, gate]
    if gf is not None:
        in_specs.append(pl.BlockSpec((1, d), lambda bi, i: (0, 0)))
        args.append(gf.reshape(1, d))
    return pl.pallas_call(
        functools.partial(_outproj_kernel, final_norm=gf is not None),
        grid=(b, t // tm),
        in_specs=in_specs,
        out_specs=pl.BlockSpec((1, tm, d), lambda bi, i: (bi, i, 0)),
        out_shape=jax.ShapeDtypeStruct(x.shape, F32),
        compiler_params=_params(("arbitrary", "arbitrary")),
        name="outproj",
    )(*args)


def _trunk(x, mods, weights, *, per_row, tm_in, tn_in, tm_out, gla_tt, cache=None, state=None):
    (norm_g, sb_w_in, sb_w_out, gla_w_in, gla_wa, gla_wa2, gla_b_a, gla_norm_g, gla_w_out,
     final_norm_g) = weights
    b, t, d = x.shape
    if per_row:
        fold = lambda a: a.reshape(1, b * t, a.shape[-1])
        rows = lambda m: jnp.broadcast_to(m[:, None, :], (b, t, d)).reshape(1, b * t, d)
    else:
        fold = lambda a: a
        rows = lambda m: m[:, None, :]
    unfold = lambda a: a.reshape(b, t, a.shape[-1])
    width = SB_HEADS * SB_HEAD_DIM

    shift, scale, gate = mods[0]
    assert width == d
    k, v, qz = _inproj(fold(x), rows(scale), rows(shift), norm_g[0], sb_w_in, (1, 2, 0, 3), 2,
                       (SB_HEAD_DIM ** -0.5 * math.log2(math.e), None), tm=tm_in, tn=tn_in)
    k, v, qz = unfold(k), unfold(v), qz.reshape(2, b, t, d)
    if cache is None:
        branch = _sb_prompt(qz, k, v, tq=min(SB_QUERY_TILE, t))
    else:
        branch = _sb_sample(qz, k, v, cache[0], cache[1])
    x1 = _outproj(fold(branch), sb_w_out, fold(x), rows(gate), None, tm=tm_out)

    shift, scale, gate = mods[1]
    nk, nv = GLA_HEADS * GLA_DK, GLA_HEADS * GLA_DV
    assert 2 * nk == d and nv == d
    qk, vr, dec = _inproj(x1, rows(scale), rows(shift), norm_g[1], gla_w_in, (0, 1, 2), 1, (None, None),
                          tm=tm_in, tn=tn_in, decay=(gla_wa, gla_wa2, gla_b_a))
    chunk = min(GLA_CHUNK, t)
    branch, s_new = _gla(unfold(qk), vr.reshape(2, b, t, d), unfold(dec), gla_norm_g, state,
                         chunk=chunk, tt=min(gla_tt, t))
    y = _outproj(fold(branch), gla_w_out, x1, rows(gate), final_norm_g, tm=tm_out)
    return unfold(y), k, v, s_new


def kernel(x_prompt, x_sample, cache_sb_k, cache_sb_v, state_gla, c_prompt, c_sample, w_ada, b_ada, norm_g,
           sb_w_in, sb_w_out, gla_w_in, gla_w_a2, gla_b_a, gla_norm_g, gla_w_out, final_norm_g):
    depth, d, _ = w_ada.shape
    assert depth == 2 and sb_w_in.shape[0] == 1 and gla_w_in.shape[0] == 1
    bp, tp, _ = x_prompt.shape
    bs, ts, _ = x_sample.shape
    past = cache_sb_k.shape[2]
    nk, nv = GLA_HEADS * GLA_DK, GLA_HEADS * GLA_DV
    main = 2 * nk + 2 * nv

    mod = _ada(jnp.concatenate([c_prompt, c_sample], axis=0), w_ada, b_ada)
    split = lambda m: (m[:, :d], m[:, d:2 * d], m[:, 2 * d:])
    mods_p = [split(mod[l, :bp]) for l in range(depth)]
    mods_s = [split(mod[l, bp:]) for l in range(depth)]

    gla_w = gla_w_in[0]
    gla_wa = jnp.pad(gla_w[:, main:], ((0, 0), (0, V7X_LANES - GLA_GATE_RANK))).astype(BF16)
    gla_wa2 = jnp.pad(gla_w_a2[0], ((0, V7X_LANES - GLA_GATE_RANK), (0, 0))).astype(BF16)
    weights = (norm_g, sb_w_in[0].astype(BF16), sb_w_out[0].astype(BF16), gla_w.astype(BF16),
               gla_wa, gla_wa2, gla_b_a[0], gla_norm_g[0], gla_w_out[0].astype(BF16), final_norm_g)

    y_p, k_p, v_p, s_p = _trunk(x_prompt, mods_p, weights, per_row=False,
                                tm_in=min(INPROJ_ROW_TILE, tp), tn_in=INPROJ_COL_TILE,
                                tm_out=min(OUTPROJ_ROW_TILE, tp), gla_tt=GLA_TOKEN_TILE)
    cache = (cache_sb_k[0].reshape(bs, past * SB_HEADS, SB_HEAD_DIM),
             cache_sb_v[0].reshape(bs, past * SB_HEADS, SB_HEAD_DIM))
    y_s, k_s, v_s, s_s = _trunk(x_sample, mods_s, weights, per_row=True,
                                tm_in=bs * ts, tn_in=d, tm_out=bs * ts, gla_tt=ts, cache=cache,
                                state=state_gla[0])

    heads = lambda a: a.reshape(1, a.shape[0], a.shape[1], SB_HEADS, SB_HEAD_DIM)
    return (y_p, y_s, heads(k_p), heads(v_p), heads(k_s), heads(v_s), s_p[None], s_s[None])
```

```python
import functools
import math

import jax
import jax.numpy as jnp
from jax import lax
from jax.experimental import pallas as pl
from jax.experimental.pallas import tpu as pltpu

F32 = jnp.float32
BF16 = jnp.bfloat16

SB_HEADS = 16
SB_HEAD_DIM = 128
GLA_HEADS = 4
GLA_DK = 256
GLA_DV = 512
GLA_GATE_RANK = 16
GLA_TAU = 16.0
GLA_CHUNK = 64
EPS = 1e-6

V7X_LANES = 128
V7X_VMEM_LIMIT_BYTES = 56 * 1024 * 1024
ADA_COL_TILE = 512
INPROJ_ROW_TILE = 1024
INPROJ_COL_TILE = 1024
OUTPROJ_ROW_TILE = 512
GLA_TOKEN_TILE = 1024
SB_BLOCK = 256
SB_QUERY_TILE = 2048
SB_UNDERFLOW_LOG2 = 160.0
SB_FINISHED_RUN = 1e30
NORM_ROWS = 16
NORM_UNROLL = 4


def _params(semantics):
    return pltpu.CompilerParams(dimension_semantics=semantics,
                                vmem_limit_bytes=V7X_VMEM_LIMIT_BYTES)


def _split_bf16(x):
    hi = x.astype(BF16)
    lo = (x - hi.astype(F32)).astype(BF16)
    return hi, lo


def _log_sigmoid(z):
    return jnp.minimum(z, 0.0) - jnp.log(1.0 + jnp.exp(-jnp.abs(z)))


def _softplus2(z):
    neg_abs = lax.bitcast_convert_type(lax.bitcast_convert_type(z, jnp.uint32) | jnp.uint32(1 << 31), F32)
    return jnp.maximum(z, 0.0) + jnp.log2(1.0 + jnp.exp2(neg_abs))


def _silu(z):
    return z / (1.0 + jnp.exp(-z))


def _ada_kernel(c_ref, w_ref, b_ref, o_ref):
    acc = jnp.dot(c_ref[...].astype(BF16), w_ref[0].astype(BF16), preferred_element_type=F32)
    o_ref[0] = acc + b_ref[0]


def _ada(c_all, w_ada, b_ada, tn=ADA_COL_TILE):
    depth, d, n = w_ada.shape
    rows = c_all.shape[0]
    return pl.pallas_call(
        _ada_kernel,
        grid=(depth, n // tn),
        in_specs=[pl.BlockSpec((rows, d), lambda l, j: (0, 0)),
                  pl.BlockSpec((1, d, tn), lambda l, j: (l, 0, j)),
                  pl.BlockSpec((1, 1, tn), lambda l, j: (l, 0, j))],
        out_specs=pl.BlockSpec((1, rows, tn), lambda l, j: (l, 0, j)),
        out_shape=jax.ShapeDtypeStruct((depth, rows, n), F32),
        compiler_params=_params(("arbitrary", "arbitrary")),
        name="ada",
    )(c_all, w_ada, b_ada.reshape(depth, 1, n))


def _inproj_kernel(*refs, n_f32, bf16_mults, tm, steps_per_seg, with_decay):
    x_ref, sc_ref, sh_ref, g_ref, w_ref = refs[:5]
    pos = 5
    if with_decay:
        wa_ref, wa2_ref, ba_ref = refs[pos:pos + 3]
        pos += 3
    of_refs = refs[pos:pos + n_f32]
    ob_ref = refs[pos + n_f32]
    pos += n_f32 + 1
    if with_decay:
        dec_ref = refs[pos]
        pos += 1
    h_ref = refs[pos]
    n = pl.program_id(2)
    seg = n // steps_per_seg

    @pl.when(n == 0)
    def _():
        def norm_rows(r, carry):
            rows = pl.ds(pl.multiple_of(r * NORM_ROWS, NORM_ROWS), NORM_ROWS)
            x = x_ref[0, rows, :]
            y = x * lax.rsqrt(jnp.mean(x * x, axis=-1, keepdims=True) + EPS) * g_ref[...]
            if sc_ref.shape[1] == 1:
                sc, sh = sc_ref[0], sh_ref[0]
            else:
                sc, sh = sc_ref[0, rows, :], sh_ref[0, rows, :]
            h_ref[rows, :] = (y * (1.0 + sc) + sh).astype(BF16)
            return carry
        steps = tm // NORM_ROWS
        lax.fori_loop(0, steps, norm_rows, 0, unroll=NORM_UNROLL if steps % NORM_UNROLL == 0 else 1)

    for idx, of_ref in enumerate(of_refs):
        @pl.when(seg == idx)
        def _(of_ref=of_ref):
            of_ref[0] = jnp.dot(h_ref[...], w_ref[...], preferred_element_type=F32)

    @pl.when(seg >= n_f32)
    def _():
        acc = jnp.dot(h_ref[...], w_ref[...], preferred_element_type=F32)
        if any(m is not None for m in bf16_mults):
            mult = jnp.float32(1.0)
            for idx, m in enumerate(bf16_mults):
                if m is not None:
                    mult = jnp.where(seg == n_f32 + idx, jnp.float32(m), mult)
            acc = acc * mult
        ob_ref[0, 0] = acc.astype(BF16)

    if with_decay:
        @pl.when(n == pl.num_programs(2) - 1)
        def _():
            a_lr = jnp.dot(h_ref[...], wa_ref[...], preferred_element_type=F32)
            pre = jnp.dot(a_lr.astype(BF16), wa2_ref[...], preferred_element_type=F32) + ba_ref[...]
            dec_ref[0] = _log_sigmoid(pre) * (1.0 / GLA_TAU)


def _inproj(x, scale, shift, g, w, seg_order, n_f32, bf16_mults, *, tm, tn, decay=None):
    b, t, d = x.shape
    n_seg = len(seg_order)
    n_bf16 = len(bf16_mults)
    sps = d // tn
    assert t % tm == 0 and w.shape[1] >= n_seg * d and n_seg == n_f32 + n_bf16 and d == sps * tn
    assert tm % NORM_ROWS == 0

    def w_block(j):
        seg = jnp.int32(seg_order[-1])
        for k, s in enumerate(seg_order[:-1]):
            seg = jnp.where(j // sps == k, s, seg)
        return seg * sps + j % sps

    mod_rows = scale.shape[1]
    mod_spec = (pl.BlockSpec((1, 1, d), lambda bi, i, j: (bi, 0, 0)) if mod_rows == 1
                else pl.BlockSpec((1, tm, d), lambda bi, i, j: (bi, i, 0)))
    in_specs = [pl.BlockSpec((1, tm, d), lambda bi, i, j: (bi, i, 0)), mod_spec, mod_spec,
                pl.BlockSpec((1, d), lambda bi, i, j: (0, 0)),
                pl.BlockSpec((d, tn), lambda bi, i, j: (0, w_block(j)))]
    args = [x, scale, shift, g.reshape(1, d), w]
    if decay is not None:
        wa, wa2, ba = decay
        in_specs += [pl.BlockSpec(wa.shape, lambda bi, i, j: (0, 0)),
                     pl.BlockSpec(wa2.shape, lambda bi, i, j: (0, 0)),
                     pl.BlockSpec((1, ba.shape[-1]), lambda bi, i, j: (0, 0))]
        args += [wa, wa2, ba.reshape(1, -1)]

    def f32_spec(k):
        return pl.BlockSpec((1, tm, tn), lambda bi, i, j: (bi, i, jnp.clip(j - k * sps, 0, sps - 1)))

    out_specs = [f32_spec(k) for k in range(n_f32)]
    out_specs.append(pl.BlockSpec(
        (1, 1, tm, tn),
        lambda bi, i, j: (jnp.maximum(j // sps - n_f32, 0), bi, i, jnp.where(j // sps >= n_f32, j % sps, 0))))
    out_shape = [jax.ShapeDtypeStruct((b, t, d), F32) for _ in range(n_f32)]
    out_shape.append(jax.ShapeDtypeStruct((n_bf16, b, t, d), BF16))
    if decay is not None:
        nk = decay[1].shape[1]
        out_specs.append(pl.BlockSpec((1, tm, nk), lambda bi, i, j: (bi, i, 0)))
        out_shape.append(jax.ShapeDtypeStruct((b, t, nk), F32))
    return pl.pallas_call(
        functools.partial(_inproj_kernel, n_f32=n_f32, bf16_mults=tuple(bf16_mults), tm=tm,
                          steps_per_seg=sps, with_decay=decay is not None),
        grid=(b, t // tm, n_seg * sps),
        in_specs=in_specs, out_specs=out_specs, out_shape=out_shape,
        scratch_shapes=[pltpu.VMEM((tm, d), BF16)],
        compiler_params=_params(("arbitrary", "arbitrary", "arbitrary")),
        name="inproj",
    )(*args)


def _sb_blocks(qs, k_chains, v_chains, w, runs, mask=None, valid_chains=None):
    half = V7X_LANES
    stacked = lambda hi, lo, cols: jnp.concatenate([hi[:, cols], lo[:, cols]], axis=1)
    z_chains = [[lax.dot_general(q, k_blk, (((1,), (1,)), ((), ())), preferred_element_type=F32)
                 for k_blk in chain] for q, chain in zip(qs, k_chains)]
    sum_chains = []
    for z_chain in z_chains:
        sums = []
        for pos, z in enumerate(z_chain):
            sp = _softplus2(z)
            if mask is not None and pos == 0:
                sp = jnp.where(mask, sp, 0.0)
            hi, lo = _split_bf16(sp)
            sums.append((jnp.dot(stacked(hi, lo, slice(half, None)), w, preferred_element_type=F32),
                         jnp.dot(stacked(hi, lo, slice(None, half)), w, preferred_element_type=F32)))
        sum_chains.append(sums)
    outs = []
    for g, (z_chain, sums, v_chain, run) in enumerate(zip(z_chains, sum_chains, v_chains, runs)):
        weights = []
        for pos, (z, (cs_r, cs_l)) in enumerate(zip(z_chain, sums)):
            if valid_chains is not None and valid_chains[g][pos] is not None:
                run = jnp.where(valid_chains[g][pos], run, SB_FINISHED_RUN)
            a_r = jnp.exp2(z[:, half:] - cs_r[:, :half] - run)
            run = run + cs_r[:, half:]
            a_l = jnp.exp2(z[:, :half] - cs_l[:, :half] - run)
            run = run + cs_l[:, half:]
            a = jnp.concatenate([a_l, a_r], axis=1)
            if mask is not None and pos == 0:
                a = jnp.where(mask, a, 0.0)
            weights.append(a.astype(BF16))
        a_all = weights[0] if len(weights) == 1 else jnp.concatenate(weights, axis=1)
        v_all = v_chain[0] if len(v_chain) == 1 else jnp.concatenate(v_chain, axis=0)
        outs.append((jnp.dot(a_all, v_all, preferred_element_type=F32), run))
    return outs


def _sb_consts():
    j = lax.broadcasted_iota(jnp.int32, (V7X_LANES, V7X_LANES), 0)
    s = lax.broadcasted_iota(jnp.int32, (V7X_LANES, V7X_LANES), 1)
    half = jnp.concatenate([(j >= s).astype(BF16), jnp.ones((V7X_LANES, V7X_LANES), BF16)], axis=1)
    return jnp.concatenate([half, half], axis=0)


def _causal_mask(rows):
    t_idx = lax.broadcasted_iota(jnp.int32, (rows, SB_BLOCK), 0)
    s_idx = lax.broadcasted_iota(jnp.int32, (rows, SB_BLOCK), 1)
    return s_idx < t_idx


def _sb_prompt_kernel(q_ref, k_ref, v_ref, zg_ref, w_ref, o_ref, acc_ref, run_ref, *, tq):
    i = pl.program_id(2)
    groups = tq // SB_BLOCK
    w = w_ref[...]

    def keys(first):
        rows = pl.ds(pl.multiple_of(first, SB_BLOCK), SB_BLOCK)
        return k_ref[0, rows, :].astype(BF16), v_ref[0, rows, :].astype(BF16)

    group_rows = [slice(g * SB_BLOCK, (g + 1) * SB_BLOCK) for g in range(groups)]
    queries = lambda: [q_ref[0, rows, :] for rows in group_rows]
    blks = [i * groups + g for g in range(groups)]
    kvs = [(keys(blk * SB_BLOCK), keys(jnp.maximum(blk - 1, 0) * SB_BLOCK)) for blk in blks]
    outs = _sb_blocks(queries(), [[kv[0][0], kv[1][0]] for kv in kvs], [[kv[0][1], kv[1][1]] for kv in kvs], w,
                      [jnp.zeros((SB_BLOCK, V7X_LANES), F32)] * groups, mask=_causal_mask(SB_BLOCK),
                      valid_chains=[[None, blk >= 1] for blk in blks])
    for rows, (pv, run) in zip(group_rows, outs):
        acc_ref[rows, :] = pv
        run_ref[rows, :] = run

    half = SB_BLOCK // 2
    upper_rows = [slice(g * SB_BLOCK, g * SB_BLOCK + half) for g in range(groups)]
    lower_rows = [slice(g * SB_BLOCK + half, (g + 1) * SB_BLOCK) for g in range(groups)]

    def pending(d, row_sets):
        low = jnp.float32(jnp.inf)
        for g, rows in enumerate(row_sets):
            low = jnp.minimum(low, jnp.where(i * groups + g - d >= 0, jnp.min(run_ref[rows, :]), jnp.inf))
        return low

    def more(carry):
        return jnp.minimum(carry[1], carry[2]) < SB_UNDERFLOW_LOG2

    def diagonal(carry):
        d = carry[0]
        blks = [i * groups + g - d for g in range(groups)]

        def sweep(row_sets):
            kvs = [keys(jnp.maximum(blk, 0) * SB_BLOCK) for blk in blks]
            outs = _sb_blocks([q_ref[0, rows, :] for rows in row_sets], [[kv[0]] for kv in kvs],
                              [[kv[1]] for kv in kvs], w, [run_ref[rows, :] for rows in row_sets],
                              valid_chains=[[blk >= 0] for blk in blks])
            for rows, (pv, run) in zip(row_sets, outs):
                acc_ref[rows, :] += pv
                run_ref[rows, :] = run

        lax.cond(carry[2] < SB_UNDERFLOW_LOG2, lambda: sweep(group_rows), lambda: sweep(upper_rows))
        return d + 1, pending(d + 1, upper_rows), pending(d + 1, lower_rows)

    lax.while_loop(more, diagonal, (jnp.int32(2), pending(2, upper_rows), pending(2, lower_rows)))
    o_ref[0] = (acc_ref[...] * _silu(zg_ref[0].astype(F32))).astype(o_ref.dtype)


def _sb_prompt(qz, k, v, tq):
    _, b, t, _ = qz.shape
    assert t % tq == 0 and tq % SB_BLOCK == 0
    tile = pl.BlockSpec((1, tq, SB_HEAD_DIM), lambda bi, h, i: (bi, i, h))
    stacked = lambda which: pl.BlockSpec((None, 1, tq, SB_HEAD_DIM), lambda bi, h, i: (which, bi, i, h))
    whole = pl.BlockSpec((1, t, SB_HEAD_DIM), lambda bi, h, i: (bi, 0, h))
    return pl.pallas_call(
        functools.partial(_sb_prompt_kernel, tq=tq),
        grid=(b, SB_HEADS, t // tq),
        in_specs=[stacked(0), whole, whole, stacked(1),
                  pl.BlockSpec((SB_BLOCK, SB_BLOCK), lambda bi, h, i: (0, 0))],
        out_specs=tile,
        out_shape=jax.ShapeDtypeStruct(qz.shape[1:], BF16),
        scratch_shapes=[pltpu.VMEM((tq, SB_HEAD_DIM), F32), pltpu.VMEM((tq, V7X_LANES), F32)],
        compiler_params=_params(("arbitrary", "arbitrary", "arbitrary")),
        name="sb_prompt",
    )(qz, k, v, qz, _sb_consts())


def _sb_sample_kernel(q_ref, zg_ref, kn_ref, vn_ref, kc_hbm, vc_hbm, w_ref, o_ref,
                      kbuf, vbuf, sem, acc_ref, run_ref, *, n_blocks):
    bi = pl.program_id(0)
    t_new = q_ref.shape[1]
    block_rows = SB_BLOCK * SB_HEADS
    w = w_ref[...]

    def fetch(j, slot):
        src = pl.ds(j * block_rows, block_rows)
        return (pltpu.make_async_copy(kc_hbm.at[bi, src, :], kbuf.at[slot], sem.at[0, slot]),
                pltpu.make_async_copy(vc_hbm.at[bi, src, :], vbuf.at[slot], sem.at[1, slot]))

    for copy in fetch(n_blocks - 1, 0):
        copy.start()

    def head(h):
        return slice(h * t_new, (h + 1) * t_new), slice(h * SB_HEAD_DIM, (h + 1) * SB_HEAD_DIM)

    mask = _causal_mask(t_new)
    padding = jnp.zeros((SB_BLOCK - t_new, SB_HEAD_DIM), BF16)
    heads = [head(h) for h in range(SB_HEADS)]
    padded = lambda ref, cols: jnp.concatenate([ref[0, :, cols].astype(BF16), padding], axis=0)
    outs = _sb_blocks([q_ref[0, :, cols] for _, cols in heads], [[padded(kn_ref, cols)] for _, cols in heads],
                      [[padded(vn_ref, cols)] for _, cols in heads], w,
                      [jnp.zeros((t_new, V7X_LANES), F32)] * SB_HEADS, mask=mask)
    for (rows, _), (pv, run) in zip(heads, outs):
        acc_ref[rows, :] = pv
        run_ref[rows, :] = run

    def more(carry):
        return (carry[0] >= 0) & (carry[1] < SB_UNDERFLOW_LOG2)

    def past_block(carry):
        j = carry[0]
        slot = (n_blocks - 1 - j) % 2
        for copy in fetch(j, slot):
            copy.wait()

        @pl.when(j > 0)
        def _():
            for copy in fetch(j - 1, 1 - slot):
                copy.start()

        keys = [pl.ds(h, SB_BLOCK, stride=SB_HEADS) for h in range(SB_HEADS)]
        outs = _sb_blocks([q_ref[0, :, cols] for _, cols in heads],
                          [[kbuf[slot, rows, :].astype(BF16)] for rows in keys],
                          [[vbuf[slot, rows, :].astype(BF16)] for rows in keys], w,
                          [run_ref[rows, :] for rows, _ in heads])
        for (rows, _), (pv, run) in zip(heads, outs):
            acc_ref[rows, :] += pv
            run_ref[rows, :] = run
        return j - 1, jnp.min(run_ref[...])

    j_end, _ = lax.while_loop(more, past_block, (jnp.int32(n_blocks - 1), jnp.min(run_ref[...])))

    @pl.when(j_end >= 0)
    def _():
        for copy in fetch(j_end, (n_blocks - 1 - j_end) % 2):
            copy.wait()

    for h in range(SB_HEADS):
        rows, cols = head(h)
        o_ref[0, :, cols] = (acc_ref[rows, :] * _silu(zg_ref[0, :, cols].astype(F32))).astype(o_ref.dtype)


def _sb_sample(qz, k_new, v_new, k_past, v_past):
    _, b, t, d = qz.shape
    n_blocks = k_past.shape[1] // (SB_BLOCK * SB_HEADS)
    assert k_past.shape[1] == n_blocks * SB_BLOCK * SB_HEADS and n_blocks >= 1 and t <= V7X_LANES
    new = pl.BlockSpec((1, t, d), lambda bi: (bi, 0, 0))
    stacked = lambda which: pl.BlockSpec((None, 1, t, d), lambda bi: (which, bi, 0, 0))
    return pl.pallas_call(
        functools.partial(_sb_sample_kernel, n_blocks=n_blocks),
        grid=(b,),
        in_specs=[stacked(0), stacked(1), new, new,
                  pl.BlockSpec(memory_space=pl.ANY), pl.BlockSpec(memory_space=pl.ANY),
                  pl.BlockSpec((SB_BLOCK, SB_BLOCK), lambda bi: (0, 0))],
        out_specs=new,
        out_shape=jax.ShapeDtypeStruct(qz.shape[1:], BF16),
        scratch_shapes=[pltpu.VMEM((2, SB_BLOCK * SB_HEADS, SB_HEAD_DIM), F32),
                        pltpu.VMEM((2, SB_BLOCK * SB_HEADS, SB_HEAD_DIM), F32),
                        pltpu.SemaphoreType.DMA((2, 2)),
                        pltpu.VMEM((SB_HEADS * t, SB_HEAD_DIM), F32),
                        pltpu.VMEM((SB_HEADS * t, V7X_LANES), F32)],
        compiler_params=_params(("arbitrary",)),
        name="sb_sample",
    )(qz, qz, k_new, v_new, k_past, v_past, _sb_consts())


def _gla_kernel(*refs, chunk, n_chunks, has_state):
    if has_state:
        q_ref, k_ref, v_ref, dec_ref, r_ref, gn_ref, tril_ref, s0_ref, o_ref, sf_ref, st_ref, upd_ref = refs
    else:
        q_ref, k_ref, v_ref, dec_ref, r_ref, gn_ref, tril_ref, o_ref, sf_ref, st_ref, upd_ref = refs
    step = pl.program_id(2)

    @pl.when(step == 0)
    def _():
        st_ref[...] = s0_ref[0, 0].T if has_state else jnp.zeros_like(st_ref)

    tril2 = tril_ref[...]
    causal = (lax.broadcasted_iota(jnp.int32, (chunk, chunk), 1)
              <= lax.broadcasted_iota(jnp.int32, (chunk, chunk), 0))
    q_scale = GLA_DK ** -0.5
    nt_dot = lambda a, b: lax.dot_general(a, b, (((1,), (1,)), ((), ())), preferred_element_type=F32)
    chunks = [pl.ds(c * chunk, chunk) for c in range(n_chunks)]

    q_in, k_out, k_end, decay = [], [], [], []
    for rows in chunks:
        g_hi, g_lo = _split_bf16(dec_ref[0, rows, :])
        b = jnp.dot(tril2, jnp.concatenate([g_hi, g_lo], axis=0), preferred_element_type=F32)
        b_last = b[chunk - 1:chunk, :]
        k = k_ref[0, rows, :]
        q_in.append((q_ref[0, rows, :] * q_scale * jnp.exp(b)).astype(BF16))
        k_out.append((k * jnp.exp(-b)).astype(BF16))
        k_end.append((k * jnp.exp(b_last - b)).astype(BF16))
        decay.append(jnp.exp(b_last))
    intra = []
    for c, rows in enumerate(chunks):
        v = v_ref[0, rows, :]
        a = jnp.where(causal, nt_dot(q_in[c], k_out[c]), 0.0).astype(BF16)
        intra.append(jnp.dot(a, v, preferred_element_type=F32))
        upd_ref[c] = lax.dot_general(v, k_end[c], (((0,), (0,)), ((), ())), preferred_element_type=F32)
    for c, rows in enumerate(chunks):
        state_t = st_ref[...]
        o = intra[c] + nt_dot(q_in[c], state_t.astype(BF16))
        st_ref[...] = state_t * decay[c] + upd_ref[c]
        o = o * lax.rsqrt(jnp.mean(o * o, axis=-1, keepdims=True) + EPS) * gn_ref[...]
        o_ref[0, rows, :] = (o * _silu(r_ref[0, rows, :].astype(F32))).astype(o_ref.dtype)

    @pl.when(step == pl.num_programs(2) - 1)
    def _():
        sf_ref[0, 0] = st_ref[...].T


def _gla(qk, vr, dec, gn, s0, *, chunk, tt):
    _, b, t, _ = vr.shape
    assert t % tt == 0 and tt % chunk == 0
    tril = (lax.broadcasted_iota(jnp.int32, (chunk, chunk), 1)
            <= lax.broadcasted_iota(jnp.int32, (chunk, chunk), 0)).astype(BF16)
    tril2 = jnp.concatenate([tril, tril], axis=1)
    n_chunks = tt // chunk
    kspec = lambda off: pl.BlockSpec((1, tt, GLA_DK), lambda bi, h, s: (bi, s, h + off))
    vspec = pl.BlockSpec((1, tt, GLA_DV), lambda bi, h, s: (bi, s, h))
    sspec = pl.BlockSpec((1, 1, GLA_DK, GLA_DV), lambda bi, h, s: (bi, h, 0, 0))
    stacked = lambda which: pl.BlockSpec((None, 1, tt, GLA_DV), lambda bi, h, s: (which, bi, s, h))
    in_specs = [kspec(0), kspec(GLA_HEADS), stacked(0), kspec(0), stacked(1),
                pl.BlockSpec((1, GLA_DV), lambda bi, h, s: (0, h)),
                pl.BlockSpec((chunk, 2 * chunk), lambda bi, h, s: (0, 0))]
    args = [qk, qk, vr, dec, vr, gn.reshape(1, -1), tril2]
    if s0 is not None:
        in_specs.append(sspec)
        args.append(s0)
    return pl.pallas_call(
        functools.partial(_gla_kernel, chunk=chunk, n_chunks=n_chunks, has_state=s0 is not None),
        grid=(b, GLA_HEADS, t // tt),
        in_specs=in_specs,
        out_specs=[vspec, sspec],
        out_shape=[jax.ShapeDtypeStruct(vr.shape[1:], BF16),
                   jax.ShapeDtypeStruct((b, GLA_HEADS, GLA_DK, GLA_DV), F32)],
        scratch_shapes=[pltpu.VMEM((GLA_DV, GLA_DK), F32), pltpu.VMEM((n_chunks, GLA_DV, GLA_DK), F32)],
        compiler_params=_params(("arbitrary", "arbitrary", "arbitrary")),
        name="gla",
    )(*args)


def _outproj_kernel(y_ref, w_ref, x_ref, gate_ref, *rest, final_norm):
    if final_norm:
        gf_ref, o_ref = rest
    else:
        (o_ref,) = rest
    y = jnp.dot(y_ref[0], w_ref[...], preferred_element_type=F32)
    x = x_ref[0] + gate_ref[0] * y
    if final_norm:
        x = x * lax.rsqrt(jnp.mean(x * x, axis=-1, keepdims=True) + EPS) * gf_ref[...]
    o_ref[0] = x


def _outproj(y, w, x, gate, gf, *, tm):
    b, t, d = x.shape
    kdim = y.shape[-1]
    assert t % tm == 0
    gate_spec = (pl.BlockSpec((1, 1, d), lambda bi, i: (bi, 0, 0)) if gate.shape[1] == 1
                 else pl.BlockSpec((1, tm, d), lambda bi, i: (bi, i, 0)))
    in_specs = [pl.BlockSpec((1, tm, kdim), lambda bi, i: (bi, i, 0)),
                pl.BlockSpec((kdim, d), lambda bi, i: (0, 0)),
                pl.BlockSpec((1, tm, d), lambda bi, i: (bi, i, 0)),
                gate_spec]
    args = [y, w, x, gate]
    if gf is not None:
        in_specs.append(pl.BlockSpec((1, d), lambda bi, i: (0, 0)))
        args.append(gf.reshape(1, d))
    return pl.pallas_call(
        functools.partial(_outproj_kernel, final_norm=gf is not None),
        grid=(b, t // tm),
        in_specs=in_specs,
        out_specs=pl.BlockSpec((1, tm, d), lambda bi, i: (bi, i, 0)),
        out_shape=jax.ShapeDtypeStruct(x.shape, F32),
        compiler_params=_params(("arbitrary", "arbitrary")),
        name="outproj",
    )(*args)


def _trunk(x, mods, weights, *, per_row, tm_in, tn_in, tm_out, gla_tt, cache=None, state=None):
    (norm_g, sb_w_in, sb_w_out, gla_w_in, gla_wa, gla_wa2, gla_b_a, gla_norm_g, gla_w_out,
     final_norm_g) = weights
    b, t, d = x.shape
    if per_row:
        fold = lambda a: a.reshape(1, b * t, a.shape[-1])
        rows = lambda m: jnp.broadcast_to(m[:, None, :], (b, t, d)).reshape(1, b * t, d)
    else:
        fold = lambda a: a
        rows = lambda m: m[:, None, :]
    unfold = lambda a: a.reshape(b, t, a.shape[-1])
    width = SB_HEADS * SB_HEAD_DIM

    shift, scale, gate = mods[0]
    assert width == d
    k, v, qz = _inproj(fold(x), rows(scale), rows(shift), norm_g[0], sb_w_in, (1, 2, 0, 3), 2,
                       (SB_HEAD_DIM ** -0.5 * math.log2(math.e), None), tm=tm_in, tn=tn_in)
    k, v, qz = unfold(k), unfold(v), qz.reshape(2, b, t, d)
    if cache is None:
        branch = _sb_prompt(qz, k, v, tq=min(SB_QUERY_TILE, t))
    else:
        branch = _sb_sample(qz, k, v, cache[0], cache[1])
    x1 = _outproj(fold(branch), sb_w_out, fold(x), rows(gate), None, tm=tm_out)

    shift, scale, gate = mods[1]
    nk, nv = GLA_HEADS * GLA_DK, GLA_HEADS * GLA_DV
    assert 2 * nk == d and nv == d
    qk, vr, dec = _inproj(x1, rows(scale), rows(shift), norm_g[1], gla_w_in, (0, 1, 2), 1, (None, None),
                          tm=tm_in, tn=tn_in, decay=(gla_wa, gla_wa2, gla_b_a))
    chunk = min(GLA_CHUNK, t)
    branch, s_new = _gla(unfold(qk), vr.reshape(2, b, t, d), unfold(dec), gla_norm_g, state,
                         chunk=chunk, tt=min(gla_tt, t))
    y = _outproj(fold(branch), gla_w_out, x1, rows(gate), final_norm_g, tm=tm_out)
    return unfold(y), k, v, s_new


def kernel(x_prompt, x_sample, cache_sb_k, cache_sb_v, state_gla, c_prompt, c_sample, w_ada, b_ada, norm_g,
           sb_w_in, sb_w_out, gla_w_in, gla_w_a2, gla_b_a, gla_norm_g, gla_w_out, final_norm_g):
    depth, d, _ = w_ada.shape
    assert depth == 2 and sb_w_in.shape[0] == 1 and gla_w_in.shape[0] == 1
    bp, tp, _ = x_prompt.shape
    bs, ts, _ = x_sample.shape
    past = cache_sb_k.shape[2]
    nk, nv = GLA_HEADS * GLA_DK, GLA_HEADS * GLA_DV
    main = 2 * nk + 2 * nv

    mod = _ada(jnp.concatenate([c_prompt, c_sample], axis=0), w_ada, b_ada)
    split = lambda m: (m[:, :d], m[:, d:2 * d], m[:, 2 * d:])
    mods_p = [split(mod[l, :bp]) for l in range(depth)]
    mods_s = [split(mod[l, bp:]) for l in range(depth)]

    gla_w = gla_w_in[0]
    gla_wa = jnp.pad(gla_w[:, main:], ((0, 0), (0, V7X_LANES - GLA_GATE_RANK))).astype(BF16)
    gla_wa2 = jnp.pad(gla_w_a2[0], ((0, V7X_LANES - GLA_GATE_RANK), (0, 0))).astype(BF16)
    weights = (norm_g, sb_w_in[0].astype(BF16), sb_w_out[0].astype(BF16), gla_w.astype(BF16),
               gla_wa, gla_wa2, gla_b_a[0], gla_norm_g[0], gla_w_out[0].astype(BF16), final_norm_g)

    y_p, k_p, v_p, s_p = _trunk(x_prompt, mods_p, weights, per_row=False,
                                tm_in=min(INPROJ_ROW_TILE, tp), tn_in=INPROJ_COL_TILE,
                                tm_out=min(OUTPROJ_ROW_TILE, tp), gla_tt=GLA_TOKEN_TILE)
    cache = (cache_sb_k[0].reshape(bs, past * SB_HEADS, SB_HEAD_DIM),
             cache_sb_v[0].reshape(bs, past * SB_HEADS, SB_HEAD_DIM))
    y_s, k_s, v_s, s_s = _trunk(x_sample, mods_s, weights, per_row=True,
                                tm_in=bs * ts, tn_in=d, tm_out=bs * ts, gla_tt=ts, cache=cache,
                                state=state_gla[0])

    heads = lambda a: a.reshape(1, a.shape[0], a.shape[1], SB_HEADS, SB_HEAD_DIM)
    return (y_p, y_s, heads(k_p), heads(v_p), heads(k_s), heads(v_s), s_p[None], s_s[None])
```

```python
import functools
import math

import jax
import jax.numpy as jnp
from jax import lax
from jax.experimental import pallas as pl
from jax.experimental.pallas import tpu as pltpu

F32 = jnp.float32
BF16 = jnp.bfloat16

SB_HEADS = 16
SB_HEAD_DIM = 128
GLA_HEADS = 4
GLA_DK = 256
GLA_DV = 512
GLA_GATE_RANK = 16
GLA_TAU = 16.0
GLA_CHUNK = 64
EPS = 1e-6

V7X_LANES = 128
V7X_VMEM_LIMIT_BYTES = 56 * 1024 * 1024
ADA_COL_TILE = 512
INPROJ_ROW_TILE = 1024
INPROJ_COL_TILE = 1024
OUTPROJ_ROW_TILE = 512
GLA_TOKEN_TILE = 1024
SB_BLOCK = 256
SB_QUERY_TILE = 4096
SB_UNDERFLOW_LOG2 = 160.0
SB_FINISHED_RUN = 1e30
NORM_ROWS = 16
NORM_UNROLL = 4


def _params(semantics):
    return pltpu.CompilerParams(dimension_semantics=semantics,
                                vmem_limit_bytes=V7X_VMEM_LIMIT_BYTES)


def _split_bf16(x):
    hi = x.astype(BF16)
    lo = (x - hi.astype(F32)).astype(BF16)
    return hi, lo


def _log_sigmoid(z):
    return jnp.minimum(z, 0.0) - jnp.log(1.0 + jnp.exp(-jnp.abs(z)))


def _softplus2(z):
    neg_abs = lax.bitcast_convert_type(lax.bitcast_convert_type(z, jnp.uint32) | jnp.uint32(1 << 31), F32)
    return jnp.maximum(z, 0.0) + jnp.log2(1.0 + jnp.exp2(neg_abs))


def _silu(z):
    return z / (1.0 + jnp.exp(-z))


def _ada_kernel(c_ref, w_ref, b_ref, o_ref):
    acc = jnp.dot(c_ref[...].astype(BF16), w_ref[0].astype(BF16), preferred_element_type=F32)
    o_ref[0] = acc + b_ref[0]


def _ada(c_all, w_ada, b_ada, tn=ADA_COL_TILE):
    depth, d, n = w_ada.shape
    rows = c_all.shape[0]
    return pl.pallas_call(
        _ada_kernel,
        grid=(depth, n // tn),
        in_specs=[pl.BlockSpec((rows, d), lambda l, j: (0, 0)),
                  pl.BlockSpec((1, d, tn), lambda l, j: (l, 0, j)),
                  pl.BlockSpec((1, 1, tn), lambda l, j: (l, 0, j))],
        out_specs=pl.BlockSpec((1, rows, tn), lambda l, j: (l, 0, j)),
        out_shape=jax.ShapeDtypeStruct((depth, rows, n), F32),
        compiler_params=_params(("arbitrary", "arbitrary")),
        name="ada",
    )(c_all, w_ada, b_ada.reshape(depth, 1, n))


def _inproj_kernel(*refs, n_f32, bf16_mults, tm, steps_per_seg, with_decay):
    x_ref, sc_ref, sh_ref, g_ref, w_ref = refs[:5]
    pos = 5
    if with_decay:
        wa_ref, wa2_ref, ba_ref = refs[pos:pos + 3]
        pos += 3
    of_refs = refs[pos:pos + n_f32]
    ob_ref = refs[pos + n_f32]
    pos += n_f32 + 1
    if with_decay:
        dec_ref = refs[pos]
        pos += 1
    h_ref = refs[pos]
    n = pl.program_id(2)
    seg = n // steps_per_seg

    @pl.when(n == 0)
    def _():
        def norm_rows(r, carry):
            rows = pl.ds(pl.multiple_of(r * NORM_ROWS, NORM_ROWS), NORM_ROWS)
            x = x_ref[0, rows, :]
            y = x * lax.rsqrt(jnp.mean(x * x, axis=-1, keepdims=True) + EPS) * g_ref[...]
            if sc_ref.shape[1] == 1:
                sc, sh = sc_ref[0], sh_ref[0]
            else:
                sc, sh = sc_ref[0, rows, :], sh_ref[0, rows, :]
            h_ref[rows, :] = (y * (1.0 + sc) + sh).astype(BF16)
            return carry
        steps = tm // NORM_ROWS
        lax.fori_loop(0, steps, norm_rows, 0, unroll=NORM_UNROLL if steps % NORM_UNROLL == 0 else 1)

    for idx, of_ref in enumerate(of_refs):
        @pl.when(seg == idx)
        def _(of_ref=of_ref):
            of_ref[0] = jnp.dot(h_ref[...], w_ref[...], preferred_element_type=F32)

    @pl.when(seg >= n_f32)
    def _():
        acc = jnp.dot(h_ref[...], w_ref[...], preferred_element_type=F32)
        if any(m is not None for m in bf16_mults):
            mult = jnp.float32(1.0)
            for idx, m in enumerate(bf16_mults):
                if m is not None:
                    mult = jnp.where(seg == n_f32 + idx, jnp.float32(m), mult)
            acc = acc * mult
        ob_ref[0, 0] = acc.astype(BF16)

    if with_decay:
        @pl.when(n == pl.num_programs(2) - 1)
        def _():
            a_lr = jnp.dot(h_ref[...], wa_ref[...], preferred_element_type=F32)
            pre = jnp.dot(a_lr.astype(BF16), wa2_ref[...], preferred_element_type=F32) + ba_ref[...]
            dec_ref[0] = _log_sigmoid(pre) * (1.0 / GLA_TAU)


def _inproj(x, scale, shift, g, w, seg_order, n_f32, bf16_mults, *, tm, tn, decay=None):
    b, t, d = x.shape
    n_seg = len(seg_order)
    n_bf16 = len(bf16_mults)
    sps = d // tn
    assert t % tm == 0 and w.shape[1] >= n_seg * d and n_seg == n_f32 + n_bf16 and d == sps * tn
    assert tm % NORM_ROWS == 0

    def w_block(j):
        seg = jnp.int32(seg_order[-1])
        for k, s in enumerate(seg_order[:-1]):
            seg = jnp.where(j // sps == k, s, seg)
        return seg * sps + j % sps

    mod_rows = scale.shape[1]
    mod_spec = (pl.BlockSpec((1, 1, d), lambda bi, i, j: (bi, 0, 0)) if mod_rows == 1
                else pl.BlockSpec((1, tm, d), lambda bi, i, j: (bi, i, 0)))
    in_specs = [pl.BlockSpec((1, tm, d), lambda bi, i, j: (bi, i, 0)), mod_spec, mod_spec,
                pl.BlockSpec((1, d), lambda bi, i, j: (0, 0)),
                pl.BlockSpec((d, tn), lambda bi, i, j: (0, w_block(j)))]
    args = [x, scale, shift, g.reshape(1, d), w]
    if decay is not None:
        wa, wa2, ba = decay
        in_specs += [pl.BlockSpec(wa.shape, lambda bi, i, j: (0, 0)),
                     pl.BlockSpec(wa2.shape, lambda bi, i, j: (0, 0)),
                     pl.BlockSpec((1, ba.shape[-1]), lambda bi, i, j: (0, 0))]
        args += [wa, wa2, ba.reshape(1, -1)]

    def f32_spec(k):
        return pl.BlockSpec((1, tm, tn), lambda bi, i, j: (bi, i, jnp.clip(j - k * sps, 0, sps - 1)))

    out_specs = [f32_spec(k) for k in range(n_f32)]
    out_specs.append(pl.BlockSpec(
        (1, 1, tm, tn),
        lambda bi, i, j: (jnp.maximum(j // sps - n_f32, 0), bi, i, jnp.where(j // sps >= n_f32, j % sps, 0))))
    out_shape = [jax.ShapeDtypeStruct((b, t, d), F32) for _ in range(n_f32)]
    out_shape.append(jax.ShapeDtypeStruct((n_bf16, b, t, d), BF16))
    if decay is not None:
        nk = decay[1].shape[1]
        out_specs.append(pl.BlockSpec((1, tm, nk), lambda bi, i, j: (bi, i, 0)))
        out_shape.append(jax.ShapeDtypeStruct((b, t, nk), F32))
    return pl.pallas_call(
        functools.partial(_inproj_kernel, n_f32=n_f32, bf16_mults=tuple(bf16_mults), tm=tm,
                          steps_per_seg=sps, with_decay=decay is not None),
        grid=(b, t // tm, n_seg * sps),
        in_specs=in_specs, out_specs=out_specs, out_shape=out_shape,
        scratch_shapes=[pltpu.VMEM((tm, d), BF16)],
        compiler_params=_params(("arbitrary", "arbitrary", "arbitrary")),
        name="inproj",
    )(*args)


def _sb_blocks(qs, k_chains, v_chains, w, runs, mask=None, valid_chains=None):
    half = V7X_LANES
    stacked = lambda hi, lo, cols: jnp.concatenate([hi[:, cols], lo[:, cols]], axis=1)
    z_chains = [[lax.dot_general(q, k_blk, (((1,), (1,)), ((), ())), preferred_element_type=F32)
                 for k_blk in chain] for q, chain in zip(qs, k_chains)]
    sum_chains = []
    for z_chain in z_chains:
        sums = []
        for pos, z in enumerate(z_chain):
            sp = _softplus2(z)
            if mask is not None and pos == 0:
                sp = jnp.where(mask, sp, 0.0)
            hi, lo = _split_bf16(sp)
            sums.append((jnp.dot(stacked(hi, lo, slice(half, None)), w, preferred_element_type=F32),
                         jnp.dot(stacked(hi, lo, slice(None, half)), w, preferred_element_type=F32)))
        sum_chains.append(sums)
    outs = []
    for g, (z_chain, sums, v_chain, run) in enumerate(zip(z_chains, sum_chains, v_chains, runs)):
        weights = []
        for pos, (z, (cs_r, cs_l)) in enumerate(zip(z_chain, sums)):
            if valid_chains is not None and valid_chains[g][pos] is not None:
                run = jnp.where(valid_chains[g][pos], run, SB_FINISHED_RUN)
            a_r = jnp.exp2(z[:, half:] - cs_r[:, :half] - run)
            run = run + cs_r[:, half:]
            a_l = jnp.exp2(z[:, :half] - cs_l[:, :half] - run)
            run = run + cs_l[:, half:]
            a = jnp.concatenate([a_l, a_r], axis=1)
            if mask is not None and pos == 0:
                a = jnp.where(mask, a, 0.0)
            weights.append(a.astype(BF16))
        a_all = weights[0] if len(weights) == 1 else jnp.concatenate(weights, axis=1)
        v_all = v_chain[0] if len(v_chain) == 1 else jnp.concatenate(v_chain, axis=0)
        outs.append((jnp.dot(a_all, v_all, preferred_element_type=F32), run))
    return outs


def _sb_consts():
    j = lax.broadcasted_iota(jnp.int32, (V7X_LANES, V7X_LANES), 0)
    s = lax.broadcasted_iota(jnp.int32, (V7X_LANES, V7X_LANES), 1)
    half = jnp.concatenate([(j >= s).astype(BF16), jnp.ones((V7X_LANES, V7X_LANES), BF16)], axis=1)
    return jnp.concatenate([half, half], axis=0)


def _causal_mask(rows):
    t_idx = lax.broadcasted_iota(jnp.int32, (rows, SB_BLOCK), 0)
    s_idx = lax.broadcasted_iota(jnp.int32, (rows, SB_BLOCK), 1)
    return s_idx < t_idx


def _sb_prompt_kernel(q_ref, k_ref, v_ref, zg_ref, w_ref, o_ref, acc_ref, run_ref, *, tq):
    i = pl.program_id(2)
    groups = tq // SB_BLOCK
    w = w_ref[...]

    def keys(first):
        rows = pl.ds(pl.multiple_of(first, SB_BLOCK), SB_BLOCK)
        return k_ref[0, rows, :].astype(BF16), v_ref[0, rows, :].astype(BF16)

    group_rows = [slice(g * SB_BLOCK, (g + 1) * SB_BLOCK) for g in range(groups)]
    queries = lambda: [q_ref[0, rows, :] for rows in group_rows]
    blks = [i * groups + g for g in range(groups)]
    kvs = [(keys(blk * SB_BLOCK), keys(jnp.maximum(blk - 1, 0) * SB_BLOCK)) for blk in blks]
    outs = _sb_blocks(queries(), [[kv[0][0], kv[1][0]] for kv in kvs], [[kv[0][1], kv[1][1]] for kv in kvs], w,
                      [jnp.zeros((SB_BLOCK, V7X_LANES), F32)] * groups, mask=_causal_mask(SB_BLOCK),
                      valid_chains=[[None, blk >= 1] for blk in blks])
    for rows, (pv, run) in zip(group_rows, outs):
        acc_ref[rows, :] = pv
        run_ref[rows, :] = run

    half = SB_BLOCK // 2
    upper_rows = [slice(g * SB_BLOCK, g * SB_BLOCK + half) for g in range(groups)]
    lower_rows = [slice(g * SB_BLOCK + half, (g + 1) * SB_BLOCK) for g in range(groups)]
    top_rows = [slice(g * SB_BLOCK, g * SB_BLOCK + half // 2) for g in range(groups)]
    mid_rows = [slice(g * SB_BLOCK + half // 2, g * SB_BLOCK + half) for g in range(groups)]

    def pending(d, row_sets):
        low = jnp.float32(jnp.inf)
        for g, rows in enumerate(row_sets):
            low = jnp.minimum(low, jnp.where(i * groups + g - d >= 0, jnp.min(run_ref[rows, :]), jnp.inf))
        return low

    def more(carry):
        return jnp.minimum(jnp.minimum(carry[1], carry[2]), carry[3]) < SB_UNDERFLOW_LOG2

    def diagonal(carry):
        d = carry[0]
        blks = [i * groups + g - d for g in range(groups)]

        def sweep(row_sets):
            kvs = [keys(jnp.maximum(blk, 0) * SB_BLOCK) for blk in blks]
            outs = _sb_blocks([q_ref[0, rows, :] for rows in row_sets], [[kv[0]] for kv in kvs],
                              [[kv[1]] for kv in kvs], w, [run_ref[rows, :] for rows in row_sets],
                              valid_chains=[[blk >= 0] for blk in blks])
            for rows, (pv, run) in zip(row_sets, outs):
                acc_ref[rows, :] += pv
                run_ref[rows, :] = run

        lax.cond(carry[3] < SB_UNDERFLOW_LOG2, lambda: sweep(group_rows),
                 lambda: lax.cond(carry[2] < SB_UNDERFLOW_LOG2, lambda: sweep(upper_rows),
                                  lambda: sweep(top_rows)))
        return d + 1, pending(d + 1, top_rows), pending(d + 1, mid_rows), pending(d + 1, lower_rows)

    lax.while_loop(more, diagonal,
                   (jnp.int32(2), pending(2, top_rows), pending(2, mid_rows), pending(2, lower_rows)))
    o_ref[0] = (acc_ref[...] * _silu(zg_ref[0].astype(F32))).astype(o_ref.dtype)


def _sb_prompt(qz, k, v, tq):
    _, b, t, _ = qz.shape
    assert t % tq == 0 and tq % SB_BLOCK == 0
    tile = pl.BlockSpec((1, tq, SB_HEAD_DIM), lambda bi, h, i: (bi, i, h))
    stacked = lambda which: pl.BlockSpec((None, 1, tq, SB_HEAD_DIM), lambda bi, h, i: (which, bi, i, h))
    whole = pl.BlockSpec((1, t, SB_HEAD_DIM), lambda bi, h, i: (bi, 0, h))
    return pl.pallas_call(
        functools.partial(_sb_prompt_kernel, tq=tq),
        grid=(b, SB_HEADS, t // tq),
        in_specs=[stacked(0), whole, whole, stacked(1),
                  pl.BlockSpec((SB_BLOCK, SB_BLOCK), lambda bi, h, i: (0, 0))],
        out_specs=tile,
        out_shape=jax.ShapeDtypeStruct(qz.shape[1:], BF16),
        scratch_shapes=[pltpu.VMEM((tq, SB_HEAD_DIM), F32), pltpu.VMEM((tq, V7X_LANES), F32)],
        compiler_params=_params(("arbitrary", "arbitrary", "arbitrary")),
        name="sb_prompt",
    )(qz, k, v, qz, _sb_consts())


def _sb_sample_kernel(q_ref, zg_ref, kn_ref, vn_ref, kc_hbm, vc_hbm, w_ref, o_ref,
                      kbuf, vbuf, sem, acc_ref, run_ref, *, n_blocks):
    bi = pl.program_id(0)
    t_new = q_ref.shape[1]
    block_rows = SB_BLOCK * SB_HEADS
    w = w_ref[...]

    def fetch(j, slot):
        src = pl.ds(j * block_rows, block_rows)
        return (pltpu.make_async_copy(kc_hbm.at[bi, src, :], kbuf.at[slot], sem.at[0, slot]),
                pltpu.make_async_copy(vc_hbm.at[bi, src, :], vbuf.at[slot], sem.at[1, slot]))

    for copy in fetch(n_blocks - 1, 0):
        copy.start()

    def head(h):
        return slice(h * t_new, (h + 1) * t_new), slice(h * SB_HEAD_DIM, (h + 1) * SB_HEAD_DIM)

    mask = _causal_mask(t_new)
    padding = jnp.zeros((SB_BLOCK - t_new, SB_HEAD_DIM), BF16)
    heads = [head(h) for h in range(SB_HEADS)]
    padded = lambda ref, cols: jnp.concatenate([ref[0, :, cols].astype(BF16), padding], axis=0)
    outs = _sb_blocks([q_ref[0, :, cols] for _, cols in heads], [[padded(kn_ref, cols)] for _, cols in heads],
                      [[padded(vn_ref, cols)] for _, cols in heads], w,
                      [jnp.zeros((t_new, V7X_LANES), F32)] * SB_HEADS, mask=mask)
    for (rows, _), (pv, run) in zip(heads, outs):
        acc_ref[rows, :] = pv
        run_ref[rows, :] = run

    def more(carry):
        return (carry[0] >= 0) & (carry[1] < SB_UNDERFLOW_LOG2)

    def past_block(carry):
        j = carry[0]
        slot = (n_blocks - 1 - j) % 2
        for copy in fetch(j, slot):
            copy.wait()

        @pl.when(j > 0)
        def _():
            for copy in fetch(j - 1, 1 - slot):
                copy.start()

        keys = [pl.ds(h, SB_BLOCK, stride=SB_HEADS) for h in range(SB_HEADS)]
        outs = _sb_blocks([q_ref[0, :, cols] for _, cols in heads],
                          [[kbuf[slot, rows, :].astype(BF16)] for rows in keys],
                          [[vbuf[slot, rows, :].astype(BF16)] for rows in keys], w,
                          [run_ref[rows, :] for rows, _ in heads])
        for (rows, _), (pv, run) in zip(heads, outs):
            acc_ref[rows, :] += pv
            run_ref[rows, :] = run
        return j - 1, jnp.min(run_ref[...])

    j_end, _ = lax.while_loop(more, past_block, (jnp.int32(n_blocks - 1), jnp.min(run_ref[...])))

    @pl.when(j_end >= 0)
    def _():
        for copy in fetch(j_end, (n_blocks - 1 - j_end) % 2):
            copy.wait()

    for h in range(SB_HEADS):
        rows, cols = head(h)
        o_ref[0, :, cols] = (acc_ref[rows, :] * _silu(zg_ref[0, :, cols].astype(F32))).astype(o_ref.dtype)


def _sb_sample(qz, k_new, v_new, k_past, v_past):
    _, b, t, d = qz.shape
    n_blocks = k_past.shape[1] // (SB_BLOCK * SB_HEADS)
    assert k_past.shape[1] == n_blocks * SB_BLOCK * SB_HEADS and n_blocks >= 1 and t <= V7X_LANES
    new = pl.BlockSpec((1, t, d), lambda bi: (bi, 0, 0))
    stacked = lambda which: pl.BlockSpec((None, 1, t, d), lambda bi: (which, bi, 0, 0))
    return pl.pallas_call(
        functools.partial(_sb_sample_kernel, n_blocks=n_blocks),
        grid=(b,),
        in_specs=[stacked(0), stacked(1), new, new,
                  pl.BlockSpec(memory_space=pl.ANY), pl.BlockSpec(memory_space=pl.ANY),
                  pl.BlockSpec((SB_BLOCK, SB_BLOCK), lambda bi: (0, 0))],
        out_specs=new,
        out_shape=jax.ShapeDtypeStruct(qz.shape[1:], BF16),
        scratch_shapes=[pltpu.VMEM((2, SB_BLOCK * SB_HEADS, SB_HEAD_DIM), F32),
                        pltpu.VMEM((2, SB_BLOCK * SB_HEADS, SB_HEAD_DIM), F32),
                        pltpu.SemaphoreType.DMA((2, 2)),
                        pltpu.VMEM((SB_HEADS * t, SB_HEAD_DIM), F32),
                        pltpu.VMEM((SB_HEADS * t, V7X_LANES), F32)],
        compiler_params=_params(("arbitrary",)),
        name="sb_sample",
    )(qz, qz, k_new, v_new, k_past, v_past, _sb_consts())


def _gla_kernel(*refs, chunk, n_chunks, has_state):
    if has_state:
        q_ref, k_ref, v_ref, dec_ref, r_ref, gn_ref, tril_ref, s0_ref, o_ref, sf_ref, st_ref, upd_ref = refs
    else:
        q_ref, k_ref, v_ref, dec_ref, r_ref, gn_ref, tril_ref, o_ref, sf_ref, st_ref, upd_ref = refs
    step = pl.program_id(2)

    @pl.when(step == 0)
    def _():
        st_ref[...] = s0_ref[0, 0].T if has_state else jnp.zeros_like(st_ref)

    tril2 = tril_ref[...]
    causal = (lax.broadcasted_iota(jnp.int32, (chunk, chunk), 1)
              <= lax.broadcasted_iota(jnp.int32, (chunk, chunk), 0))
    q_scale = GLA_DK ** -0.5
    nt_dot = lambda a, b: lax.dot_general(a, b, (((1,), (1,)), ((), ())), preferred_element_type=F32)
    chunks = [pl.ds(c * chunk, chunk) for c in range(n_chunks)]

    q_in, k_out, k_end, decay = [], [], [], []
    for rows in chunks:
        g_hi, g_lo = _split_bf16(dec_ref[0, rows, :])
        b = jnp.dot(tril2, jnp.concatenate([g_hi, g_lo], axis=0), preferred_element_type=F32)
        b_last = b[chunk - 1:chunk, :]
        k = k_ref[0, rows, :]
        q_in.append((q_ref[0, rows, :] * q_scale * jnp.exp(b)).astype(BF16))
        k_out.append((k * jnp.exp(-b)).astype(BF16))
        k_end.append((k * jnp.exp(b_last - b)).astype(BF16))
        decay.append(jnp.exp(b_last))
    intra = []
    for c, rows in enumerate(chunks):
        v = v_ref[0, rows, :]
        a = jnp.where(causal, nt_dot(q_in[c], k_out[c]), 0.0).astype(BF16)
        intra.append(jnp.dot(a, v, preferred_element_type=F32))
        upd_ref[c] = lax.dot_general(v, k_end[c], (((0,), (0,)), ((), ())), preferred_element_type=F32)
    for c, rows in enumerate(chunks):
        state_t = st_ref[...]
        o = intra[c] + nt_dot(q_in[c], state_t.astype(BF16))
        st_ref[...] = state_t * decay[c] + upd_ref[c]
        o = o * lax.rsqrt(jnp.mean(o * o, axis=-1, keepdims=True) + EPS) * gn_ref[...]
        o_ref[0, rows, :] = (o * _silu(r_ref[0, rows, :].astype(F32))).astype(o_ref.dtype)

    @pl.when(step == pl.num_programs(2) - 1)
    def _():
        sf_ref[0, 0] = st_ref[...].T


def _gla(qk, vr, dec, gn, s0, *, chunk, tt):
    _, b, t, _ = vr.shape
    assert t % tt == 0 and tt % chunk == 0
    tril = (lax.broadcasted_iota(jnp.int32, (chunk, chunk), 1)
            <= lax.broadcasted_iota(jnp.int32, (chunk, chunk), 0)).astype(BF16)
    tril2 = jnp.concatenate([tril, tril], axis=1)
    n_chunks = tt // chunk
    kspec = lambda off: pl.BlockSpec((1, tt, GLA_DK), lambda bi, h, s: (bi, s, h + off))
    vspec = pl.BlockSpec((1, tt, GLA_DV), lambda bi, h, s: (bi, s, h))
    sspec = pl.BlockSpec((1, 1, GLA_DK, GLA_DV), lambda bi, h, s: (bi, h, 0, 0))
    stacked = lambda which: pl.BlockSpec((None, 1, tt, GLA_DV), lambda bi, h, s: (which, bi, s, h))
    in_specs = [kspec(0), kspec(GLA_HEADS), stacked(0), kspec(0), stacked(1),
                pl.BlockSpec((1, GLA_DV), lambda bi, h, s: (0, h)),
                pl.BlockSpec((chunk, 2 * chunk), lambda bi, h, s: (0, 0))]
    args = [qk, qk, vr, dec, vr, gn.reshape(1, -1), tril2]
    if s0 is not None:
        in_specs.append(sspec)
        args.append(s0)
    return pl.pallas_call(
        functools.partial(_gla_kernel, chunk=chunk, n_chunks=n_chunks, has_state=s0 is not None),
        grid=(b, GLA_HEADS, t // tt),
        in_specs=in_specs,
        out_specs=[vspec, sspec],
        out_shape=[jax.ShapeDtypeStruct(vr.shape[1:], BF16),
                   jax.ShapeDtypeStruct((b, GLA_HEADS, GLA_DK, GLA_DV), F32)],
        scratch_shapes=[pltpu.VMEM((GLA_DV, GLA_DK), F32), pltpu.VMEM((n_chunks, GLA_DV, GLA_DK), F32)],
        compiler_params=_params(("arbitrary", "arbitrary", "arbitrary")),
        name="gla",
    )(*args)


def _outproj_kernel(y_ref, w_ref, x_ref, gate_ref, *rest, final_norm):
    if final_norm:
        gf_ref, o_ref = rest
    else:
        (o_ref,) = rest
    y = jnp.dot(y_ref[0], w_ref[...], preferred_element_type=F32)
    x = x_ref[0] + gate_ref[0] * y
    if final_norm:
        x = x * lax.rsqrt(jnp.mean(x * x, axis=-1, keepdims=True) + EPS) * gf_ref[...]
    o_ref[0] = x


def _outproj(y, w, x, gate, gf, *, tm):
    b, t, d = x.shape
    kdim = y.shape[-1]
    assert t % tm == 0
    gate_spec = (pl.BlockSpec((1, 1, d), lambda bi, i: (bi, 0, 0)) if gate.shape[1] == 1
                 else pl.BlockSpec((1, tm, d), lambda bi, i: (bi, i, 0)))
    in_specs = [pl.BlockSpec((1, tm, kdim), lambda bi, i: (bi, i, 0)),
                pl.BlockSpec((kdim, d), lambda bi, i: (0, 0)),
                pl.BlockSpec((1, tm, d), lambda bi, i: (bi, i, 0)),
                gate_spec]
    args = [y, w, x, gate]
    if gf is not None:
        in_specs.append(pl.BlockSpec((1, d), lambda bi, i: (0, 0)))
        args.append(gf.reshape(1, d))
    return pl.pallas_call(
        functools.partial(_outproj_kernel, final_norm=gf is not None),
        grid=(b, t // tm),
        in_specs=in_specs,
        out_specs=pl.BlockSpec((1, tm, d), lambda bi, i: (bi, i, 0)),
        out_shape=jax.ShapeDtypeStruct(x.shape, F32),
        compiler_params=_params(("arbitrary", "arbitrary")),
        name="outproj",
    )(*args)


def _trunk(x, mods, weights, *, per_row, tm_in, tn_in, tm_out, gla_tt, cache=None, state=None):
    (norm_g, sb_w_in, sb_w_out, gla_w_in, gla_wa, gla_wa2, gla_b_a, gla_norm_g, gla_w_out,
     final_norm_g) = weights
    b, t, d = x.shape
    if per_row:
        fold = lambda a: a.reshape(1, b * t, a.shape[-1])
        rows = lambda m: jnp.broadcast_to(m[:, None, :], (b, t, d)).reshape(1, b * t, d)
    else:
        fold = lambda a: a
        rows = lambda m: m[:, None, :]
    unfold = lambda a: a.reshape(b, t, a.shape[-1])
    width = SB_HEADS * SB_HEAD_DIM

    shift, scale, gate = mods[0]
    assert width == d
    k, v, qz = _inproj(fold(x), rows(scale), rows(shift), norm_g[0], sb_w_in, (1, 2, 0, 3), 2,
                       (SB_HEAD_DIM ** -0.5 * math.log2(math.e), None), tm=tm_in, tn=tn_in)
    k, v, qz = unfold(k), unfold(v), qz.reshape(2, b, t, d)
    if cache is None:
        branch = _sb_prompt(qz, k, v, tq=min(SB_QUERY_TILE, t))
    else:
        branch = _sb_sample(qz, k, v, cache[0], cache[1])
    x1 = _outproj(fold(branch), sb_w_out, fold(x), rows(gate), None, tm=tm_out)

    shift, scale, gate = mods[1]
    nk, nv = GLA_HEADS * GLA_DK, GLA_HEADS * GLA_DV
    assert 2 * nk == d and nv == d
    qk, vr, dec = _inproj(x1, rows(scale), rows(shift), norm_g[1], gla_w_in, (0, 1, 2), 1, (None, None),
                          tm=tm_in, tn=tn_in, decay=(gla_wa, gla_wa2, gla_b_a))
    chunk = min(GLA_CHUNK, t)
    branch, s_new = _gla(unfold(qk), vr.reshape(2, b, t, d), unfold(dec), gla_norm_g, state,
                         chunk=chunk, tt=min(gla_tt, t))
    y = _outproj(fold(branch), gla_w_out, x1, rows(gate), final_norm_g, tm=tm_out)
    return unfold(y), k, v, s_new


def kernel(x_prompt, x_sample, cache_sb_k, cache_sb_v, state_gla, c_prompt, c_sample, w_ada, b_ada, norm_g,
           sb_w_in, sb_w_out, gla_w_in, gla_w_a2, gla_b_a, gla_norm_g, gla_w_out, final_norm_g):
    depth, d, _ = w_ada.shape
    assert depth == 2 and sb_w_in.shape[0] == 1 and gla_w_in.shape[0] == 1
    bp, tp, _ = x_prompt.shape
    bs, ts, _ = x_sample.shape
    past = cache_sb_k.shape[2]
    nk, nv = GLA_HEADS * GLA_DK, GLA_HEADS * GLA_DV
    main = 2 * nk + 2 * nv

    mod = _ada(jnp.concatenate([c_prompt, c_sample], axis=0), w_ada, b_ada)
    split = lambda m: (m[:, :d], m[:, d:2 * d], m[:, 2 * d:])
    mods_p = [split(mod[l, :bp]) for l in range(depth)]
    mods_s = [split(mod[l, bp:]) for l in range(depth)]

    gla_w = gla_w_in[0]
    gla_wa = jnp.pad(gla_w[:, main:], ((0, 0), (0, V7X_LANES - GLA_GATE_RANK))).astype(BF16)
    gla_wa2 = jnp.pad(gla_w_a2[0], ((0, V7X_LANES - GLA_GATE_RANK), (0, 0))).astype(BF16)
    weights = (norm_g, sb_w_in[0].astype(BF16), sb_w_out[0].astype(BF16), gla_w.astype(BF16),
               gla_wa, gla_wa2, gla_b_a[0], gla_norm_g[0], gla_w_out[0].astype(BF16), final_norm_g)

    y_p, k_p, v_p, s_p = _trunk(x_prompt, mods_p, weights, per_row=False,
                                tm_in=min(INPROJ_ROW_TILE, tp), tn_in=INPROJ_COL_TILE,
                                tm_out=min(OUTPROJ_ROW_TILE, tp), gla_tt=GLA_TOKEN_TILE)
    cache = (cache_sb_k[0].reshape(bs, past * SB_HEADS, SB_HEAD_DIM),
             cache_sb_v[0].reshape(bs, past * SB_HEADS, SB_HEAD_DIM))
    y_s, k_s, v_s, s_s = _trunk(x_sample, mods_s, weights, per_row=True,
                                tm_in=bs * ts, tn_in=d, tm_out=bs * ts, gla_tt=ts, cache=cache,
                                state=state_gla[0])

    heads = lambda a: a.reshape(1, a.shape[0], a.shape[1], SB_HEADS, SB_HEAD_DIM)
    return (y_p, y_s, heads(k_p), heads(v_p), heads(k_s), heads(v_s), s_p[None], s_s[None])
```
